```python
import math
import jax, jax.numpy as jnp
from jax import lax
import numpy as np

D_MODEL = 1024
BATCH = 32
SEQ = 2048
DEPTH = 2

CHUNK = 64
Q_BLOCK = 128
HALF = D_MODEL // 2
FOX_HEAD_DIM = 64
FOX_HEADS = HALF // FOX_HEAD_DIM
S5_CH = HALF
S5_GROUP_CH = 16
S5_GROUPS = S5_CH // S5_GROUP_CH
S5_STATE = 64
RWKV_HEAD_DIM = 64
RWKV_HEADS = HALF // RWKV_HEAD_DIM
RWKV_DECAY_LORA = 64
RWKV_AAA_LORA = 64
RWKV_GATE_LORA = 160
MLSTM_HEADS = 4
MLSTM_HEAD_DIM = HALF // MLSTM_HEADS
MLSTM_CONV = 4
D_FF = int(round(8 * D_MODEL / 3 / 256)) * 256
FFN_CONV = 3
N_EVEN = (DEPTH + 1) // 2
N_ODD = DEPTH // 2
DN_ALPHA = (2 * DEPTH) ** 0.25
DN_BETA = (8 * DEPTH) ** -0.25
LN_EPS = 1e-5
RWKV_GN_EPS = 64e-5

EVEN_SIZES = [HALF, HALF, HALF, FOX_HEADS, S5_CH]
EVEN_PROJ = sum(EVEN_SIZES)
RWKV_SIZES = [HALF, HALF, HALF, RWKV_DECAY_LORA, RWKV_AAA_LORA, RWKV_GATE_LORA]
RWKV_PROJ = sum(RWKV_SIZES)
MLSTM_SIZES = [2 * HALF, HALF, HALF, MLSTM_HEADS, MLSTM_HEADS]
MLSTM_PROJ = sum(MLSTM_SIZES)
ODD_PROJ = RWKV_PROJ + MLSTM_PROJ

kernel_name = "hybrid_fox_s5_rwkv7_mlstm_convffn"


def _split(x, sizes):
    return jnp.split(x, np.cumsum(sizes)[:-1].tolist(), axis=-1)


def layer_norm(x, g, b):
    xf = x.astype(jnp.float32)
    mu = xf.mean(-1, keepdims=True)
    var = jnp.square(xf - mu).mean(-1, keepdims=True)
    y = (xf - mu) * lax.rsqrt(var + LN_EPS) * g.astype(jnp.float32) + b.astype(jnp.float32)
    return y.astype(x.dtype)


def causal_dwconv(x, w, b):
    width, ch = w.shape
    y = lax.conv_general_dilated(x, w[:, None, :].astype(x.dtype), (1,), [(width - 1, 0)],
                                 dimension_numbers=('NWC', 'WIO', 'NWC'),
                                 feature_group_count=ch)
    return y + b.astype(x.dtype)


def fox_attention(q, k, v, fg_logit):
    Bsz, L, H, dh = q.shape
    F = jnp.cumsum(jax.nn.log_sigmoid(fg_logit.astype(jnp.float32)), axis=1)
    F = F.transpose(0, 2, 1)
    qh, kh, vh = (t.transpose(0, 2, 1, 3) for t in (q, k, v))
    scale = dh ** -0.5
    outs = []
    for blk in range(L // Q_BLOCK):
        q0, q1 = blk * Q_BLOCK, (blk + 1) * Q_BLOCK
        s = jnp.einsum('bhtd,bhsd->bhts', qh[:, :, q0:q1], kh[:, :, :q1]).astype(jnp.float32) * scale
        s = s + F[:, :, q0:q1, None] - F[:, :, None, :q1]
        mask = (q0 + jnp.arange(Q_BLOCK))[:, None] >= jnp.arange(q1)[None, :]
        p = jax.nn.softmax(jnp.where(mask, s, -jnp.inf), axis=-1).astype(v.dtype)
        outs.append(jnp.einsum('bhts,bhsd->bthd', p, vh[:, :, :q1]))
    return jnp.concatenate(outs, axis=1)


def _s5_combine(e1, e2):
    a1r, a1i, b1r, b1i = e1
    a2r, a2i, b2r, b2i = e2
    return (a2r * a1r - a2i * a1i, a2r * a1i + a2i * a1r,
            a2r * b1r - a2i * b1i + b2r, a2r * b1i + a2i * b1r + b2i)


def s5_mixer(u, a_re, a_im, b_re, b_im, c_re, c_im, d, log_dt, w_glu, b_glu):
    f32 = jnp.float32
    Bsz, L, _ = u.shape
    G, P, Cg = S5_GROUPS, S5_STATE, S5_GROUP_CH
    a_re, a_im, b_re, b_im, c_re, c_im = (t.astype(f32) for t in (a_re, a_im, b_re, b_im, c_re, c_im))
    dt = jnp.exp(log_dt.astype(f32))[:, None]
    mag = jnp.exp(a_re * dt)
    lam_re, lam_im = mag * jnp.cos(a_im * dt), mag * jnp.sin(a_im * dt)
    den = a_re ** 2 + a_im ** 2
    zr = ((lam_re - 1.0) * a_re + lam_im * a_im) / den
    zi = (lam_im * a_re - (lam_re - 1.0) * a_im) / den
    bb_re = zr[..., None] * b_re - zi[..., None] * b_im
    bb_im = zr[..., None] * b_im + zi[..., None] * b_re
    ug = u.astype(f32).reshape(Bsz, L, G, Cg)
    bu_re = jnp.einsum('blgc,gpc->blgp', ug, bb_re)
    bu_im = jnp.einsum('blgc,gpc->blgp', ug, bb_im)
    ar = jnp.broadcast_to(lam_re, (1, L, G, P))
    ai = jnp.broadcast_to(lam_im, (1, L, G, P))
    _, _, h_re, h_im = lax.associative_scan(_s5_combine, (ar, ai, bu_re, bu_im), axis=1)
    y = (jnp.einsum('gcp,blgp->blgc', c_re, h_re) - jnp.einsum('gcp,blgp->blgc', c_im, h_im)
         + d.astype(f32).reshape(G, Cg) * ug).reshape(Bsz, L, S5_CH)
    z = jax.nn.gelu(y)
    return z * jax.nn.sigmoid(z @ w_glu.astype(f32) + b_glu.astype(f32))


def rwkv7_time_mix(p, mu, w0, w2, a0, a2, g2, k_k, k_a, r_k, ln_g, ln_b):
    f32 = jnp.float32
    Bsz, L, _ = p.shape
    H, N = RWKV_HEADS, RWKV_HEAD_DIM
    p = p.astype(f32)
    prev = jnp.pad(p[:, :-1], ((0, 0), (1, 0), (0, 0)))
    p = p + (prev - p) * mu.astype(f32)
    r, k, v, wd, ad, gd = _split(p, RWKV_SIZES)
    wlog = -jax.nn.softplus(-(w0 + jnp.tanh(wd) @ w2)) - 0.5
    decay = jnp.exp(-jnp.exp(wlog))
    a = jax.nn.sigmoid(a0 + ad @ a2)
    g = jax.nn.sigmoid(gd) @ g2
    heads = lambda t: t.reshape(Bsz, L, H, N)
    kk = heads(k * k_k)
    kk = kk * lax.rsqrt(jnp.maximum(jnp.sum(kk * kk, -1, keepdims=True), 1e-24))
    k = k * (1.0 + (a - 1.0) * k_a)
    r, k, v, decay, a = (heads(t) for t in (r, k, v, decay, a))
    tm = lambda t: jnp.moveaxis(t, 1, 0)

    def step(S, inp):
        r_t, w_t, k_t, v_t, kk_t, a_t = inp
        sa = jnp.einsum('bhij,bhj->bhi', S, -kk_t)
        S = (S * w_t[:, :, None, :] + sa[..., :, None] * (kk_t * a_t)[:, :, None, :]
             + v_t[..., :, None] * k_t[:, :, None, :])
        return S, jnp.einsum('bhij,bhj->bhi', S, r_t)

    S0 = jnp.zeros((Bsz, H, N, N), f32)
    _, o = lax.scan(step, S0, (tm(r), tm(decay), tm(k), tm(v), tm(kk), tm(a)))
    o = jnp.moveaxis(o, 0, 1)
    om = o.mean(-1, keepdims=True)
    ov = jnp.square(o - om).mean(-1, keepdims=True)
    o = (o - om) * lax.rsqrt(ov + RWKV_GN_EPS) * ln_g.reshape(H, N) + ln_b.reshape(H, N)
    bonus = jnp.sum(r * k * r_k, -1, keepdims=True) * v
    return (o + bonus).reshape(Bsz, L, H * N) * g


def mlstm_mix(p, conv_w, conv_b, i_bias, f_bias, ln_g, skip):
    f32 = jnp.float32
    Bsz, L, _ = p.shape
    H, dh = MLSTM_HEADS, MLSTM_HEAD_DIM
    qk_in, v, z, i_pre, f_pre = _split(p.astype(f32), MLSTM_SIZES)
    qk = jax.nn.silu(causal_dwconv(qk_in, conv_w.astype(f32), conv_b.astype(f32)))
    q, k = jnp.split(qk, 2, axis=-1)
    i_pre = i_pre + i_bias.astype(f32)
    lf = jax.nn.log_sigmoid(f_pre + f_bias.astype(f32))
    nc = L // CHUNK
    chunks = lambda t: t.reshape(Bsz, nc, CHUNK, H, -1).transpose(1, 0, 3, 2, 4)
    gchunks = lambda t: t.reshape(Bsz, nc, CHUNK, H).transpose(1, 0, 3, 2)
    qc, kc, vc = chunks(q), chunks(k) * dh ** -0.5, chunks(v)
    ic, fc = gchunks(i_pre), gchunks(lf)
    causal = jnp.tril(jnp.ones((CHUNK, CHUNK), bool))

    def step(carry, inp):
        C, n, m = carry
        q_c, k_c, v_c, i_c, f_c = inp
        b = jnp.cumsum(f_c, -1)
        dmat = jnp.where(causal, b[..., :, None] - b[..., None, :] + i_c[..., None, :], -jnp.inf)
        inter = b + m[..., None]
        m_t = jnp.maximum(inter, dmat.max(-1))
        w_intra = jnp.exp(dmat - m_t[..., None])
        w_inter = jnp.exp(inter - m_t)
        s = jnp.einsum('bhtd,bhsd->bhts', q_c, k_c) * w_intra
        num = (w_inter[..., None] * jnp.einsum('bhtd,bhde->bhte', q_c, C)
               + jnp.einsum('bhts,bhse->bhte', s, v_c))
        den = w_inter * jnp.einsum('bhtd,bhd->bht', q_c, n) + s.sum(-1)
        h = num / jnp.maximum(jnp.abs(den), jnp.exp(-m_t))[..., None]
        b_end = b[..., -1]
        g_s = b_end[..., None] - b + i_c
        m_new = jnp.maximum(b_end + m, g_s.max(-1))
        w_old = jnp.exp(b_end + m - m_new)
        w_s = jnp.exp(g_s - m_new[..., None])
        C = w_old[..., None, None] * C + jnp.einsum('bhs,bhsd,bhse->bhde', w_s, k_c, v_c)
        n = w_old[..., None] * n + jnp.einsum('bhs,bhsd->bhd', w_s, k_c)
        return (C, n, m_new), h

    init = (jnp.zeros((Bsz, H, dh, dh), f32), jnp.zeros((Bsz, H, dh), f32), jnp.zeros((Bsz, H), f32))
    _, h = lax.scan(step, init, (qc, kc, vc, ic, fc))
    h = h.transpose(1, 0, 3, 2, 4).reshape(Bsz, L, H, dh)
    hm = h.mean(-1, keepdims=True)
    hv = jnp.square(h - hm).mean(-1, keepdims=True)
    h = ((h - hm) * lax.rsqrt(hv + LN_EPS)).reshape(Bsz, L, H * dh) * ln_g.astype(f32)
    return (h + skip.astype(f32) * q) * jax.nn.silu(z)


def even_mixer(x, w_in, fox_fb, a_re, a_im, b_re, b_im, c_re, c_im, d, log_dt, w_glu, b_glu, w_out):
    Bsz, L, _ = x.shape
    q, k, v, fg, u = _split(x @ w_in, EVEN_SIZES)
    hd = lambda t: t.reshape(Bsz, L, FOX_HEADS, FOX_HEAD_DIM)
    fox = fox_attention(hd(q), hd(k), hd(v), fg + fox_fb).reshape(Bsz, L, HALF)
    s5 = s5_mixer(u, a_re, a_im, b_re, b_im, c_re, c_im, d, log_dt, w_glu, b_glu)
    return jnp.concatenate([fox.astype(x.dtype), s5.astype(x.dtype)], axis=-1) @ w_out


def odd_mixer(x, w_in, mu, w0, w2, a0, a2, g2, k_k, k_a, r_k, rln_g, rln_b,
              conv_w, conv_b, ib, fb, mln_g, skip, w_out):
    p_rwkv, p_mlstm = _split(x @ w_in, [RWKV_PROJ, MLSTM_PROJ])
    c = rwkv7_time_mix(p_rwkv, mu, w0, w2, a0, a2, g2, k_k, k_a, r_k, rln_g, rln_b)
    dm = mlstm_mix(p_mlstm, conv_w, conv_b, ib, fb, mln_g, skip)
    return jnp.concatenate([c.astype(x.dtype), dm.astype(x.dtype)], axis=-1) @ w_out


def conv_ffn(x, w_up, conv_w, conv_b, w_down):
    u, g = jnp.split(x @ w_up, 2, axis=-1)
    u = causal_dwconv(u, conv_w, conv_b)
    return (jax.nn.gelu(u) * g) @ w_down


def setup_inputs(seed: int = 0) -> dict:
    f32 = jnp.float32
    ks = iter(jax.random.split(jax.random.key(seed), 64))
    nrm = lambda shape, scale: scale * jax.random.normal(next(ks), shape, f32)
    uni = lambda shape, lo, hi: jax.random.uniform(next(ks), shape, f32, lo, hi)
    NE, NO = N_EVEN, N_ODD
    G, P, Cg = S5_GROUPS, S5_STATE, S5_GROUP_CH
    lin = jnp.arange(HALF, dtype=f32) / (HALF - 1)
    return {
        "x": nrm((BATCH, SEQ, D_MODEL), 1.0),
        "ev_w_in": nrm((NE, D_MODEL, EVEN_PROJ), D_MODEL ** -0.5),
        "ev_fox_fb": uni((NE, FOX_HEADS), 1.0, 4.0),
        "ev_s5_a_re": -0.5 * jnp.exp(nrm((NE, G, P), 0.05)),
        "ev_s5_a_im": jnp.pi * jnp.arange(P, dtype=f32) + nrm((NE, G, P), 0.01),
        "ev_s5_b_re": nrm((NE, G, P, Cg), (2 * Cg) ** -0.5),
        "ev_s5_b_im": nrm((NE, G, P, Cg), (2 * Cg) ** -0.5),
        "ev_s5_c_re": nrm((NE, G, Cg, P), 0.5),
        "ev_s5_c_im": nrm((NE, G, Cg, P), 0.5),
        "ev_s5_d": nrm((NE, S5_CH), 1.0),
        "ev_s5_log_dt": uni((NE, G), math.log(1e-3), math.log(1e-1)),
        "ev_s5_w_glu": nrm((NE, S5_CH, S5_CH), S5_CH ** -0.5),
        "ev_s5_b_glu": nrm((NE, S5_CH), 0.01),
        "ev_w_out": nrm((NE, D_MODEL, D_MODEL), D_MODEL ** -0.5 * DN_BETA),
        "od_w_in": nrm((NO, D_MODEL, ODD_PROJ), D_MODEL ** -0.5),
        "od_rwkv_mu": uni((NO, RWKV_PROJ), 0.0, 1.0),
        "od_rwkv_w0": -7.0 + 5.0 * lin ** 0.85 + 0.5 + nrm((NO, HALF), 0.01),
        "od_rwkv_w2": nrm((NO, RWKV_DECAY_LORA, HALF), 0.1 * RWKV_DECAY_LORA ** -0.5),
        "od_rwkv_a0": nrm((NO, HALF), 0.1),
        "od_rwkv_a2": nrm((NO, RWKV_AAA_LORA, HALF), 0.1 * RWKV_AAA_LORA ** -0.5),
        "od_rwkv_g2": nrm((NO, RWKV_GATE_LORA, HALF), RWKV_GATE_LORA ** -0.5),
        "od_rwkv_k_k": 0.85 + nrm((NO, HALF), 0.02),
        "od_rwkv_k_a": 1.0 + nrm((NO, HALF), 0.02),
        "od_rwkv_r_k": -0.04 + nrm((NO, RWKV_HEADS, RWKV_HEAD_DIM), 0.02),
        "od_rwkv_ln_g": 1.0 + nrm((NO, HALF), 0.02),
        "od_rwkv_ln_b": nrm((NO, HALF), 0.02),
        "od_mlstm_conv_w": nrm((NO, MLSTM_CONV, 2 * HALF), MLSTM_CONV ** -0.5),
        "od_mlstm_conv_b": nrm((NO, 2 * HALF), 0.01),
        "od_mlstm_ib": nrm((NO, MLSTM_HEADS), 0.1),
        "od_mlstm_fb": jnp.linspace(3.0, 6.0, MLSTM_HEADS, dtype=f32) + nrm((NO, MLSTM_HEADS), 0.01),
        "od_mlstm_ln_g": 1.0 + nrm((NO, HALF), 0.02),
        "od_mlstm_skip": 1.0 + nrm((NO, HALF), 0.02),
        "od_w_out": nrm((NO, D_MODEL, D_MODEL), D_MODEL ** -0.5 * DN_BETA),
        "ln1_g": 1.0 + nrm((DEPTH, D_MODEL), 0.02),
        "ln1_b": nrm((DEPTH, D_MODEL), 0.02),
        "ffn_w_up": nrm((DEPTH, D_MODEL, 2 * D_FF), D_MODEL ** -0.5),
        "ffn_conv_w": nrm((DEPTH, FFN_CONV, D_FF), FFN_CONV ** -0.5),
        "ffn_conv_b": nrm((DEPTH, D_FF), 0.01),
        "ffn_w_down": nrm((DEPTH, D_FF, D_MODEL), D_FF ** -0.5 * DN_BETA),
        "ln2_g": 1.0 + nrm((DEPTH, D_MODEL), 0.02),
        "ln2_b": nrm((DEPTH, D_MODEL), 0.02),
    }


def reference(x, ev_w_in, ev_fox_fb, ev_s5_a_re, ev_s5_a_im, ev_s5_b_re, ev_s5_b_im,
              ev_s5_c_re, ev_s5_c_im, ev_s5_d, ev_s5_log_dt, ev_s5_w_glu, ev_s5_b_glu, ev_w_out,
              od_w_in, od_rwkv_mu, od_rwkv_w0, od_rwkv_w2, od_rwkv_a0, od_rwkv_a2, od_rwkv_g2,
              od_rwkv_k_k, od_rwkv_k_a, od_rwkv_r_k, od_rwkv_ln_g, od_rwkv_ln_b,
              od_mlstm_conv_w, od_mlstm_conv_b, od_mlstm_ib, od_mlstm_fb, od_mlstm_ln_g,
              od_mlstm_skip, od_w_out, ln1_g, ln1_b, ffn_w_up, ffn_conv_w, ffn_conv_b,
              ffn_w_down, ln2_g, ln2_b):
    for layer in range(DEPTH):
        i = layer // 2
        if layer % 2 == 0:
            mix = even_mixer(x, ev_w_in[i], ev_fox_fb[i], ev_s5_a_re[i], ev_s5_a_im[i],
                             ev_s5_b_re[i], ev_s5_b_im[i], ev_s5_c_re[i], ev_s5_c_im[i],
                             ev_s5_d[i], ev_s5_log_dt[i], ev_s5_w_glu[i], ev_s5_b_glu[i], ev_w_out[i])
        else:
            mix = odd_mixer(x, od_w_in[i], od_rwkv_mu[i], od_rwkv_w0[i], od_rwkv_w2[i],
                            od_rwkv_a0[i], od_rwkv_a2[i], od_rwkv_g2[i], od_rwkv_k_k[i],
                            od_rwkv_k_a[i], od_rwkv_r_k[i], od_rwkv_ln_g[i], od_rwkv_ln_b[i],
                            od_mlstm_conv_w[i], od_mlstm_conv_b[i], od_mlstm_ib[i], od_mlstm_fb[i],
                            od_mlstm_ln_g[i], od_mlstm_skip[i], od_w_out[i])
        x = layer_norm(DN_ALPHA * x + mix, ln1_g[layer], ln1_b[layer])
        ffn = conv_ffn(x, ffn_w_up[layer], ffn_conv_w[layer], ffn_conv_b[layer], ffn_w_down[layer])
        x = layer_norm(DN_ALPHA * x + ffn, ln2_g[layer], ln2_b[layer])
    return x
```

```python
import functools
import math

import jax
import jax.numpy as jnp
from jax import lax
from jax.experimental import pallas as pl
from jax.experimental.pallas import tpu as pltpu

F32 = jnp.float32
BF16 = jnp.bfloat16
HIGHEST = lax.Precision.HIGHEST

D_MODEL = 1024
HALF = D_MODEL // 2
FOX_HEADS = 8
FOX_DH = HALF // FOX_HEADS
FOX_BLOCK = 128
S5_GROUPS = 32
S5_GROUP_CH = 16
S5_STATE = 64
S5_CHUNK = 16
RWKV_HEADS = 8
RWKV_N = HALF // RWKV_HEADS
RWKV_DECAY_LORA = 64
RWKV_AAA_LORA = 64
RWKV_GATE_LORA = 160
RWKV_GN_EPS = 64e-5
MLSTM_HEADS = 4
MLSTM_DH = HALF // MLSTM_HEADS
MLSTM_CONV = 4
CHUNK = 64
D_FF = 2816
FFN_CONV = 3
FFN_TILE = 256
LN_EPS = 1e-5
HALO = 8
NEG_BIG = -1e30
V7X_VMEM_LIMIT_BYTES = 56 * 1024 * 1024


def _cparams(*sem):
    return pltpu.CompilerParams(dimension_semantics=sem, vmem_limit_bytes=V7X_VMEM_LIMIT_BYTES)


def _resident(shape):
    nd = len(shape)
    return pl.BlockSpec(shape, lambda *_: (0,) * nd, pipeline_mode=pl.Buffered(1))


def _rows(tm, width):
    return pl.BlockSpec((tm, width), lambda i: (i, 0))


def _halo_rows(tm, width):
    return pl.BlockSpec((HALO, width), lambda i: (jnp.maximum(i * (tm // HALO) - 1, 0), 0))


def _mm(a, b):
    return jnp.dot(a.astype(BF16), b.astype(BF16), preferred_element_type=F32)


def _dg(a, b, ca, cb):
    return lax.dot_general(a, b, (((ca,), (cb,)), ((), ())), preferred_element_type=F32)


def _hi_lo(a):
    hi = a.astype(BF16)
    lo = (a - hi.astype(F32)).astype(BF16)
    return hi, lo


def _mm3(a, b, ca=1, cb=0):
    ah, al = _hi_lo(a)
    bh, bl = _hi_lo(b)
    return _dg(ah, bh, ca, cb) + _dg(ah, bl, ca, cb) + _dg(al, bh, ca, cb)


def _split3(a):
    a1 = a.astype(BF16)
    r1 = a - a1.astype(F32)
    a2 = r1.astype(BF16)
    a3 = (r1 - a2.astype(F32)).astype(BF16)
    return a1, a2, a3


def _mm_exact_rhs(a, b01):
    a1, a2, a3 = _split3(a)
    return (jnp.dot(a1, b01, preferred_element_type=F32) + jnp.dot(a2, b01, preferred_element_type=F32)
            + jnp.dot(a3, b01, preferred_element_type=F32))


def _mm_exact_lhs(a01, b):
    b1, b2, b3 = _split3(b)
    return (jnp.dot(a01, b1, preferred_element_type=F32) + jnp.dot(a01, b2, preferred_element_type=F32)
            + jnp.dot(a01, b3, preferred_element_type=F32))


def _layer_norm(x, g, b):
    mu = jnp.mean(x, -1, keepdims=True)
    d = x - mu
    var = jnp.mean(d * d, -1, keepdims=True)
    return d * lax.rsqrt(var + LN_EPS) * g + b


def _iota2(shape):
    return lax.broadcasted_iota(jnp.int32, shape, 0), lax.broadcasted_iota(jnp.int32, shape, 1)


def _even_pre_kernel(x_ref, wq_ref, wk_ref, wv_ref, wf_ref, wu_ref, q_ref, k_ref, v_ref, fg_ref, u_ref):
    xb = x_ref[...].astype(BF16)
    q_ref[...] = (jnp.dot(xb, wq_ref[...], preferred_element_type=F32) * (FOX_DH ** -0.5)).astype(BF16)
    k_ref[...] = jnp.dot(xb, wk_ref[...], preferred_element_type=F32).astype(BF16)
    v_ref[...] = jnp.dot(xb, wv_ref[...], preferred_element_type=F32).astype(BF16)
    u_ref[...] = jnp.dot(xb, wu_ref[...], preferred_element_type=F32).astype(BF16)
    fg_ref[...] = jnp.dot(xb, wf_ref[...], preferred_element_type=F32)[:, :FOX_HEADS]


def _even_pre(x, w_in, tm=512):
    T = x.shape[0]
    wb = w_in.astype(BF16)
    wq, wk, wv = wb[:, :HALF], wb[:, HALF:2 * HALF], wb[:, 2 * HALF:3 * HALF]
    wf = jnp.pad(wb[:, 3 * HALF:3 * HALF + FOX_HEADS], ((0, 0), (0, 128 - FOX_HEADS)))
    wu = wb[:, 3 * HALF + FOX_HEADS:]
    half_out = jax.ShapeDtypeStruct((T, HALF), BF16)
    return pl.pallas_call(
        _even_pre_kernel,
        grid=(T // tm,),
        in_specs=[_rows(tm, D_MODEL), _resident(wq.shape), _resident(wk.shape), _resident(wv.shape),
                  _resident(wf.shape), _resident(wu.shape)],
        out_specs=[_rows(tm, HALF), _rows(tm, HALF), _rows(tm, HALF), _rows(tm, FOX_HEADS), _rows(tm, HALF)],
        out_shape=[half_out, half_out, half_out, jax.ShapeDtypeStruct((T, FOX_HEADS), F32), half_out],
        compiler_params=_cparams("parallel"),
        name="even_pre",
    )(x, wq, wk, wv, wf, wu)


def _fox_gate_kernel(fg_ref, fb_ref, o_ref, *, L):
    ls = jax.nn.log_sigmoid(fg_ref[...] + fb_ref[...])
    r, c = _iota2((128, 128))
    tri = (r <= c).astype(BF16)
    carry = jnp.zeros((FOX_HEADS, 1), F32)
    for j in range(L // 128):
        cum = _mm_exact_rhs(ls[:, j * 128:(j + 1) * 128], tri) + carry
        o_ref[:, j * 128:(j + 1) * 128] = cum
        carry = cum[:, 127:128]


def _fox_gate(fg_t, fb):
    B, H, L = fg_t.shape
    return pl.pallas_call(
        functools.partial(_fox_gate_kernel, L=L),
        grid=(B,),
        in_specs=[pl.BlockSpec((None, H, L), lambda b: (b, 0, 0)), _resident((H, 1))],
        out_specs=pl.BlockSpec((None, H, L), lambda b: (b, 0, 0)),
        out_shape=jax.ShapeDtypeStruct((B, H, L), F32),
        compiler_params=_cparams("parallel"),
        name="fox_gate",
    )(fg_t, fb.reshape(H, 1))


def _fox_kernel(q_ref, k_ref, v_ref, fc_ref, fr_ref, o_ref, *, L):
    TQ = FOX_BLOCK
    lane = lax.broadcasted_iota(jnp.int32, (1, 128), 1)
    lo = lane < FOX_DH
    r, c = _iota2((TQ, TQ))
    causal = c <= r

    for hp in range(FOX_HEADS // 2):
        ls = slice(128 * hp, 128 * (hp + 1))

        def q_block(i, _, hp=hp, ls=ls):
            r0 = pl.multiple_of(i * TQ, TQ)
            q = q_ref[pl.ds(r0, TQ), ls]
            zero = jnp.zeros_like(q)
            qm = (jnp.where(lo, q, zero), jnp.where(lo, zero, q))
            fcol = tuple(fc_ref[pl.ds(r0, TQ), 2 * hp + hh:2 * hp + hh + 1] for hh in range(2))

            def step(c0, carry, masked):
                kj = k_ref[pl.ds(c0, TQ), ls]
                vj = v_ref[pl.ds(c0, TQ), ls]
                new = []
                for hh in range(2):
                    m, l, acc = carry[hh]
                    s = _dg(qm[hh], kj, 1, 1)
                    frow = fr_ref[2 * hp + hh:2 * hp + hh + 1, pl.ds(c0, TQ)]
                    s = s + (fcol[hh] - frow)
                    if masked:
                        s = jnp.where(causal, s, NEG_BIG)
                    mn = jnp.maximum(m, jnp.max(s, -1, keepdims=True))
                    a = jnp.exp(m - mn)
                    p = jnp.exp(s - mn)
                    l = a * l + jnp.sum(p, -1, keepdims=True)
                    acc = a * acc + jnp.dot(p.astype(BF16), vj, preferred_element_type=F32)
                    new.append((mn, l, acc))
                return tuple(new)

            init = tuple((jnp.full((TQ, 1), NEG_BIG, F32), jnp.zeros((TQ, 1), F32), jnp.zeros((TQ, 128), F32))
                         for _ in range(2))
            carry = lax.fori_loop(0, i, lambda j, cr: step(pl.multiple_of(j * TQ, TQ), cr, False), init)
            carry = step(r0, carry, True)
            o0 = carry[0][2] / carry[0][1]
            o1 = carry[1][2] / carry[1][1]
            o_ref[pl.ds(r0, TQ), ls] = jnp.where(lo, o0, o1).astype(BF16)
            return 0

        lax.fori_loop(0, L // TQ, q_block, 0)


def _fox_attention(q, k, v, f_col, f_row):
    B, H, L = f_row.shape
    seq = pl.BlockSpec((L, HALF), lambda b: (b, 0))
    return pl.pallas_call(
        functools.partial(_fox_kernel, L=L),
        grid=(B,),
        in_specs=[seq, seq, seq, pl.BlockSpec((L, H), lambda b: (b, 0)),
                  pl.BlockSpec((None, H, L), lambda b: (b, 0, 0))],
        out_specs=seq,
        out_shape=jax.ShapeDtypeStruct((B * L, HALF), BF16),
        compiler_params=_cparams("parallel"),
        name="fox_attention",
    )(q, k, v, f_col, f_row)


def _s5_matrices(a_re, a_im, b_re, b_im, c_re, c_im, d, log_dt):
    G, P, Cg, LC = S5_GROUPS, S5_STATE, S5_GROUP_CH, S5_CHUNK
    a_re, a_im, b_re, b_im, c_re, c_im = (t.astype(F32) for t in (a_re, a_im, b_re, b_im, c_re, c_im))
    dt = jnp.exp(log_dt.astype(F32))[:, None]
    mag = jnp.exp(a_re * dt)
    lam_re, lam_im = mag * jnp.cos(a_im * dt), mag * jnp.sin(a_im * dt)
    den = a_re ** 2 + a_im ** 2
    zr = ((lam_re - 1.0) * a_re + lam_im * a_im) / den
    zi = (lam_im * a_re - (lam_re - 1.0) * a_im) / den
    bb_re = zr[..., None] * b_re - zi[..., None] * b_im
    bb_im = zr[..., None] * b_im + zi[..., None] * b_re
    n = jnp.arange(LC + 1, dtype=F32)[:, None, None]
    pw_mag = jnp.exp(n * (a_re * dt)[None])
    pr, pi = pw_mag * jnp.cos(n * (a_im * dt)[None]), pw_mag * jnp.sin(n * (a_im * dt)[None])
    ein = functools.partial(jnp.einsum, precision=HIGHEST)
    cr = c_re[None] * pr[:, :, None, :] - c_im[None] * pi[:, :, None, :]
    ci = -(c_re[None] * pi[:, :, None, :] + c_im[None] * pr[:, :, None, :])
    kn = ein('ngcp,gpd->ngcd', cr[:LC], bb_re) + ein('ngcp,gpd->ngcd', ci[:LC], bb_im)
    s_idx = jnp.arange(LC)
    lag = s_idx[None, :] - s_idx[:, None]
    kt = jnp.where((lag >= 0)[:, :, None, None, None],
                   kn[jnp.clip(lag, 0, LC - 1)], 0.0)
    dmat = (lag == 0)[:, :, None, None, None] * (d.astype(F32).reshape(1, 1, G, Cg, 1)
                                                 * jnp.eye(Cg, dtype=F32)[None, None, None])
    kmat = (kt + dmat).transpose(2, 0, 4, 1, 3).reshape(G, LC * Cg, LC * Cg)
    rr, ri = pr[LC - 1 - s_idx], pi[LC - 1 - s_idx]
    e_re = rr[..., None] * bb_re[None] - ri[..., None] * bb_im[None]
    e_im = rr[..., None] * bb_im[None] + ri[..., None] * bb_re[None]
    emat = jnp.concatenate([e_re, e_im], axis=2).transpose(1, 0, 3, 2).reshape(G, LC * Cg, 2 * P)
    fmat = jnp.concatenate([cr[1:], ci[1:]], axis=3)
    fmat = fmat.transpose(1, 3, 0, 2).reshape(G, 2 * P, LC * Cg)
    lam_a = jnp.concatenate([pr[LC], pr[LC]], axis=-1)[:, None, :]
    lam_b = jnp.concatenate([-pi[LC], pi[LC]], axis=-1)[:, None, :]
    return kmat.astype(BF16), emat.astype(BF16), fmat.astype(BF16), lam_a, lam_b


def _s5_kernel(u_ref, k_ref, e_ref, f_ref, la_ref, lb_ref, y_ref, e_s, h_s, *, NK, B, TR):
    R = NK * B
    for t in range(R // TR):
        rs = slice(t * TR, (t + 1) * TR)
        e_s[rs, :] = jnp.dot(u_ref[rs, :], e_ref[...], preferred_element_type=F32)
    la, lb = la_ref[...], lb_ref[...]

    def body(kc, h):
        r0 = pl.multiple_of(kc * B, B)
        h_s[pl.ds(r0, B), :] = h
        return la * h + lb * pltpu.roll(h, S5_STATE, 1) + e_s[pl.ds(r0, B), :]

    lax.fori_loop(0, NK, body, jnp.zeros((B, 2 * S5_STATE), F32))
    for t in range(R // TR):
        rs = slice(t * TR, (t + 1) * TR)
        y_ref[rs, :] = (jnp.dot(u_ref[rs, :], k_ref[...], preferred_element_type=F32)
                        + jnp.dot(h_s[rs, :].astype(BF16), f_ref[...], preferred_element_type=F32))


def _s5(u, mats, B, L):
    kmat, emat, fmat, lam_a, lam_b = mats
    G, Cg, LC, P2 = S5_GROUPS, S5_GROUP_CH, S5_CHUNK, 2 * S5_STATE
    NK = L // LC
    R, W = NK * B, LC * Cg
    ug = u.reshape(B, NK, LC, G, Cg).transpose(3, 1, 0, 2, 4).reshape(G, R, W)
    TR = min(R, 512)
    per_g = lambda a, b: pl.BlockSpec((None, a, b), lambda g: (g, 0, 0))
    y = pl.pallas_call(
        functools.partial(_s5_kernel, NK=NK, B=B, TR=TR),
        grid=(G,),
        in_specs=[per_g(R, W), per_g(W, W), per_g(W, P2), per_g(P2, W), per_g(1, P2), per_g(1, P2)],
        out_specs=per_g(R, W),
        out_shape=jax.ShapeDtypeStruct((G, R, W), F32),
        scratch_shapes=[pltpu.VMEM((R, P2), F32), pltpu.VMEM((R, P2), F32)],
        compiler_params=_cparams("parallel"),
        name="s5",
    )(ug, kmat, emat, fmat, lam_a, lam_b)
    return y.reshape(G, NK, B, LC, Cg).transpose(2, 1, 3, 0, 4).reshape(B * L, HALF)


def _mix_kernel(a_ref, b_ref, x_ref, wt_ref, wb_ref, g_ref, beta_ref, *rest, even, alpha):
    if even:
        wg_ref, bg_ref, o_ref = rest
        z = jax.nn.gelu(b_ref[...])
        second = z * jax.nn.sigmoid(_mm(z, wg_ref[...]) + bg_ref[...])
    else:
        (o_ref,) = rest
        second = b_ref[...]
    mix = _mm(a_ref[...], wt_ref[...]) + _mm(second, wb_ref[...])
    o_ref[...] = _layer_norm(alpha * x_ref[...] + mix, g_ref[...], beta_ref[...])


def _mix_out(a, b, x, w_out, ln_g, ln_b, alpha, glu=None, tm=512):
    T = x.shape[0]
    wb16 = w_out.astype(BF16)
    args = [a, b, x, wb16[:HALF], wb16[HALF:], ln_g.reshape(1, D_MODEL), ln_b.reshape(1, D_MODEL)]
    specs = [_rows(tm, HALF), _rows(tm, HALF), _rows(tm, D_MODEL), _resident((HALF, D_MODEL)),
             _resident((HALF, D_MODEL)), _resident((1, D_MODEL)), _resident((1, D_MODEL))]
    if glu is not None:
        w_glu, b_glu = glu
        args += [w_glu.astype(BF16), b_glu.reshape(1, HALF).astype(F32)]
        specs += [_resident((HALF, HALF)), _resident((1, HALF))]
    return pl.pallas_call(
        functools.partial(_mix_kernel, even=glu is not None, alpha=alpha),
        grid=(T // tm,),
        in_specs=specs,
        out_specs=_rows(tm, D_MODEL),
        out_shape=jax.ShapeDtypeStruct((T, D_MODEL), F32),
        compiler_params=_cparams("parallel"),
        name="mix_out",
    )(*args)


def _ffn_kernel(x_ref, xh_ref, wu_ref, cw_ref, cb_ref, wd_ref, g_ref, beta_ref, o_ref, *, tiles_per_seq, alpha):
    tm = x_ref.shape[0]
    first = (pl.program_id(0) % tiles_per_seq) == 0
    x = x_ref[...]
    xh = jnp.where(first, 0.0, xh_ref[...])
    xe = jnp.concatenate([xh, x], axis=0).astype(BF16)
    xb = xe[HALO:]
    acc = jnp.zeros((tm, D_MODEL), F32)
    for c in range(D_FF // FFN_TILE):
        cs = slice(c * FFN_TILE, (c + 1) * FFN_TILE)
        gs = slice(D_FF + c * FFN_TILE, D_FF + (c + 1) * FFN_TILE)
        ue = jnp.dot(xe, wu_ref[:, cs], preferred_element_type=F32)
        gate = jnp.dot(xb, wu_ref[:, gs], preferred_element_type=F32)
        cw = cw_ref[:, cs]
        u = (cw[2:3] * ue[HALO:] + cw[1:2] * ue[HALO - 1:HALO - 1 + tm] + cw[0:1] * ue[HALO - 2:HALO - 2 + tm]
             + cb_ref[:, cs])
        acc = acc + _mm(jax.nn.gelu(u) * gate, wd_ref[cs, :])
    o_ref[...] = _layer_norm(alpha * x + acc, g_ref[...], beta_ref[...])


def _conv_ffn(x, L, w_up, conv_w, conv_b, w_down, ln_g, ln_b, alpha, tm=512):
    T = x.shape[0]
    tm = min(tm, L)
    return pl.pallas_call(
        functools.partial(_ffn_kernel, tiles_per_seq=L // tm, alpha=alpha),
        grid=(T // tm,),
        in_specs=[_rows(tm, D_MODEL), _halo_rows(tm, D_MODEL), _resident((D_MODEL, 2 * D_FF)),
                  _resident((FFN_CONV, D_FF)), _resident((1, D_FF)), _resident((D_FF, D_MODEL)),
                  _resident((1, D_MODEL)), _resident((1, D_MODEL))],
        out_specs=_rows(tm, D_MODEL),
        out_shape=jax.ShapeDtypeStruct((T, D_MODEL), F32),
        compiler_params=_cparams("parallel"),
        name="conv_ffn",
    )(x, x, w_up.astype(BF16), conv_w.astype(F32), conv_b.reshape(1, D_FF).astype(F32),
      w_down.astype(BF16), ln_g.reshape(1, D_MODEL), ln_b.reshape(1, D_MODEL))


_LORA_PAD = (128, 128, 256)
_RWKV_PAD = 3 * HALF + sum(_LORA_PAD)


def _odd_pre_kernel(x_ref, xh_ref, wrk_ref, mu_ref, w2_ref, a2_ref, g2_ref, vec_ref, bd_ref,
                    wqk_ref, cw_ref, cb_ref, wv_ref, wz_ref, wif_ref, gb_ref,
                    r_ref, k_ref, v_ref, lw_ref, kk_ref, ka_ref, g_ref, bo_ref,
                    mq_ref, mk_ref, mv_ref, mz_ref, gate_ref, *, tiles_per_seq):
    tm = x_ref.shape[0]
    first = (pl.program_id(0) % tiles_per_seq) == 0
    xh = jnp.where(first, 0.0, xh_ref[...])
    xe = jnp.concatenate([xh, x_ref[...]], axis=0).astype(BF16)
    xb = xe[HALO:]

    pe = jnp.dot(xe, wrk_ref[...], preferred_element_type=F32)
    cur, prev = pe[HALO:], pe[HALO - 1:HALO - 1 + tm]
    p = cur + (prev - cur) * mu_ref[...]
    r, k, v = p[:, :HALF], p[:, HALF:2 * HALF], p[:, 2 * HALF:3 * HALF]
    o0 = 3 * HALF
    wd = p[:, o0:o0 + _LORA_PAD[0]]
    ad = p[:, o0 + _LORA_PAD[0]:o0 + _LORA_PAD[0] + _LORA_PAD[1]]
    gd = p[:, o0 + _LORA_PAD[0] + _LORA_PAD[1]:]
    w0, a0, k_k, k_a, r_k = (vec_ref[i:i + 1, :] for i in range(5))
    wlog = -jax.nn.softplus(-(w0 + _mm(jnp.tanh(wd), w2_ref[...]))) - 0.5
    lw_ref[...] = -jnp.exp(wlog)
    a = jax.nn.sigmoid(a0 + _mm(ad, a2_ref[...]))
    g_ref[...] = _mm(jax.nn.sigmoid(gd), g2_ref[...])
    kk = k * k_k
    ss = _mm_exact_rhs(kk * kk, bd_ref[...])
    kk = kk * lax.rsqrt(jnp.maximum(ss, 1e-24))
    kmod = k * (1.0 + (a - 1.0) * k_a)
    bo_ref[...] = _mm_exact_rhs(r * kmod * r_k, bd_ref[...]) * v
    r_ref[...] = r
    k_ref[...] = kmod
    v_ref[...] = v
    kk_ref[...] = kk
    ka_ref[...] = kk * a

    qke = jnp.dot(xe, wqk_ref[...], preferred_element_type=F32)
    cw = cw_ref[...]
    qk = cb_ref[...]
    for j in range(MLSTM_CONV):
        off = HALO - (MLSTM_CONV - 1) + j
        qk = qk + cw[j:j + 1] * qke[off:off + tm]
    qk = jax.nn.silu(qk)
    mq_ref[...] = qk[:, :HALF]
    mk_ref[...] = (qk[:, HALF:] * (MLSTM_DH ** -0.5)).astype(BF16)
    mv_ref[...] = jnp.dot(xb, wv_ref[...], preferred_element_type=F32).astype(BF16)
    mz_ref[...] = jnp.dot(xb, wz_ref[...], preferred_element_type=F32)
    pre = jnp.dot(xb, wif_ref[...], preferred_element_type=F32)[:, :2 * MLSTM_HEADS] + gb_ref[...]
    is_i = lax.broadcasted_iota(jnp.int32, pre.shape, 1) < MLSTM_HEADS
    gate_ref[...] = jnp.where(is_i, pre, jax.nn.log_sigmoid(pre))


def _head_block_ones(width, head):
    idx = jnp.arange(width) // head
    return (idx[:, None] == idx[None, :]).astype(BF16)


def _odd_pre(x, L, w_in, mu, w0, w2, a0, a2, g2, k_k, k_a, r_k, conv_w, conv_b, ib, fb, tm=256):
    T = x.shape[0]
    tm = min(tm, L)
    wb = w_in.astype(BF16)
    o = 3 * HALF
    sizes = (RWKV_DECAY_LORA, RWKV_AAA_LORA, RWKV_GATE_LORA)

    def pad_lora(m, axis):
        parts, s = [], o
        for sz, pd in zip(sizes, _LORA_PAD):
            piece = lax.slice_in_dim(m, s, s + sz, axis=axis)
            widths = [(0, 0)] * m.ndim
            widths[axis] = (0, pd - sz)
            parts.append(jnp.pad(piece, widths))
            s += sz
        return jnp.concatenate([lax.slice_in_dim(m, 0, o, axis=axis)] + parts, axis=axis)

    rwkv_proj = o + sum(sizes)
    wrk = pad_lora(wb[:, :rwkv_proj], 1)
    mu_p = pad_lora(mu.astype(F32).reshape(1, -1), 1)
    padr = lambda m, rows: jnp.pad(m.astype(BF16), ((0, rows - m.shape[0]), (0, 0)))
    w2p, a2p, g2p = padr(w2, _LORA_PAD[0]), padr(a2, _LORA_PAD[1]), padr(g2, _LORA_PAD[2])
    vecs = jnp.stack([w0, a0, k_k, k_a, r_k.reshape(HALF)]).astype(F32)
    vecs = jnp.pad(vecs, ((0, 8 - vecs.shape[0]), (0, 0)))
    bd = _head_block_ones(HALF, RWKV_N)
    wm = wb[:, rwkv_proj:]
    wqk, wmv, wmz = wm[:, :2 * HALF], wm[:, 2 * HALF:3 * HALF], wm[:, 3 * HALF:4 * HALF]
    wif = jnp.pad(wm[:, 4 * HALF:], ((0, 0), (0, 128 - 2 * MLSTM_HEADS)))
    gbias = jnp.concatenate([ib, fb]).astype(F32).reshape(1, 2 * MLSTM_HEADS)
    args = [x, x, wrk, mu_p, w2p, a2p, g2p, vecs, bd, wqk, conv_w.astype(F32),
            conv_b.astype(F32).reshape(1, 2 * HALF), wmv, wmz, wif, gbias]
    specs = [_rows(tm, D_MODEL), _halo_rows(tm, D_MODEL)] + [_resident(a.shape) for a in args[2:]]
    f_half = jax.ShapeDtypeStruct((T, HALF), F32)
    b_half = jax.ShapeDtypeStruct((T, HALF), BF16)
    out_shape = [f_half] * 8 + [f_half, b_half, b_half, f_half, jax.ShapeDtypeStruct((T, 2 * MLSTM_HEADS), F32)]
    out_specs = [_rows(tm, HALF)] * 12 + [_rows(tm, 2 * MLSTM_HEADS)]
    return pl.pallas_call(
        functools.partial(_odd_pre_kernel, tiles_per_seq=L // tm),
        grid=(T // tm,),
        in_specs=specs,
        out_specs=out_specs,
        out_shape=out_shape,
        compiler_params=_cparams("parallel"),
        name="odd_pre",
    )(*args)


def _rwkv_kernel(r_ref, k_ref, v_ref, lw_ref, kk_ref, ka_ref, g_ref, bo_ref, lng_ref, lnb_ref, o_ref, m_s):
    C, N = CHUNK, RWKV_N

    @pl.when(pl.program_id(1) == 0)
    def _():
        m_s[...] = jnp.zeros_like(m_s)

    ri, ci = _iota2((C, C))
    incl, strict, eye = ci <= ri, ci < ri, ci == ri
    lw = lw_ref[...]
    cs = _mm_exact_lhs(incl.astype(BF16), lw)
    cend = cs[C - 1:C, :]
    e_neg = jnp.exp(-cs)
    e_rem = jnp.exp(cend - cs)
    r, k, v, kk, ka = r_ref[...], k_ref[...], v_ref[...], kk_ref[...], ka_ref[...]
    al = -kk * jnp.exp(cs - lw)
    rt = r * jnp.exp(cs)
    bt, kt = ka * e_neg, k * e_neg
    bp, kp = ka * e_rem, k * e_rem
    gam = jnp.exp(cend)
    zeros = jnp.zeros((C, N), F32)
    outs = []
    for h in range(RWKV_HEADS):
        sl = slice(N * h, N * (h + 1))
        vh = v[:, sl]
        pm = _mm3(jnp.concatenate([al[:, sl], rt[:, sl]], 0), jnp.concatenate([bt[:, sl], kt[:, sl]], 0), 1, 1)
        a_ab = jnp.where(strict, pm[:C, :C], 0.0)
        a_ak = jnp.where(strict, pm[:C, C:], 0.0)
        a_rb = jnp.where(incl, pm[C:, :C], 0.0)
        a_rk = jnp.where(incl, pm[C:, C:], 0.0)
        w = jnp.concatenate([al[:, sl], _mm3(a_ak, vh)], 1)
        npow = a_ab
        levels = int(math.log2(C))
        for lvl in range(levels):
            w = w + _mm3(npow, w)
            if lvl < levels - 1:
                npow = _mm3(npow, npow)
        zv = jnp.concatenate([zeros, vh], 1)
        qt = _mm3(jnp.concatenate([a_rb, a_rk], 1), jnp.concatenate([w, zv], 0))
        qb = _mm3(bp[:, sl], w, 0, 0) + _mm3(kp[:, sl], zv, 0, 0)
        m0 = m_s[h]
        o_h = _mm3(rt[:, sl] + qt[:, :N], m0) + qt[:, N:]
        m_s[h] = _mm3(qb[:, :N] + jnp.where(eye, gam[:, sl], 0.0), m0) + qb[:, N:]
        om = jnp.mean(o_h, -1, keepdims=True)
        d = o_h - om
        ov = jnp.mean(d * d, -1, keepdims=True)
        outs.append(d * lax.rsqrt(ov + RWKV_GN_EPS))
    on = jnp.concatenate(outs, 1) * lng_ref[...] + lnb_ref[...]
    o_ref[...] = ((on + bo_ref[...]) * g_ref[...]).astype(BF16)


def _rwkv(r, k, v, lw, kk, ka, g, bonus, ln_g, ln_b, B, L):
    NC = L // CHUNK
    blk = pl.BlockSpec((CHUNK, HALF), lambda b, c: (b * NC + c, 0))
    vec = pl.BlockSpec((1, HALF), lambda b, c: (0, 0))
    return pl.pallas_call(
        _rwkv_kernel,
        grid=(B, NC),
        in_specs=[blk] * 8 + [vec, vec],
        out_specs=blk,
        out_shape=jax.ShapeDtypeStruct((B * L, HALF), BF16),
        scratch_shapes=[pltpu.VMEM((RWKV_HEADS, RWKV_N, RWKV_N), F32)],
        compiler_params=_cparams("parallel", "arbitrary"),
        name="rwkv7",
    )(r, k, v, lw, kk, ka, g, bonus, ln_g.reshape(1, HALF).astype(F32), ln_b.reshape(1, HALF).astype(F32))


def _mlstm_kernel(q_ref, k_ref, v_ref, z_ref, gc_ref, gr_ref, lng_ref, skip_ref, o_ref, c_s, m_s):
    C, H, DH = CHUNK, MLSTM_HEADS, MLSTM_DH

    @pl.when(pl.program_id(1) == 0)
    def _():
        c_s[...] = jnp.zeros_like(c_s)
        m_s[...] = jnp.zeros_like(m_s)

    ri, ci = _iota2((C, C))
    incl = ci <= ri
    gc, gr = gc_ref[...], gr_ref[...]
    b_cols = _mm_exact_lhs(incl.astype(BF16), gc)
    b_rows = _mm_exact_rhs(gr, (ri <= ci).astype(BF16))
    lane = lax.broadcasted_iota(jnp.int32, (1, DH), 1)
    ones_col = jnp.where(lane == 0, 1.0, 0.0).astype(BF16)
    q_all, z_all = q_ref[...], z_ref[...]
    outs = []
    for h in range(H):
        sl = slice(DH * h, DH * (h + 1))
        q, kh, vh = q_all[:, sl], k_ref[:, sl], v_ref[:, sl]
        qb = q.astype(BF16)
        vaug = jnp.concatenate([vh, jnp.broadcast_to(ones_col, (C, DH))], 1)
        i_col, i_row = gc[:, h:h + 1], gr[h:h + 1, :]
        b_col, b_row = b_cols[:, H + h:H + h + 1], b_rows[H + h:H + h + 1, :]
        m_prev = m_s[h:h + 1, 0:1]
        dmat = jnp.where(incl, b_col - b_row + i_row, NEG_BIG)
        inter = b_col + m_prev
        m_t = jnp.maximum(inter, jnp.max(dmat, -1, keepdims=True))
        w_intra = jnp.exp(dmat - m_t)
        w_inter = jnp.exp(inter - m_t)
        s = _dg(qb, kh, 1, 1) * w_intra
        nd = w_inter * jnp.dot(qb, c_s[h].astype(BF16), preferred_element_type=F32) \
            + jnp.dot(s.astype(BF16), vaug, preferred_element_type=F32)
        den = nd[:, DH:DH + 1]
        hid = nd[:, :DH] / jnp.maximum(jnp.abs(den), jnp.exp(-m_t))
        b_end = b_col[C - 1:C, :]
        g_row = b_end - b_row + i_row
        g_col = b_end - b_col + i_col
        m_new = jnp.maximum(b_end + m_prev, jnp.max(g_row, -1, keepdims=True))
        w_old = jnp.exp(b_end + m_prev - m_new)
        kw = (jnp.exp(g_col - m_new) * kh.astype(F32)).astype(BF16)
        c_s[h] = w_old * c_s[h] + _dg(kw, vaug, 0, 0)
        m_s[h:h + 1, :] = jnp.broadcast_to(m_new, (1, 128))
        hm = jnp.mean(hid, -1, keepdims=True)
        d = hid - hm
        hv = jnp.mean(d * d, -1, keepdims=True)
        outs.append(d * lax.rsqrt(hv + LN_EPS))
    hn = jnp.concatenate(outs, 1) * lng_ref[...]
    o_ref[...] = ((hn + skip_ref[...] * q_all) * jax.nn.silu(z_all)).astype(BF16)


def _mlstm(q, k, v, z, gates, ln_g, skip, B, L):
    NC, H2 = L // CHUNK, 2 * MLSTM_HEADS
    blk = pl.BlockSpec((CHUNK, HALF), lambda b, c: (b * NC + c, 0))
    vec = pl.BlockSpec((1, HALF), lambda b, c: (0, 0))
    g_rows = gates.reshape(B, NC, CHUNK, H2).transpose(0, 1, 3, 2)
    return pl.pallas_call(
        _mlstm_kernel,
        grid=(B, NC),
        in_specs=[blk, blk, blk, blk, pl.BlockSpec((CHUNK, H2), lambda b, c: (b * NC + c, 0)),
                  pl.BlockSpec((None, None, H2, CHUNK), lambda b, c: (b, c, 0, 0)), vec, vec],
        out_specs=blk,
        out_shape=jax.ShapeDtypeStruct((B * L, HALF), BF16),
        scratch_shapes=[pltpu.VMEM((MLSTM_HEADS, MLSTM_DH, 2 * MLSTM_DH), F32), pltpu.VMEM((8, 128), F32)],
        compiler_params=_cparams("parallel", "arbitrary"),
        name="mlstm",
    )(q, k, v, z, gates, g_rows, ln_g.reshape(1, HALF).astype(F32), skip.reshape(1, HALF).astype(F32))


def kernel(x, ev_w_in, ev_fox_fb, ev_s5_a_re, ev_s5_a_im, ev_s5_b_re, ev_s5_b_im, ev_s5_c_re, ev_s5_c_im, ev_s5_d, ev_s5_log_dt, ev_s5_w_glu, ev_s5_b_glu, ev_w_out, od_w_in, od_rwkv_mu, od_rwkv_w0, od_rwkv_w2, od_rwkv_a0, od_rwkv_a2, od_rwkv_g2, od_rwkv_k_k, od_rwkv_k_a, od_rwkv_r_k, od_rwkv_ln_g, od_rwkv_ln_b, od_mlstm_conv_w, od_mlstm_conv_b, od_mlstm_ib, od_mlstm_fb, od_mlstm_ln_g, od_mlstm_skip, od_w_out, ln1_g, ln1_b, ffn_w_up, ffn_conv_w, ffn_conv_b, ffn_w_down, ln2_g, ln2_b):
    B, L, _ = x.shape
    depth = ln1_g.shape[0]
    alpha = float((2 * depth) ** 0.25)
    h = x.reshape(B * L, D_MODEL).astype(F32)
    for layer in range(depth):
        i = layer // 2
        if layer % 2 == 0:
            q, k, v, fg, u = _even_pre(h, ev_w_in[i])
            f_row = _fox_gate(fg.reshape(B, L, FOX_HEADS).transpose(0, 2, 1), ev_fox_fb[i].astype(F32))
            f_col = f_row.transpose(0, 2, 1).reshape(B * L, FOX_HEADS)
            fox = _fox_attention(q, k, v, f_col, f_row)
            mats = _s5_matrices(ev_s5_a_re[i], ev_s5_a_im[i], ev_s5_b_re[i], ev_s5_b_im[i],
                                ev_s5_c_re[i], ev_s5_c_im[i], ev_s5_d[i], ev_s5_log_dt[i])
            y = _s5(u, mats, B, L)
            h = _mix_out(fox, y, h, ev_w_out[i], ln1_g[layer], ln1_b[layer], alpha,
                         glu=(ev_s5_w_glu[i], ev_s5_b_glu[i]))
        else:
            (r, k, v, lw, kk, ka, g, bonus, mq, mk, mv, mz, gates) = _odd_pre(
                h, L, od_w_in[i], od_rwkv_mu[i], od_rwkv_w0[i], od_rwkv_w2[i], od_rwkv_a0[i], od_rwkv_a2[i],
                od_rwkv_g2[i], od_rwkv_k_k[i], od_rwkv_k_a[i], od_rwkv_r_k[i], od_mlstm_conv_w[i],
                od_mlstm_conv_b[i], od_mlstm_ib[i], od_mlstm_fb[i])
            c = _rwkv(r, k, v, lw, kk, ka, g, bonus, od_rwkv_ln_g[i], od_rwkv_ln_b[i], B, L)
            dm = _mlstm(mq, mk, mv, mz, gates, od_mlstm_ln_g[i], od_mlstm_skip[i], B, L)
            h = _mix_out(c, dm, h, od_w_out[i], ln1_g[layer], ln1_b[layer], alpha)
        h = _conv_ffn(h, L, ffn_w_up[layer], ffn_conv_w[layer], ffn_conv_b[layer], ffn_w_down[layer],
                      ln2_g[layer], ln2_b[layer], alpha)
    return h.reshape(B, L, D_MODEL).astype(x.dtype)
```

```python
import functools
import math

import jax
import jax.numpy as jnp
from jax import lax
from jax.experimental import pallas as pl
from jax.experimental.pallas import tpu as pltpu

F32 = jnp.float32
BF16 = jnp.bfloat16
HIGHEST = lax.Precision.HIGHEST

D_MODEL = 1024
HALF = D_MODEL // 2
FOX_HEADS = 8
FOX_DH = HALF // FOX_HEADS
FOX_TILE = 256
FOX_AUG = 256
S5_GROUPS = 32
S5_GROUP_CH = 16
S5_STATE = 64
S5_CHUNK = 16
RWKV_HEADS = 8
RWKV_N = HALF // RWKV_HEADS
RWKV_DECAY_LORA = 64
RWKV_AAA_LORA = 64
RWKV_GATE_LORA = 160
RWKV_GN_EPS = 64e-5
RWKV_PASSES = 1
RWKV_STATE_PASSES = 3
MLSTM_HEADS = 4
MLSTM_DH = HALF // MLSTM_HEADS
MLSTM_CONV = 4
CHUNK = 64
D_FF = 2816
FFN_CONV = 3
FFN_TILE = 256
LN_EPS = 1e-5
HALO = 8
NEG_BIG = -1e30
V7X_VMEM_LIMIT_BYTES = 56 * 1024 * 1024


def _cparams(*sem):
    return pltpu.CompilerParams(dimension_semantics=sem, vmem_limit_bytes=V7X_VMEM_LIMIT_BYTES)


def _resident(shape):
    nd = len(shape)
    return pl.BlockSpec(shape, lambda *_: (0,) * nd, pipeline_mode=pl.Buffered(1))


def _rows(tm, width):
    return pl.BlockSpec((tm, width), lambda i: (i, 0))


def _halo_rows(tm, width):
    return pl.BlockSpec((HALO, width), lambda i: (jnp.maximum(i * (tm // HALO) - 1, 0), 0))


def _mm(a, b):
    return jnp.dot(a.astype(BF16), b.astype(BF16), preferred_element_type=F32)


def _dg(a, b, ca, cb):
    return lax.dot_general(a, b, (((ca,), (cb,)), ((), ())), preferred_element_type=F32)


def _hi_lo(a):
    hi = a.astype(BF16)
    lo = (a - hi.astype(F32)).astype(BF16)
    return hi, lo


def _mm3(a, b, ca=1, cb=0):
    ah, al = _hi_lo(a)
    bh, bl = _hi_lo(b)
    return _dg(ah, bh, ca, cb) + _dg(ah, bl, ca, cb) + _dg(al, bh, ca, cb)


def _mmp(a, b, ca=1, cb=0, passes=1):
    if passes == 3:
        return _mm3(a, b, ca, cb)
    return _dg(a.astype(BF16), b.astype(BF16), ca, cb)


def _split3(a):
    a1 = a.astype(BF16)
    r1 = a - a1.astype(F32)
    a2 = r1.astype(BF16)
    a3 = (r1 - a2.astype(F32)).astype(BF16)
    return a1, a2, a3


def _mm_exact_rhs(a, b01):
    a1, a2, a3 = _split3(a)
    return (jnp.dot(a1, b01, preferred_element_type=F32) + jnp.dot(a2, b01, preferred_element_type=F32)
            + jnp.dot(a3, b01, preferred_element_type=F32))


def _mm_exact_lhs(a01, b):
    b1, b2, b3 = _split3(b)
    return (jnp.dot(a01, b1, preferred_element_type=F32) + jnp.dot(a01, b2, preferred_element_type=F32)
            + jnp.dot(a01, b3, preferred_element_type=F32))


def _layer_norm(x, g, b):
    mu = jnp.mean(x, -1, keepdims=True)
    d = x - mu
    var = jnp.mean(d * d, -1, keepdims=True)
    return d * lax.rsqrt(var + LN_EPS) * g + b


def _iota2(shape):
    return lax.broadcasted_iota(jnp.int32, shape, 0), lax.broadcasted_iota(jnp.int32, shape, 1)


def _even_pre_kernel(x_ref, wqt_ref, wk_ref, wvt_ref, wf_ref, wu_ref, qt_ref, k_ref, vt_ref, fg_ref, u_ref):
    xb = x_ref[...].astype(BF16)
    qt_ref[...] = (_dg(wqt_ref[...], xb, 1, 1) * (FOX_DH ** -0.5)).astype(BF16)
    vt_ref[...] = _dg(wvt_ref[...], xb, 1, 1).astype(BF16)
    k_ref[...] = jnp.dot(xb, wk_ref[...], preferred_element_type=F32).astype(BF16)
    u_ref[...] = jnp.dot(xb, wu_ref[...], preferred_element_type=F32).astype(BF16)
    fg_ref[...] = jnp.dot(xb, wf_ref[...], preferred_element_type=F32)[:, :FOX_HEADS]


def _even_pre(x, w_in, tm=512):
    T = x.shape[0]
    wb = w_in.astype(BF16)
    wqt, wk, wvt = wb[:, :HALF].T, wb[:, HALF:2 * HALF], wb[:, 2 * HALF:3 * HALF].T
    wf = jnp.pad(wb[:, 3 * HALF:3 * HALF + FOX_HEADS], ((0, 0), (0, 128 - FOX_HEADS)))
    wu = wb[:, 3 * HALF + FOX_HEADS:]
    half_out = jax.ShapeDtypeStruct((T, HALF), BF16)
    half_t = jax.ShapeDtypeStruct((HALF, T), BF16)
    cols = pl.BlockSpec((HALF, tm), lambda i: (0, i))
    return pl.pallas_call(
        _even_pre_kernel,
        grid=(T // tm,),
        in_specs=[_rows(tm, D_MODEL), _resident(wqt.shape), _resident(wk.shape), _resident(wvt.shape),
                  _resident(wf.shape), _resident(wu.shape)],
        out_specs=[cols, _rows(tm, HALF), cols, _rows(tm, FOX_HEADS), _rows(tm, HALF)],
        out_shape=[half_t, half_out, half_t, jax.ShapeDtypeStruct((T, FOX_HEADS), F32), half_out],
        compiler_params=_cparams("parallel"),
        name="even_pre",
    )(x, wqt, wk, wvt, wf, wu)


def _fox_gate_kernel(fg_ref, fb_ref, o_ref, *, L):
    ls = jax.nn.log_sigmoid(fg_ref[...] + fb_ref[...])
    r, c = _iota2((128, 128))
    tri = (r <= c).astype(BF16)
    carry = jnp.zeros((FOX_HEADS, 1), F32)
    for j in range(L // 128):
        cum = _mm_exact_rhs(ls[:, j * 128:(j + 1) * 128], tri) + carry
        o_ref[:, j * 128:(j + 1) * 128] = cum
        carry = cum[:, 127:128]


def _fox_gate(fg_t, fb):
    B, H, L = fg_t.shape
    return pl.pallas_call(
        functools.partial(_fox_gate_kernel, L=L),
        grid=(B,),
        in_specs=[pl.BlockSpec((None, H, L), lambda b: (b, 0, 0)), _resident((H, 1))],
        out_specs=pl.BlockSpec((None, H, L), lambda b: (b, 0, 0)),
        out_shape=jax.ShapeDtypeStruct((B, H, L), F32),
        compiler_params=_cparams("parallel"),
        name="fox_gate",
    )(fg_t, fb.reshape(H, 1))


def _fox_kernel(qt_ref, k_ref, vt_ref, fc_ref, fr_ref, o_ref, qa_s, ka_s, m_s, l_s, acc_s, *, L):
    TQ = TK = FOX_TILE
    H, DH, KA = FOX_HEADS, FOX_DH, FOX_AUG
    heads = range(H)

    t1, t2, t3 = (t.astype(F32) for t in _split3(fr_ref[...]))
    r16 = lax.broadcasted_iota(jnp.int32, (16, 1), 0)
    upper = lax.broadcasted_iota(jnp.int32, (2 * DH, 1), 0) < DH
    for h in heads:
        p, hh = divmod(h, 2)
        qpair = qt_ref[2 * DH * p:2 * DH * (p + 1), :]
        keep = upper if hh == 0 else jnp.logical_not(upper)
        qa_s[h, 0:2 * DH, :] = jnp.where(keep, qpair, jnp.zeros_like(qpair))
        ones_rows = (r16 >= 3 + 3 * hh) & (r16 < 6 + 3 * hh)
        blk = jnp.where(r16 == 0, t1[h:h + 1], jnp.where(r16 == 1, t2[h:h + 1], jnp.where(
            r16 == 2, t3[h:h + 1], jnp.where(ones_rows, 1.0, 0.0))))
        qa_s[h, 2 * DH:2 * DH + 16, :] = blk.astype(BF16)
        qa_s[h, 2 * DH + 16:KA, :] = jnp.zeros((KA - 2 * DH - 16, L), BF16)
    c1, c2, c3 = _split3(fc_ref[...])
    rs, cs_ = _iota2((128, 128))
    lane = lax.broadcasted_iota(jnp.int32, (1, 128), 1)
    ones3 = jnp.where(lane < 3, 1.0, 0.0)
    for p in range(H // 2):
        def sel(i, p=p):
            hit = ((rs == 2 * p) & (cs_ == 3 + i)) | ((rs == 2 * p + 1) & (cs_ == 6 + i))
            return jnp.where(hit, -1.0, 0.0).astype(BF16)
        aug = (jnp.dot(c1, sel(0), preferred_element_type=F32) + jnp.dot(c2, sel(1), preferred_element_type=F32)
               + jnp.dot(c3, sel(2), preferred_element_type=F32) + ones3)
        ka_s[:, KA * p:KA * p + 2 * DH] = k_ref[:, 2 * DH * p:2 * DH * (p + 1)]
        ka_s[:, KA * p + 2 * DH:KA * (p + 1)] = aug.astype(BF16)

    ri, ci = _iota2((TK, TQ))
    visible = ri <= ci

    def q_block(qi, _):
        q0 = pl.multiple_of(qi * TQ, TQ)
        m_s[...] = jnp.full(m_s.shape, NEG_BIG, F32)
        l_s[...] = jnp.zeros(l_s.shape, F32)
        acc_s[...] = jnp.zeros(acc_s.shape, F32)

        def tile(k0, masked):
            kt = [ka_s[pl.ds(k0, TK), KA * p:KA * (p + 1)] for p in range(H // 2)]
            st = [jnp.dot(kt[h // 2], qa_s[h, :, pl.ds(q0, TQ)], preferred_element_type=F32) for h in heads]
            if masked:
                st = [jnp.where(visible, s, NEG_BIG) for s in st]
            m_old = [m_s[h:h + 1, :] for h in heads]
            m_new = [jnp.maximum(m_old[h], jnp.max(st[h], 0, keepdims=True)) for h in heads]
            pt = [jnp.exp(st[h] - m_new[h]) for h in heads]
            pv = [jnp.dot(vt_ref[DH * h:DH * (h + 1), pl.ds(k0, TK)], pt[h].astype(BF16),
                          preferred_element_type=F32) for h in heads]
            for h in heads:
                a = jnp.exp(m_old[h] - m_new[h])
                m_s[h:h + 1, :] = m_new[h]
                l_s[h:h + 1, :] = a * l_s[h:h + 1, :] + jnp.sum(pt[h], 0, keepdims=True)
                acc_s[DH * h:DH * (h + 1), :] = a * acc_s[DH * h:DH * (h + 1), :] + pv[h]

        def k_step(j, carry):
            tile(pl.multiple_of(j * TK, TK), False)
            return carry

        lax.fori_loop(0, qi, k_step, 0)
        tile(q0, True)
        for p in range(H // 2):
            o_pair = jnp.concatenate([acc_s[DH * h:DH * (h + 1), :] / l_s[h:h + 1, :] for h in (2 * p, 2 * p + 1)], 0)
            o_ref[pl.ds(q0, TQ), 2 * DH * p:2 * DH * (p + 1)] = o_pair.T.astype(BF16)
        return 0

    lax.fori_loop(0, L // TQ, q_block, 0)


def _fox_attention(qt, k, vt, f_col, f_row):
    B, H, L = f_row.shape
    seq = pl.BlockSpec((L, HALF), lambda b: (b, 0))
    seq_t = pl.BlockSpec((HALF, L), lambda b: (0, b))
    return pl.pallas_call(
        functools.partial(_fox_kernel, L=L),
        grid=(B,),
        in_specs=[seq_t, seq, seq_t, pl.BlockSpec((L, 128), lambda b: (b, 0)),
                  pl.BlockSpec((None, H, L), lambda b: (b, 0, 0))],
        out_specs=seq,
        out_shape=jax.ShapeDtypeStruct((B * L, HALF), BF16),
        scratch_shapes=[pltpu.VMEM((H, FOX_AUG, L), BF16), pltpu.VMEM((L, FOX_AUG * H // 2), BF16),
                        pltpu.VMEM((H, FOX_TILE), F32), pltpu.VMEM((H, FOX_TILE), F32),
                        pltpu.VMEM((HALF, FOX_TILE), F32)],
        compiler_params=_cparams("parallel"),
        name="fox_attention",
    )(qt, k, vt, f_col, f_row)


def _s5_matrices(a_re, a_im, b_re, b_im, c_re, c_im, d, log_dt):
    G, P, Cg, LC = S5_GROUPS, S5_STATE, S5_GROUP_CH, S5_CHUNK
    a_re, a_im, b_re, b_im, c_re, c_im = (t.astype(F32) for t in (a_re, a_im, b_re, b_im, c_re, c_im))
    dt = jnp.exp(log_dt.astype(F32))[:, None]
    mag = jnp.exp(a_re * dt)
    lam_re, lam_im = mag * jnp.cos(a_im * dt), mag * jnp.sin(a_im * dt)
    den = a_re ** 2 + a_im ** 2
    zr = ((lam_re - 1.0) * a_re + lam_im * a_im) / den
    zi = (lam_im * a_re - (lam_re - 1.0) * a_im) / den
    bb_re = zr[..., None] * b_re - zi[..., None] * b_im
    bb_im = zr[..., None] * b_im + zi[..., None] * b_re
    n = jnp.arange(LC + 1, dtype=F32)[:, None, None]
    pw_mag = jnp.exp(n * (a_re * dt)[None])
    pr, pi = pw_mag * jnp.cos(n * (a_im * dt)[None]), pw_mag * jnp.sin(n * (a_im * dt)[None])
    ein = functools.partial(jnp.einsum, precision=HIGHEST)
    cr = c_re[None] * pr[:, :, None, :] - c_im[None] * pi[:, :, None, :]
    ci = -(c_re[None] * pi[:, :, None, :] + c_im[None] * pr[:, :, None, :])
    kn = ein('ngcp,gpd->ngcd', cr[:LC], bb_re) + ein('ngcp,gpd->ngcd', ci[:LC], bb_im)
    s_idx = jnp.arange(LC)
    lag = s_idx[None, :] - s_idx[:, None]
    kt = jnp.where((lag >= 0)[:, :, None, None, None],
                   kn[jnp.clip(lag, 0, LC - 1)], 0.0)
    dmat = (lag == 0)[:, :, None, None, None] * (d.astype(F32).reshape(1, 1, G, Cg, 1)
                                                 * jnp.eye(Cg, dtype=F32)[None, None, None])
    kmat = (kt + dmat).transpose(2, 0, 4, 1, 3).reshape(G, LC * Cg, LC * Cg)
    rr, ri = pr[LC - 1 - s_idx], pi[LC - 1 - s_idx]
    e_re = rr[..., None] * bb_re[None] - ri[..., None] * bb_im[None]
    e_im = rr[..., None] * bb_im[None] + ri[..., None] * bb_re[None]
    emat = jnp.concatenate([e_re, e_im], axis=2).transpose(1, 0, 3, 2).reshape(G, LC * Cg, 2 * P)
    fmat = jnp.concatenate([cr[1:], ci[1:]], axis=3)
    fmat = fmat.transpose(1, 3, 0, 2).reshape(G, 2 * P, LC * Cg)
    lam_a = jnp.concatenate([pr[LC], pr[LC]], axis=-1)[:, None, :]
    lam_b = jnp.concatenate([-pi[LC], pi[LC]], axis=-1)[:, None, :]
    return kmat.astype(BF16), emat.astype(BF16), fmat.astype(BF16), lam_a, lam_b


def _s5_kernel(u_ref, k_ref, e_ref, f_ref, la_ref, lb_ref, y_ref, e_s, h_s, *, NK, B, TR):
    R = NK * B
    for t in range(R // TR):
        rs = slice(t * TR, (t + 1) * TR)
        e_s[rs, :] = jnp.dot(u_ref[rs, :], e_ref[...], preferred_element_type=F32)
    la, lb = la_ref[...], lb_ref[...]

    def body(kc, h):
        r0 = pl.multiple_of(kc * B, B)
        h_s[pl.ds(r0, B), :] = h
        return la * h + lb * pltpu.roll(h, S5_STATE, 1) + e_s[pl.ds(r0, B), :]

    lax.fori_loop(0, NK, body, jnp.zeros((B, 2 * S5_STATE), F32))
    for t in range(R // TR):
        rs = slice(t * TR, (t + 1) * TR)
        y_ref[rs, :] = (jnp.dot(u_ref[rs, :], k_ref[...], preferred_element_type=F32)
                        + jnp.dot(h_s[rs, :].astype(BF16), f_ref[...], preferred_element_type=F32))


def _s5(u, mats, B, L):
    kmat, emat, fmat, lam_a, lam_b = mats
    G, Cg, LC, P2 = S5_GROUPS, S5_GROUP_CH, S5_CHUNK, 2 * S5_STATE
    NK = L // LC
    R, W = NK * B, LC * Cg
    ug = u.reshape(B, NK, LC, G, Cg).transpose(3, 1, 0, 2, 4).reshape(G, R, W)
    TR = min(R, 512)
    per_g = lambda a, b: pl.BlockSpec((None, a, b), lambda g: (g, 0, 0))
    y = pl.pallas_call(
        functools.partial(_s5_kernel, NK=NK, B=B, TR=TR),
        grid=(G,),
        in_specs=[per_g(R, W), per_g(W, W), per_g(W, P2), per_g(P2, W), per_g(1, P2), per_g(1, P2)],
        out_specs=per_g(R, W),
        out_shape=jax.ShapeDtypeStruct((G, R, W), F32),
        scratch_shapes=[pltpu.VMEM((R, P2), F32), pltpu.VMEM((R, P2), F32)],
        compiler_params=_cparams("parallel"),
        name="s5",
    )(ug, kmat, emat, fmat, lam_a, lam_b)
    return y.reshape(G, NK, B, LC, Cg).transpose(2, 1, 3, 0, 4).reshape(B * L, HALF)


def _mix_kernel(a_ref, b_ref, x_ref, wt_ref, wb_ref, g_ref, beta_ref, *rest, even, alpha):
    if even:
        wg_ref, bg_ref, o_ref = rest
        z = jax.nn.gelu(b_ref[...])
        second = z * jax.nn.sigmoid(_mm(z, wg_ref[...]) + bg_ref[...])
    else:
        (o_ref,) = rest
        second = b_ref[...]
    mix = _mm(a_ref[...], wt_ref[...]) + _mm(second, wb_ref[...])
    o_ref[...] = _layer_norm(alpha * x_ref[...] + mix, g_ref[...], beta_ref[...])


def _mix_out(a, b, x, w_out, ln_g, ln_b, alpha, glu=None, tm=512):
    T = x.shape[0]
    wb16 = w_out.astype(BF16)
    args = [a, b, x, wb16[:HALF], wb16[HALF:], ln_g.reshape(1, D_MODEL), ln_b.reshape(1, D_MODEL)]
    specs = [_rows(tm, HALF), _rows(tm, HALF), _rows(tm, D_MODEL), _resident((HALF, D_MODEL)),
             _resident((HALF, D_MODEL)), _resident((1, D_MODEL)), _resident((1, D_MODEL))]
    if glu is not None:
        w_glu, b_glu = glu
        args += [w_glu.astype(BF16), b_glu.reshape(1, HALF).astype(F32)]
        specs += [_resident((HALF, HALF)), _resident((1, HALF))]
    return pl.pallas_call(
        functools.partial(_mix_kernel, even=glu is not None, alpha=alpha),
        grid=(T // tm,),
        in_specs=specs,
        out_specs=_rows(tm, D_MODEL),
        out_shape=jax.ShapeDtypeStruct((T, D_MODEL), F32),
        compiler_params=_cparams("parallel"),
        name="mix_out",
    )(*args)


def _ffn_kernel(x_ref, xh_ref, wu_ref, cw_ref, cb_ref, wd_ref, g_ref, beta_ref, o_ref, *, tiles_per_seq, alpha):
    tm = x_ref.shape[0]
    first = (pl.program_id(0) % tiles_per_seq) == 0
    x = x_ref[...]
    xh = jnp.where(first, 0.0, xh_ref[...])
    xe = jnp.concatenate([xh, x], axis=0).astype(BF16)
    xb = xe[HALO:]
    acc = jnp.zeros((tm, D_MODEL), F32)
    for c in range(D_FF // FFN_TILE):
        cs = slice(c * FFN_TILE, (c + 1) * FFN_TILE)
        gs = slice(D_FF + c * FFN_TILE, D_FF + (c + 1) * FFN_TILE)
        ue = jnp.dot(xe, wu_ref[:, cs], preferred_element_type=F32)
        gate = jnp.dot(xb, wu_ref[:, gs], preferred_element_type=F32)
        cw = cw_ref[:, cs]
        u = (cw[2:3] * ue[HALO:] + cw[1:2] * ue[HALO - 1:HALO - 1 + tm] + cw[0:1] * ue[HALO - 2:HALO - 2 + tm]
             + cb_ref[:, cs])
        acc = acc + _mm(jax.nn.gelu(u) * gate, wd_ref[cs, :])
    o_ref[...] = _layer_norm(alpha * x + acc, g_ref[...], beta_ref[...])


def _conv_ffn(x, L, w_up, conv_w, conv_b, w_down, ln_g, ln_b, alpha, tm=512):
    T = x.shape[0]
    tm = min(tm, L)
    return pl.pallas_call(
        functools.partial(_ffn_kernel, tiles_per_seq=L // tm, alpha=alpha),
        grid=(T // tm,),
        in_specs=[_rows(tm, D_MODEL), _halo_rows(tm, D_MODEL), _resident((D_MODEL, 2 * D_FF)),
                  _resident((FFN_CONV, D_FF)), _resident((1, D_FF)), _resident((D_FF, D_MODEL)),
                  _resident((1, D_MODEL)), _resident((1, D_MODEL))],
        out_specs=_rows(tm, D_MODEL),
        out_shape=jax.ShapeDtypeStruct((T, D_MODEL), F32),
        compiler_params=_cparams("parallel"),
        name="conv_ffn",
    )(x, x, w_up.astype(BF16), conv_w.astype(F32), conv_b.reshape(1, D_FF).astype(F32),
      w_down.astype(BF16), ln_g.reshape(1, D_MODEL), ln_b.reshape(1, D_MODEL))


_LORA_PAD = (128, 128, 256)
_RWKV_PAD = 3 * HALF + sum(_LORA_PAD)


def _odd_pre_kernel(x_ref, xh_ref, wrk_ref, mu_ref, w2_ref, a2_ref, g2_ref, vec_ref, bd_ref,
                    wqk_ref, cw_ref, cb_ref, wv_ref, wz_ref, wif_ref, gb_ref,
                    r_ref, k_ref, v_ref, lw_ref, kk_ref, ka_ref, g_ref, bo_ref,
                    mq_ref, mk_ref, mv_ref, mz_ref, gate_ref, *, tiles_per_seq):
    tm = x_ref.shape[0]
    first = (pl.program_id(0) % tiles_per_seq) == 0
    xh = jnp.where(first, 0.0, xh_ref[...])
    xe = jnp.concatenate([xh, x_ref[...]], axis=0).astype(BF16)
    xb = xe[HALO:]

    pe = jnp.dot(xe, wrk_ref[...], preferred_element_type=F32)
    cur, prev = pe[HALO:], pe[HALO - 1:HALO - 1 + tm]
    p = cur + (prev - cur) * mu_ref[...]
    r, k, v = p[:, :HALF], p[:, HALF:2 * HALF], p[:, 2 * HALF:3 * HALF]
    o0 = 3 * HALF
    wd = p[:, o0:o0 + _LORA_PAD[0]]
    ad = p[:, o0 + _LORA_PAD[0]:o0 + _LORA_PAD[0] + _LORA_PAD[1]]
    gd = p[:, o0 + _LORA_PAD[0] + _LORA_PAD[1]:]
    w0, a0, k_k, k_a, r_k = (vec_ref[i:i + 1, :] for i in range(5))
    wlog = -jax.nn.softplus(-(w0 + _mm(jnp.tanh(wd), w2_ref[...]))) - 0.5
    lw_ref[...] = -jnp.exp(wlog)
    a = jax.nn.sigmoid(a0 + _mm(ad, a2_ref[...]))
    g_ref[...] = _mm(jax.nn.sigmoid(gd), g2_ref[...])
    kk = k * k_k
    ss = _mm_exact_rhs(kk * kk, bd_ref[...])
    kk = kk * lax.rsqrt(jnp.maximum(ss, 1e-24))
    kmod = k * (1.0 + (a - 1.0) * k_a)
    bo_ref[...] = _mm_exact_rhs(r * kmod * r_k, bd_ref[...]) * v
    r_ref[...] = r
    k_ref[...] = kmod
    v_ref[...] = v
    kk_ref[...] = kk
    ka_ref[...] = kk * a

    qke = jnp.dot(xe, wqk_ref[...], preferred_element_type=F32)
    cw = cw_ref[...]
    qk = cb_ref[...]
    for j in range(MLSTM_CONV):
        off = HALO - (MLSTM_CONV - 1) + j
        qk = qk + cw[j:j + 1] * qke[off:off + tm]
    qk = jax.nn.silu(qk)
    mq_ref[...] = qk[:, :HALF]
    mk_ref[...] = (qk[:, HALF:] * (MLSTM_DH ** -0.5)).astype(BF16)
    mv_ref[...] = jnp.dot(xb, wv_ref[...], preferred_element_type=F32).astype(BF16)
    mz_ref[...] = jnp.dot(xb, wz_ref[...], preferred_element_type=F32)
    pre = jnp.dot(xb, wif_ref[...], preferred_element_type=F32)[:, :2 * MLSTM_HEADS] + gb_ref[...]
    is_i = lax.broadcasted_iota(jnp.int32, pre.shape, 1) < MLSTM_HEADS
    gate_ref[...] = jnp.where(is_i, pre, jax.nn.log_sigmoid(pre))


def _head_block_ones(width, head):
    idx = jnp.arange(width) // head
    return (idx[:, None] == idx[None, :]).astype(BF16)


def _odd_pre(x, L, w_in, mu, w0, w2, a0, a2, g2, k_k, k_a, r_k, conv_w, conv_b, ib, fb, tm=256):
    T = x.shape[0]
    tm = min(tm, L)
    wb = w_in.astype(BF16)
    o = 3 * HALF
    sizes = (RWKV_DECAY_LORA, RWKV_AAA_LORA, RWKV_GATE_LORA)

    def pad_lora(m, axis):
        parts, s = [], o
        for sz, pd in zip(sizes, _LORA_PAD):
            piece = lax.slice_in_dim(m, s, s + sz, axis=axis)
            widths = [(0, 0)] * m.ndim
            widths[axis] = (0, pd - sz)
            parts.append(jnp.pad(piece, widths))
            s += sz
        return jnp.concatenate([lax.slice_in_dim(m, 0, o, axis=axis)] + parts, axis=axis)

    rwkv_proj = o + sum(sizes)
    wrk = pad_lora(wb[:, :rwkv_proj], 1)
    mu_p = pad_lora(mu.astype(F32).reshape(1, -1), 1)
    padr = lambda m, rows: jnp.pad(m.astype(BF16), ((0, rows - m.shape[0]), (0, 0)))
    w2p, a2p, g2p = padr(w2, _LORA_PAD[0]), padr(a2, _LORA_PAD[1]), padr(g2, _LORA_PAD[2])
    vecs = jnp.stack([w0, a0, k_k, k_a, r_k.reshape(HALF)]).astype(F32)
    vecs = jnp.pad(vecs, ((0, 8 - vecs.shape[0]), (0, 0)))
    bd = _head_block_ones(HALF, RWKV_N)
    wm = wb[:, rwkv_proj:]
    wqk, wmv, wmz = wm[:, :2 * HALF], wm[:, 2 * HALF:3 * HALF], wm[:, 3 * HALF:4 * HALF]
    wif = jnp.pad(wm[:, 4 * HALF:], ((0, 0), (0, 128 - 2 * MLSTM_HEADS)))
    gbias = jnp.concatenate([ib, fb]).astype(F32).reshape(1, 2 * MLSTM_HEADS)
    args = [x, x, wrk, mu_p, w2p, a2p, g2p, vecs, bd, wqk, conv_w.astype(F32),
            conv_b.astype(F32).reshape(1, 2 * HALF), wmv, wmz, wif, gbias]
    specs = [_rows(tm, D_MODEL), _halo_rows(tm, D_MODEL)] + [_resident(a.shape) for a in args[2:]]
    f_half = jax.ShapeDtypeStruct((T, HALF), F32)
    b_half = jax.ShapeDtypeStruct((T, HALF), BF16)
    out_shape = [f_half] * 8 + [f_half, b_half, b_half, f_half, jax.ShapeDtypeStruct((T, 2 * MLSTM_HEADS), F32)]
    out_specs = [_rows(tm, HALF)] * 12 + [_rows(tm, 2 * MLSTM_HEADS)]
    return pl.pallas_call(
        functools.partial(_odd_pre_kernel, tiles_per_seq=L // tm),
        grid=(T // tm,),
        in_specs=specs,
        out_specs=out_specs,
        out_shape=out_shape,
        compiler_params=_cparams("parallel"),
        name="odd_pre",
    )(*args)


def _rwkv_kernel(r_ref, k_ref, v_ref, lw_ref, kk_ref, ka_ref, g_ref, bo_ref, lng_ref, lnb_ref, o_ref, m_s):
    C, N, H = CHUNK, RWKV_N, RWKV_HEADS
    heads = range(H)

    @pl.when(pl.program_id(1) == 0)
    def _():
        m_s[...] = jnp.zeros_like(m_s)

    ri, ci = _iota2((C, C))
    incl, strict, eye = ci <= ri, ci < ri, ci == ri
    lw = lw_ref[...]
    cs = _mm_exact_lhs(incl.astype(BF16), lw)
    cend = cs[C - 1:C, :]
    e_neg = jnp.exp(-cs)
    e_rem = jnp.exp(cend - cs)
    r, k, v, kk, ka = r_ref[...], k_ref[...], v_ref[...], kk_ref[...], ka_ref[...]
    al = -kk * jnp.exp(cs - lw)
    rt = r * jnp.exp(cs)
    bt, kt = ka * e_neg, k * e_neg
    bp, kp = ka * e_rem, k * e_rem
    gam = jnp.exp(cend)
    zeros = jnp.zeros((C, N), F32)
    mm = functools.partial(_mmp, passes=RWKV_PASSES)
    sl = [slice(N * h, N * (h + 1)) for h in heads]
    vh = [v[:, s] for s in sl]
    alh = [al[:, s] for s in sl]
    rth = [rt[:, s] for s in sl]
    pm = [mm(jnp.concatenate([alh[h], rth[h]], 0), jnp.concatenate([bt[:, sl[h]], kt[:, sl[h]]], 0), 1, 1)
          for h in heads]
    a_ab = [jnp.where(strict, pm[h][:C, :C], 0.0) for h in heads]
    a_ak = [jnp.where(strict, pm[h][:C, C:], 0.0) for h in heads]
    ri2, ci2 = _iota2((C, 2 * C))
    incl2 = (ci2 & (C - 1)) <= ri2
    a_r = [jnp.where(incl2, pm[h][C:, :], 0.0) for h in heads]
    w = [jnp.concatenate([alh[h], mm(a_ak[h], vh[h])], 1) for h in heads]
    npow = a_ab
    levels = int(math.log2(C))
    for lvl in range(levels):
        if lvl < levels - 1:
            y = [mm(npow[h], jnp.concatenate([w[h], npow[h]], 1)) for h in heads]
            w = [w[h] + y[h][:, :2 * N] for h in heads]
            npow = [y[h][:, 2 * N:] for h in heads]
        else:
            w = [w[h] + mm(npow[h], w[h]) for h in heads]
    zv = [jnp.concatenate([zeros, vh[h]], 1) for h in heads]
    qt = [mm(a_r[h], jnp.concatenate([w[h], zv[h]], 0)) for h in heads]
    qb = [mm(bp[:, sl[h]], w[h], 0, 0) + mm(kp[:, sl[h]], zv[h], 0, 0) for h in heads]
    m0 = [m_s[h] for h in heads]
    o_h = [mm(rth[h] + qt[h][:, :N], m0[h]) + qt[h][:, N:] for h in heads]
    m_new = [_mmp(qb[h][:, :N] + jnp.where(eye, gam[:, sl[h]], 0.0), m0[h], passes=RWKV_STATE_PASSES)
             + qb[h][:, N:] for h in heads]
    outs = []
    for h in heads:
        m_s[h] = m_new[h]
        om = jnp.mean(o_h[h], -1, keepdims=True)
        d = o_h[h] - om
        ov = jnp.mean(d * d, -1, keepdims=True)
        outs.append(d * lax.rsqrt(ov + RWKV_GN_EPS))
    on = jnp.concatenate(outs, 1) * lng_ref[...] + lnb_ref[...]
    o_ref[...] = ((on + bo_ref[...]) * g_ref[...]).astype(BF16)


def _rwkv(r, k, v, lw, kk, ka, g, bonus, ln_g, ln_b, B, L):
    NC = L // CHUNK
    blk = pl.BlockSpec((CHUNK, HALF), lambda b, c: (b * NC + c, 0))
    vec = pl.BlockSpec((1, HALF), lambda b, c: (0, 0))
    return pl.pallas_call(
        _rwkv_kernel,
        grid=(B, NC),
        in_specs=[blk] * 8 + [vec, vec],
        out_specs=blk,
        out_shape=jax.ShapeDtypeStruct((B * L, HALF), BF16),
        scratch_shapes=[pltpu.VMEM((RWKV_HEADS, RWKV_N, RWKV_N), F32)],
        compiler_params=_cparams("parallel", "arbitrary"),
        name="rwkv7",
    )(r, k, v, lw, kk, ka, g, bonus, ln_g.reshape(1, HALF).astype(F32), ln_b.reshape(1, HALF).astype(F32))


def _mlstm_kernel(q_ref, k_ref, v_ref, z_ref, gc_ref, gr_ref, lng_ref, skip_ref, o_ref, c_s, m_s):
    C, H, DH = CHUNK, MLSTM_HEADS, MLSTM_DH

    @pl.when(pl.program_id(1) == 0)
    def _():
        c_s[...] = jnp.zeros_like(c_s)
        m_s[...] = jnp.zeros_like(m_s)

    ri, ci = _iota2((C, C))
    incl = ci <= ri
    gc, gr = gc_ref[...], gr_ref[...]
    b_cols = _mm_exact_lhs(incl.astype(BF16), gc)
    b_rows = _mm_exact_rhs(gr, (ri <= ci).astype(BF16))
    lane = lax.broadcasted_iota(jnp.int32, (1, DH), 1)
    ones_col = jnp.where(lane == 0, 1.0, 0.0).astype(BF16)
    q_all, z_all = q_ref[...], z_ref[...]
    heads = range(H)
    sl = [slice(DH * h, DH * (h + 1)) for h in heads]
    ones_blk = jnp.broadcast_to(ones_col, (C, DH))
    qb = [q_all[:, s].astype(BF16) for s in sl]
    kh = [k_ref[:, s] for s in sl]
    vaug = [jnp.concatenate([v_ref[:, s], ones_blk], 1) for s in sl]
    c_old = [c_s[h] for h in heads]
    qk = [_dg(qb[h], kh[h], 1, 1) for h in heads]
    qc = [jnp.dot(qb[h], c_old[h].astype(BF16), preferred_element_type=F32) for h in heads]
    i_col = [gc[:, h:h + 1] for h in heads]
    i_row = [gr[h:h + 1, :] for h in heads]
    b_col = [b_cols[:, H + h:H + h + 1] for h in heads]
    b_row = [b_rows[H + h:H + h + 1, :] for h in heads]
    m_prev = [m_s[h:h + 1, 0:1] for h in heads]
    dmat = [jnp.where(incl, b_col[h] - b_row[h] + i_row[h], NEG_BIG) for h in heads]
    inter = [b_col[h] + m_prev[h] for h in heads]
    m_t = [jnp.maximum(inter[h], jnp.max(dmat[h], -1, keepdims=True)) for h in heads]
    s = [(qk[h] * jnp.exp(dmat[h] - m_t[h])).astype(BF16) for h in heads]
    sv = [jnp.dot(s[h], vaug[h], preferred_element_type=F32) for h in heads]
    b_end = [b_col[h][C - 1:C, :] for h in heads]
    m_new = [jnp.maximum(b_end[h] + m_prev[h], jnp.max(b_end[h] - b_row[h] + i_row[h], -1, keepdims=True))
             for h in heads]
    kw = [(jnp.exp(b_end[h] - b_col[h] + i_col[h] - m_new[h]) * kh[h].astype(F32)).astype(BF16) for h in heads]
    kv = [_dg(kw[h], vaug[h], 0, 0) for h in heads]
    outs = []
    for h in heads:
        c_s[h] = jnp.exp(b_end[h] + m_prev[h] - m_new[h]) * c_old[h] + kv[h]
        m_s[h:h + 1, :] = jnp.broadcast_to(m_new[h], (1, 128))
        nd = jnp.exp(inter[h] - m_t[h]) * qc[h] + sv[h]
        hid = nd[:, :DH] / jnp.maximum(jnp.abs(nd[:, DH:DH + 1]), jnp.exp(-m_t[h]))
        hm = jnp.mean(hid, -1, keepdims=True)
        d = hid - hm
        hv = jnp.mean(d * d, -1, keepdims=True)
        outs.append(d * lax.rsqrt(hv + LN_EPS))
    hn = jnp.concatenate(outs, 1) * lng_ref[...]
    o_ref[...] = ((hn + skip_ref[...] * q_all) * jax.nn.silu(z_all)).astype(BF16)


def _mlstm(q, k, v, z, gates, ln_g, skip, B, L):
    NC, H2 = L // CHUNK, 2 * MLSTM_HEADS
    blk = pl.BlockSpec((CHUNK, HALF), lambda b, c: (b * NC + c, 0))
    vec = pl.BlockSpec((1, HALF), lambda b, c: (0, 0))
    g_rows = gates.reshape(B, NC, CHUNK, H2).transpose(0, 1, 3, 2)
    return pl.pallas_call(
        _mlstm_kernel,
        grid=(B, NC),
        in_specs=[blk, blk, blk, blk, pl.BlockSpec((CHUNK, H2), lambda b, c: (b * NC + c, 0)),
                  pl.BlockSpec((None, None, H2, CHUNK), lambda b, c: (b, c, 0, 0)), vec, vec],
        out_specs=blk,
        out_shape=jax.ShapeDtypeStruct((B * L, HALF), BF16),
        scratch_shapes=[pltpu.VMEM((MLSTM_HEADS, MLSTM_DH, 2 * MLSTM_DH), F32), pltpu.VMEM((8, 128), F32)],
        compiler_params=_cparams("parallel", "arbitrary"),
        name="mlstm",
    )(q, k, v, z, gates, g_rows, ln_g.reshape(1, HALF).astype(F32), skip.reshape(1, HALF).astype(F32))


def kernel(x, ev_w_in, ev_fox_fb, ev_s5_a_re, ev_s5_a_im, ev_s5_b_re, ev_s5_b_im, ev_s5_c_re, ev_s5_c_im, ev_s5_d, ev_s5_log_dt, ev_s5_w_glu, ev_s5_b_glu, ev_w_out, od_w_in, od_rwkv_mu, od_rwkv_w0, od_rwkv_w2, od_rwkv_a0, od_rwkv_a2, od_rwkv_g2, od_rwkv_k_k, od_rwkv_k_a, od_rwkv_r_k, od_rwkv_ln_g, od_rwkv_ln_b, od_mlstm_conv_w, od_mlstm_conv_b, od_mlstm_ib, od_mlstm_fb, od_mlstm_ln_g, od_mlstm_skip, od_w_out, ln1_g, ln1_b, ffn_w_up, ffn_conv_w, ffn_conv_b, ffn_w_down, ln2_g, ln2_b):
    B, L, _ = x.shape
    depth = ln1_g.shape[0]
    alpha = float((2 * depth) ** 0.25)
    h = x.reshape(B * L, D_MODEL).astype(F32)
    for layer in range(depth):
        i = layer // 2
        if layer % 2 == 0:
            qt, k, vt, fg, u = _even_pre(h, ev_w_in[i])
            f_row = _fox_gate(fg.reshape(B, L, FOX_HEADS).transpose(0, 2, 1), ev_fox_fb[i].astype(F32))
            f_col = jnp.pad(f_row.transpose(0, 2, 1).reshape(B * L, FOX_HEADS), ((0, 0), (0, 128 - FOX_HEADS)))
            fox = _fox_attention(qt, k, vt, f_col, f_row)
            mats = _s5_matrices(ev_s5_a_re[i], ev_s5_a_im[i], ev_s5_b_re[i], ev_s5_b_im[i],
                                ev_s5_c_re[i], ev_s5_c_im[i], ev_s5_d[i], ev_s5_log_dt[i])
            y = _s5(u, mats, B, L)
            h = _mix_out(fox, y, h, ev_w_out[i], ln1_g[layer], ln1_b[layer], alpha,
                         glu=(ev_s5_w_glu[i], ev_s5_b_glu[i]))
        else:
            (r, k, v, lw, kk, ka, g, bonus, mq, mk, mv, mz, gates) = _odd_pre(
                h, L, od_w_in[i], od_rwkv_mu[i], od_rwkv_w0[i], od_rwkv_w2[i], od_rwkv_a0[i], od_rwkv_a2[i],
                od_rwkv_g2[i], od_rwkv_k_k[i], od_rwkv_k_a[i], od_rwkv_r_k[i], od_mlstm_conv_w[i],
                od_mlstm_conv_b[i], od_mlstm_ib[i], od_mlstm_fb[i])
            c = _rwkv(r, k, v, lw, kk, ka, g, bonus, od_rwkv_ln_g[i], od_rwkv_ln_b[i], B, L)
            dm = _mlstm(mq, mk, mv, mz, gates, od_mlstm_ln_g[i], od_mlstm_skip[i], B, L)
            h = _mix_out(c, dm, h, od_w_out[i], ln1_g[layer], ln1_b[layer], alpha)
        h = _conv_ffn(h, L, ffn_w_up[layer], ffn_conv_w[layer], ffn_conv_b[layer], ffn_w_down[layer],
                      ln2_g[layer], ln2_b[layer], alpha)
    return h.reshape(B, L, D_MODEL).astype(x.dtype)
```

```python
import functools
import math

import jax
import jax.numpy as jnp
from jax import lax
from jax.experimental import pallas as pl
from jax.experimental.pallas import tpu as pltpu

F32 = jnp.float32
BF16 = jnp.bfloat16
HIGHEST = lax.Precision.HIGHEST

D_MODEL = 1024
HALF = D_MODEL // 2
FOX_HEADS = 8
FOX_DH = HALF // FOX_HEADS
FOX_TILE = 256
FOX_AUG = 256
S5_GROUPS = 32
S5_GROUP_CH = 16
S5_STATE = 64
S5_CHUNK = 16
S5_SLAB_GROUPS = 128 // S5_GROUP_CH
S5_SLABS = S5_GROUPS // S5_SLAB_GROUPS
S5_TILE_CHUNKS = 16
RWKV_HEADS = 8
RWKV_N = HALF // RWKV_HEADS
RWKV_DECAY_LORA = 64
RWKV_AAA_LORA = 64
RWKV_GATE_LORA = 160
RWKV_GN_EPS = 64e-5
RWKV_PASSES = 1
RWKV_STATE_PASSES = 3
MLSTM_HEADS = 4
MLSTM_DH = HALF // MLSTM_HEADS
MLSTM_CONV = 4
CHUNK = 64
D_FF = 2816
FFN_CONV = 3
FFN_TILE = 256
LN_EPS = 1e-5
HALO = 8
NEG_BIG = -1e30
V7X_VMEM_LIMIT_BYTES = 56 * 1024 * 1024


def _cparams(*sem):
    return pltpu.CompilerParams(dimension_semantics=sem, vmem_limit_bytes=V7X_VMEM_LIMIT_BYTES)


def _resident(shape):
    nd = len(shape)
    return pl.BlockSpec(shape, lambda *_: (0,) * nd, pipeline_mode=pl.Buffered(1))


def _rows(tm, width):
    return pl.BlockSpec((tm, width), lambda i: (i, 0))


def _halo_rows(tm, width):
    return pl.BlockSpec((HALO, width), lambda i: (jnp.maximum(i * (tm // HALO) - 1, 0), 0))


def _mm(a, b):
    return jnp.dot(a.astype(BF16), b.astype(BF16), preferred_element_type=F32)


def _dg(a, b, ca, cb):
    return lax.dot_general(a, b, (((ca,), (cb,)), ((), ())), preferred_element_type=F32)


def _hi_lo(a):
    hi = a.astype(BF16)
    lo = (a - hi.astype(F32)).astype(BF16)
    return hi, lo


def _mm3(a, b, ca=1, cb=0):
    ah, al = _hi_lo(a)
    bh, bl = _hi_lo(b)
    return _dg(ah, bh, ca, cb) + _dg(ah, bl, ca, cb) + _dg(al, bh, ca, cb)


def _mmp(a, b, ca=1, cb=0, passes=1):
    if passes == 3:
        return _mm3(a, b, ca, cb)
    return _dg(a.astype(BF16), b.astype(BF16), ca, cb)


def _split3(a):
    a1 = a.astype(BF16)
    r1 = a - a1.astype(F32)
    a2 = r1.astype(BF16)
    a3 = (r1 - a2.astype(F32)).astype(BF16)
    return a1, a2, a3


def _mm_exact_rhs(a, b01):
    a1, a2, a3 = _split3(a)
    return (jnp.dot(a1, b01, preferred_element_type=F32) + jnp.dot(a2, b01, preferred_element_type=F32)
            + jnp.dot(a3, b01, preferred_element_type=F32))


def _mm_exact_lhs(a01, b):
    b1, b2, b3 = _split3(b)
    return (jnp.dot(a01, b1, preferred_element_type=F32) + jnp.dot(a01, b2, preferred_element_type=F32)
            + jnp.dot(a01, b3, preferred_element_type=F32))


def _layer_norm(x, g, b):
    mu = jnp.mean(x, -1, keepdims=True)
    d = x - mu
    var = jnp.mean(d * d, -1, keepdims=True)
    return d * lax.rsqrt(var + LN_EPS) * g + b


def _iota2(shape):
    return lax.broadcasted_iota(jnp.int32, shape, 0), lax.broadcasted_iota(jnp.int32, shape, 1)


def _even_pre_kernel(x_ref, wqt_ref, wk_ref, wvt_ref, wf_ref, wu_ref, qt_ref, k_ref, vt_ref, fg_ref, u_ref):
    xb = x_ref[...].astype(BF16)
    qt_ref[...] = (_dg(wqt_ref[...], xb, 1, 1) * (FOX_DH ** -0.5)).astype(BF16)
    vt_ref[...] = _dg(wvt_ref[...], xb, 1, 1).astype(BF16)
    k_ref[...] = jnp.dot(xb, wk_ref[...], preferred_element_type=F32).astype(BF16)
    u = jnp.dot(xb, wu_ref[...], preferred_element_type=F32).astype(BF16)
    for s in range(S5_SLABS):
        u_ref[s] = u[:, 128 * s:128 * (s + 1)]
    fg_ref[...] = jnp.dot(xb, wf_ref[...], preferred_element_type=F32)[:, :FOX_HEADS]


def _even_pre(x, w_in, tm=512):
    T = x.shape[0]
    wb = w_in.astype(BF16)
    wqt, wk, wvt = wb[:, :HALF].T, wb[:, HALF:2 * HALF], wb[:, 2 * HALF:3 * HALF].T
    wf = jnp.pad(wb[:, 3 * HALF:3 * HALF + FOX_HEADS], ((0, 0), (0, 128 - FOX_HEADS)))
    wu = wb[:, 3 * HALF + FOX_HEADS:]
    half_out = jax.ShapeDtypeStruct((T, HALF), BF16)
    half_t = jax.ShapeDtypeStruct((HALF, T), BF16)
    cols = pl.BlockSpec((HALF, tm), lambda i: (0, i))
    return pl.pallas_call(
        _even_pre_kernel,
        grid=(T // tm,),
        in_specs=[_rows(tm, D_MODEL), _resident(wqt.shape), _resident(wk.shape), _resident(wvt.shape),
                  _resident(wf.shape), _resident(wu.shape)],
        out_specs=[cols, _rows(tm, HALF), cols, _rows(tm, FOX_HEADS),
                   pl.BlockSpec((S5_SLABS, tm, 128), lambda i: (0, i, 0))],
        out_shape=[half_t, half_out, half_t, jax.ShapeDtypeStruct((T, FOX_HEADS), F32),
                   jax.ShapeDtypeStruct((S5_SLABS, T, 128), BF16)],
        compiler_params=_cparams("parallel"),
        name="even_pre",
    )(x, wqt, wk, wvt, wf, wu)


def _fox_gate_kernel(fg_ref, fb_ref, o_ref, *, L):
    ls = jax.nn.log_sigmoid(fg_ref[...] + fb_ref[...])
    r, c = _iota2((128, 128))
    tri = (r <= c).astype(BF16)
    carry = jnp.zeros((FOX_HEADS, 1), F32)
    for j in range(L // 128):
        cum = _mm_exact_rhs(ls[:, j * 128:(j + 1) * 128], tri) + carry
        o_ref[:, j * 128:(j + 1) * 128] = cum
        carry = cum[:, 127:128]


def _fox_gate(fg_t, fb):
    B, H, L = fg_t.shape
    return pl.pallas_call(
        functools.partial(_fox_gate_kernel, L=L),
        grid=(B,),
        in_specs=[pl.BlockSpec((None, H, L), lambda b: (b, 0, 0)), _resident((H, 1))],
        out_specs=pl.BlockSpec((None, H, L), lambda b: (b, 0, 0)),
        out_shape=jax.ShapeDtypeStruct((B, H, L), F32),
        compiler_params=_cparams("parallel"),
        name="fox_gate",
    )(fg_t, fb.reshape(H, 1))


def _fox_kernel(qt_ref, k_ref, vt_ref, fc_ref, fr_ref, o_ref, qa_s, ka_s, m_s, l_s, acc_s, *, L):
    TQ = TK = FOX_TILE
    H, DH, KA = FOX_HEADS, FOX_DH, FOX_AUG
    heads = range(H)

    t1, t2, t3 = (t.astype(F32) for t in _split3(fr_ref[...]))
    r16 = lax.broadcasted_iota(jnp.int32, (16, 1), 0)
    upper = lax.broadcasted_iota(jnp.int32, (2 * DH, 1), 0) < DH
    for h in heads:
        p, hh = divmod(h, 2)
        qpair = qt_ref[2 * DH * p:2 * DH * (p + 1), :]
        keep = upper if hh == 0 else jnp.logical_not(upper)
        qa_s[h, 0:2 * DH, :] = jnp.where(keep, qpair, jnp.zeros_like(qpair))
        ones_rows = (r16 >= 3 + 3 * hh) & (r16 < 6 + 3 * hh)
        blk = jnp.where(r16 == 0, t1[h:h + 1], jnp.where(r16 == 1, t2[h:h + 1], jnp.where(
            r16 == 2, t3[h:h + 1], jnp.where(ones_rows, 1.0, 0.0))))
        qa_s[h, 2 * DH:2 * DH + 16, :] = blk.astype(BF16)
        qa_s[h, 2 * DH + 16:KA, :] = jnp.zeros((KA - 2 * DH - 16, L), BF16)
    c1, c2, c3 = _split3(fc_ref[...])
    rs, cs_ = _iota2((128, 128))
    lane = lax.broadcasted_iota(jnp.int32, (1, 128), 1)
    ones3 = jnp.where(lane < 3, 1.0, 0.0)
    for p in range(H // 2):
        def sel(i, p=p):
            hit = ((rs == 2 * p) & (cs_ == 3 + i)) | ((rs == 2 * p + 1) & (cs_ == 6 + i))
            return jnp.where(hit, -1.0, 0.0).astype(BF16)
        aug = (jnp.dot(c1, sel(0), preferred_element_type=F32) + jnp.dot(c2, sel(1), preferred_element_type=F32)
               + jnp.dot(c3, sel(2), preferred_element_type=F32) + ones3)
        ka_s[:, KA * p:KA * p + 2 * DH] = k_ref[:, 2 * DH * p:2 * DH * (p + 1)]
        ka_s[:, KA * p + 2 * DH:KA * (p + 1)] = aug.astype(BF16)

    ri, ci = _iota2((TK, TQ))
    visible = ri <= ci

    def q_block(qi, _):
        q0 = pl.multiple_of(qi * TQ, TQ)
        m_s[...] = jnp.full(m_s.shape, NEG_BIG, F32)
        l_s[...] = jnp.zeros(l_s.shape, F32)
        acc_s[...] = jnp.zeros(acc_s.shape, F32)

        def tile(k0, masked):
            kt = [ka_s[pl.ds(k0, TK), KA * p:KA * (p + 1)] for p in range(H // 2)]
            st = [jnp.dot(kt[h // 2], qa_s[h, :, pl.ds(q0, TQ)], preferred_element_type=F32) for h in heads]
            if masked:
                st = [jnp.where(visible, s, NEG_BIG) for s in st]
            m_old = [m_s[h:h + 1, :] for h in heads]
            m_new = [jnp.maximum(m_old[h], jnp.max(st[h], 0, keepdims=True)) for h in heads]
            pt = [jnp.exp(st[h] - m_new[h]) for h in heads]
            pv = [jnp.dot(vt_ref[DH * h:DH * (h + 1), pl.ds(k0, TK)], pt[h].astype(BF16),
                          preferred_element_type=F32) for h in heads]
            for h in heads:
                a = jnp.exp(m_old[h] - m_new[h])
                m_s[h:h + 1, :] = m_new[h]
                l_s[h:h + 1, :] = a * l_s[h:h + 1, :] + jnp.sum(pt[h], 0, keepdims=True)
                acc_s[DH * h:DH * (h + 1), :] = a * acc_s[DH * h:DH * (h + 1), :] + pv[h]

        def k_step(j, carry):
            tile(pl.multiple_of(j * TK, TK), False)
            return carry

        lax.fori_loop(0, qi, k_step, 0)
        tile(q0, True)
        for p in range(H // 2):
            o_pair = jnp.concatenate([acc_s[DH * h:DH * (h + 1), :] / l_s[h:h + 1, :] for h in (2 * p, 2 * p + 1)], 0)
            o_ref[pl.ds(q0, TQ), 2 * DH * p:2 * DH * (p + 1)] = o_pair.T.astype(BF16)
        return 0

    lax.fori_loop(0, L // TQ, q_block, 0)


def _fox_attention(qt, k, vt, f_col, f_row):
    B, H, L = f_row.shape
    seq = pl.BlockSpec((L, HALF), lambda b: (b, 0))
    seq_t = pl.BlockSpec((HALF, L), lambda b: (0, b))
    return pl.pallas_call(
        functools.partial(_fox_kernel, L=L),
        grid=(B,),
        in_specs=[seq_t, seq, seq_t, pl.BlockSpec((L, 128), lambda b: (b, 0)),
                  pl.BlockSpec((None, H, L), lambda b: (b, 0, 0))],
        out_specs=seq,
        out_shape=jax.ShapeDtypeStruct((B * L, HALF), BF16),
        scratch_shapes=[pltpu.VMEM((H, FOX_AUG, L), BF16), pltpu.VMEM((L, FOX_AUG * H // 2), BF16),
                        pltpu.VMEM((H, FOX_TILE), F32), pltpu.VMEM((H, FOX_TILE), F32),
                        pltpu.VMEM((HALF, FOX_TILE), F32)],
        compiler_params=_cparams("parallel"),
        name="fox_attention",
    )(qt, k, vt, f_col, f_row)


def _s5_matrices(a_re, a_im, b_re, b_im, c_re, c_im, d, log_dt):
    G, P, Cg, LC = S5_GROUPS, S5_STATE, S5_GROUP_CH, S5_CHUNK
    a_re, a_im, b_re, b_im, c_re, c_im = (t.astype(F32) for t in (a_re, a_im, b_re, b_im, c_re, c_im))
    dt = jnp.exp(log_dt.astype(F32))[:, None]
    mag = jnp.exp(a_re * dt)
    lam_re, lam_im = mag * jnp.cos(a_im * dt), mag * jnp.sin(a_im * dt)
    den = a_re ** 2 + a_im ** 2
    zr = ((lam_re - 1.0) * a_re + lam_im * a_im) / den
    zi = (lam_im * a_re - (lam_re - 1.0) * a_im) / den
    bb_re = zr[..., None] * b_re - zi[..., None] * b_im
    bb_im = zr[..., None] * b_im + zi[..., None] * b_re
    n = jnp.arange(LC + 1, dtype=F32)[:, None, None]
    pw_mag = jnp.exp(n * (a_re * dt)[None])
    pr, pi = pw_mag * jnp.cos(n * (a_im * dt)[None]), pw_mag * jnp.sin(n * (a_im * dt)[None])
    ein = functools.partial(jnp.einsum, precision=HIGHEST)
    cr = c_re[None] * pr[:, :, None, :] - c_im[None] * pi[:, :, None, :]
    ci = -(c_re[None] * pi[:, :, None, :] + c_im[None] * pr[:, :, None, :])
    kn = ein('ngcp,gpd->ngcd', cr[:LC], bb_re) + ein('ngcp,gpd->ngcd', ci[:LC], bb_im)
    s_idx = jnp.arange(LC)
    lag = s_idx[None, :] - s_idx[:, None]
    kt = jnp.where((lag >= 0)[:, :, None, None, None],
                   kn[jnp.clip(lag, 0, LC - 1)], 0.0)
    dmat = (lag == 0)[:, :, None, None, None] * (d.astype(F32).reshape(1, 1, G, Cg, 1)
                                                 * jnp.eye(Cg, dtype=F32)[None, None, None])
    NS, GS = S5_SLABS, S5_SLAB_GROUPS
    eye = jnp.eye(GS, dtype=F32)
    k6 = (kt + dmat).reshape(LC, LC, NS, GS, Cg, Cg)
    kmat = jnp.einsum('stbgoi,gh->bsgitho', k6, eye).reshape(NS, LC * 128, LC * 128)
    rr, ri = pr[LC - 1 - s_idx], pi[LC - 1 - s_idx]
    e_re = rr[..., None] * bb_re[None] - ri[..., None] * bb_im[None]
    e_im = rr[..., None] * bb_im[None] + ri[..., None] * bb_re[None]
    e6 = jnp.stack([e_re, e_im]).reshape(2, LC, NS, GS, P, Cg)
    emat = jnp.einsum('zsbgpc,gh->bsgczhp', e6, eye).reshape(NS, LC * 128, 2 * GS * P)
    f6 = jnp.stack([cr[1:], ci[1:]]).reshape(2, LC, NS, GS, Cg, P)
    fmat = jnp.einsum('ztbgcp,gh->bzgpthc', f6, eye).reshape(NS, 2 * GS * P, LC * 128)
    lr, li = pr[LC].reshape(NS, 1, GS * P), pi[LC].reshape(NS, 1, GS * P)
    lam_a = jnp.concatenate([lr, lr], axis=-1)
    lam_b = jnp.concatenate([-li, li], axis=-1)
    return kmat.astype(BF16), emat.astype(BF16), fmat.astype(BF16), lam_a, lam_b


def _s5_kernel(u_ref, k_ref, e_ref, f_ref, la_ref, lb_ref, y_ref, e_s, hs_s, h_s, *, B):
    @pl.when(pl.program_id(1) == 0)
    def _():
        h_s[...] = jnp.zeros_like(h_s)

    u = u_ref[...]
    e_s[...] = jnp.dot(u, e_ref[...], preferred_element_type=F32)
    la, lb = la_ref[...], lb_ref[...]
    half = h_s.shape[1] // 2
    h = h_s[...]
    for kc in range(u.shape[0] // B):
        rs = slice(kc * B, (kc + 1) * B)
        hs_s[rs, :] = h
        swapped = jnp.concatenate([h[:, half:], h[:, :half]], axis=1)
        h = la * h + lb * swapped + e_s[rs, :]
    h_s[...] = h
    y_ref[...] = (jnp.dot(u, k_ref[...], preferred_element_type=F32)
                  + jnp.dot(hs_s[...].astype(BF16), f_ref[...], preferred_element_type=F32)).astype(BF16)


def _s5(u4, mats, B, L):
    kmat, emat, fmat, lam_a, lam_b = mats
    NS, LC = S5_SLABS, S5_CHUNK
    NK = L // LC
    R, W, S2 = NK * B, LC * 128, emat.shape[2]
    ug = u4.reshape(NS, B, NK, W).transpose(0, 2, 1, 3).reshape(NS, R, W)
    TR = B * min(NK, S5_TILE_CHUNKS)
    rows = pl.BlockSpec((None, TR, W), lambda s, r: (s, r, 0))
    per_s = lambda a, b: pl.BlockSpec((None, a, b), lambda s, r: (s, 0, 0), pipeline_mode=pl.Buffered(1))
    y = pl.pallas_call(
        functools.partial(_s5_kernel, B=B),
        grid=(NS, R // TR),
        in_specs=[rows, per_s(W, W), per_s(W, S2), per_s(S2, W), per_s(1, S2), per_s(1, S2)],
        out_specs=rows,
        out_shape=jax.ShapeDtypeStruct((NS, R, W), BF16),
        scratch_shapes=[pltpu.VMEM((TR, S2), F32), pltpu.VMEM((TR, S2), F32), pltpu.VMEM((B, S2), F32)],
        compiler_params=_cparams("parallel", "arbitrary"),
        name="s5",
    )(ug, kmat, emat, fmat, lam_a, lam_b)
    return y.reshape(NS, NK, B, W).transpose(0, 2, 1, 3).reshape(NS, B * L, 128)


def _mix_kernel(a_ref, b_ref, x_ref, wt_ref, wb_ref, g_ref, beta_ref, *rest, even, alpha):
    if even:
        wg_ref, bg_ref, o_ref = rest
        z = jax.nn.gelu(jnp.concatenate([b_ref[s] for s in range(S5_SLABS)], axis=1).astype(F32))
        second = z * jax.nn.sigmoid(_mm(z, wg_ref[...]) + bg_ref[...])
    else:
        (o_ref,) = rest
        second = b_ref[...]
    mix = _mm(a_ref[...], wt_ref[...]) + _mm(second, wb_ref[...])
    o_ref[...] = _layer_norm(alpha * x_ref[...] + mix, g_ref[...], beta_ref[...])


def _mix_out(a, b, x, w_out, ln_g, ln_b, alpha, glu=None, tm=512):
    T = x.shape[0]
    wb16 = w_out.astype(BF16)
    args = [a, b, x, wb16[:HALF], wb16[HALF:], ln_g.reshape(1, D_MODEL), ln_b.reshape(1, D_MODEL)]
    specs = [_rows(tm, HALF), _rows(tm, HALF), _rows(tm, D_MODEL), _resident((HALF, D_MODEL)),
             _resident((HALF, D_MODEL)), _resident((1, D_MODEL)), _resident((1, D_MODEL))]
    if glu is not None:
        w_glu, b_glu = glu
        specs[1] = pl.BlockSpec((S5_SLABS, tm, 128), lambda i: (0, i, 0))
        args += [w_glu.astype(BF16), b_glu.reshape(1, HALF).astype(F32)]
        specs += [_resident((HALF, HALF)), _resident((1, HALF))]
    return pl.pallas_call(
        functools.partial(_mix_kernel, even=glu is not None, alpha=alpha),
        grid=(T // tm,),
        in_specs=specs,
        out_specs=_rows(tm, D_MODEL),
        out_shape=jax.ShapeDtypeStruct((T, D_MODEL), F32),
        compiler_params=_cparams("parallel"),
        name="mix_out",
    )(*args)


def _ffn_kernel(x_ref, xh_ref, wu_ref, cw_ref, cb_ref, wd_ref, g_ref, beta_ref, o_ref, *, tiles_per_seq, alpha):
    tm = x_ref.shape[0]
    first = (pl.program_id(0) % tiles_per_seq) == 0
    x = x_ref[...]
    xh = jnp.where(first, 0.0, xh_ref[...])
    xe = jnp.concatenate([xh, x], axis=0).astype(BF16)
    xb = xe[HALO:]
    acc = jnp.zeros((tm, D_MODEL), F32)
    for c in range(D_FF // FFN_TILE):
        cs = slice(c * FFN_TILE, (c + 1) * FFN_TILE)
        gs = slice(D_FF + c * FFN_TILE, D_FF + (c + 1) * FFN_TILE)
        ue = jnp.dot(xe, wu_ref[:, cs], preferred_element_type=F32)
        gate = jnp.dot(xb, wu_ref[:, gs], preferred_element_type=F32)
        cw = cw_ref[:, cs]
        u = (cw[2:3] * ue[HALO:] + cw[1:2] * ue[HALO - 1:HALO - 1 + tm] + cw[0:1] * ue[HALO - 2:HALO - 2 + tm]
             + cb_ref[:, cs])
        acc = acc + _mm(jax.nn.gelu(u) * gate, wd_ref[cs, :])
    o_ref[...] = _layer_norm(alpha * x + acc, g_ref[...], beta_ref[...])


def _conv_ffn(x, L, w_up, conv_w, conv_b, w_down, ln_g, ln_b, alpha, tm=512):
    T = x.shape[0]
    tm = min(tm, L)
    return pl.pallas_call(
        functools.partial(_ffn_kernel, tiles_per_seq=L // tm, alpha=alpha),
        grid=(T // tm,),
        in_specs=[_rows(tm, D_MODEL), _halo_rows(tm, D_MODEL), _resident((D_MODEL, 2 * D_FF)),
                  _resident((FFN_CONV, D_FF)), _resident((1, D_FF)), _resident((D_FF, D_MODEL)),
                  _resident((1, D_MODEL)), _resident((1, D_MODEL))],
        out_specs=_rows(tm, D_MODEL),
        out_shape=jax.ShapeDtypeStruct((T, D_MODEL), F32),
        compiler_params=_cparams("parallel"),
        name="conv_ffn",
    )(x, x, w_up.astype(BF16), conv_w.astype(F32), conv_b.reshape(1, D_FF).astype(F32),
      w_down.astype(BF16), ln_g.reshape(1, D_MODEL), ln_b.reshape(1, D_MODEL))


_LORA_PAD = (128, 128, 256)
_RWKV_PAD = 3 * HALF + sum(_LORA_PAD)


def _odd_pre_kernel(x_ref, xh_ref, wrk_ref, mu_ref, w2_ref, a2_ref, g2_ref, vec_ref, bd_ref,
                    wqk_ref, cw_ref, cb_ref, wv_ref, wz_ref, wif_ref, gb_ref,
                    r_ref, k_ref, v_ref, lw_ref, kk_ref, ka_ref, g_ref, bo_ref,
                    mq_ref, mk_ref, mv_ref, mz_ref, gate_ref, *, tiles_per_seq):
    tm = x_ref.shape[0]
    first = (pl.program_id(0) % tiles_per_seq) == 0
    xh = jnp.where(first, 0.0, xh_ref[...])
    xe = jnp.concatenate([xh, x_ref[...]], axis=0).astype(BF16)
    xb = xe[HALO:]

    pe = jnp.dot(xe, wrk_ref[...], preferred_element_type=F32)
    cur, prev = pe[HALO:], pe[HALO - 1:HALO - 1 + tm]
    p = cur + (prev - cur) * mu_ref[...]
    r, k, v = p[:, :HALF], p[:, HALF:2 * HALF], p[:, 2 * HALF:3 * HALF]
    o0 = 3 * HALF
    wd = p[:, o0:o0 + _LORA_PAD[0]]
    ad = p[:, o0 + _LORA_PAD[0]:o0 + _LORA_PAD[0] + _LORA_PAD[1]]
    gd = p[:, o0 + _LORA_PAD[0] + _LORA_PAD[1]:]
    w0, a0, k_k, k_a, r_k = (vec_ref[i:i + 1, :] for i in range(5))
    wlog = -jax.nn.softplus(-(w0 + _mm(jnp.tanh(wd), w2_ref[...]))) - 0.5
    lw_ref[...] = -jnp.exp(wlog)
    a = jax.nn.sigmoid(a0 + _mm(ad, a2_ref[...]))
    g_ref[...] = _mm(jax.nn.sigmoid(gd), g2_ref[...]).astype(g_ref.dtype)
    kk = k * k_k
    ss = _mm_exact_rhs(kk * kk, bd_ref[...])
    kk = kk * lax.rsqrt(jnp.maximum(ss, 1e-24))
    kmod = k * (1.0 + (a - 1.0) * k_a)
    bo_ref[...] = (_mm_exact_rhs(r * kmod * r_k, bd_ref[...]) * v).astype(bo_ref.dtype)
    r_ref[...] = r.astype(r_ref.dtype)
    k_ref[...] = kmod.astype(k_ref.dtype)
    v_ref[...] = v.astype(v_ref.dtype)
    kk_ref[...] = kk.astype(kk_ref.dtype)
    ka_ref[...] = (kk * a).astype(ka_ref.dtype)

    qke = jnp.dot(xe, wqk_ref[...], preferred_element_type=F32)
    cw = cw_ref[...]
    qk = cb_ref[...]
    for j in range(MLSTM_CONV):
        off = HALO - (MLSTM_CONV - 1) + j
        qk = qk + cw[j:j + 1] * qke[off:off + tm]
    qk = jax.nn.silu(qk)
    mq_ref[...] = qk[:, :HALF].astype(mq_ref.dtype)
    mk_ref[...] = (qk[:, HALF:] * (MLSTM_DH ** -0.5)).astype(BF16)
    mv_ref[...] = jnp.dot(xb, wv_ref[...], preferred_element_type=F32).astype(BF16)
    mz_ref[...] = jnp.dot(xb, wz_ref[...], preferred_element_type=F32).astype(mz_ref.dtype)
    pre = jnp.dot(xb, wif_ref[...], preferred_element_type=F32)[:, :2 * MLSTM_HEADS] + gb_ref[...]
    is_i = lax.broadcasted_iota(jnp.int32, pre.shape, 1) < MLSTM_HEADS
    gate_ref[...] = jnp.where(is_i, pre, jax.nn.log_sigmoid(pre))


def _head_block_ones(width, head):
    idx = jnp.arange(width) // head
    return (idx[:, None] == idx[None, :]).astype(BF16)


def _odd_pre(x, L, w_in, mu, w0, w2, a0, a2, g2, k_k, k_a, r_k, conv_w, conv_b, ib, fb, tm=256):
    T = x.shape[0]
    tm = min(tm, L)
    wb = w_in.astype(BF16)
    o = 3 * HALF
    sizes = (RWKV_DECAY_LORA, RWKV_AAA_LORA, RWKV_GATE_LORA)

    def pad_lora(m, axis):
        parts, s = [], o
        for sz, pd in zip(sizes, _LORA_PAD):
            piece = lax.slice_in_dim(m, s, s + sz, axis=axis)
            widths = [(0, 0)] * m.ndim
            widths[axis] = (0, pd - sz)
            parts.append(jnp.pad(piece, widths))
            s += sz
        return jnp.concatenate([lax.slice_in_dim(m, 0, o, axis=axis)] + parts, axis=axis)

    rwkv_proj = o + sum(sizes)
    wrk = pad_lora(wb[:, :rwkv_proj], 1)
    mu_p = pad_lora(mu.astype(F32).reshape(1, -1), 1)
    padr = lambda m, rows: jnp.pad(m.astype(BF16), ((0, rows - m.shape[0]), (0, 0)))
    w2p, a2p, g2p = padr(w2, _LORA_PAD[0]), padr(a2, _LORA_PAD[1]), padr(g2, _LORA_PAD[2])
    vecs = jnp.stack([w0, a0, k_k, k_a, r_k.reshape(HALF)]).astype(F32)
    vecs = jnp.pad(vecs, ((0, 8 - vecs.shape[0]), (0, 0)))
    bd = _head_block_ones(HALF, RWKV_N)
    wm = wb[:, rwkv_proj:]
    wqk, wmv, wmz = wm[:, :2 * HALF], wm[:, 2 * HALF:3 * HALF], wm[:, 3 * HALF:4 * HALF]
    wif = jnp.pad(wm[:, 4 * HALF:], ((0, 0), (0, 128 - 2 * MLSTM_HEADS)))
    gbias = jnp.concatenate([ib, fb]).astype(F32).reshape(1, 2 * MLSTM_HEADS)
    args = [x, x, wrk, mu_p, w2p, a2p, g2p, vecs, bd, wqk, conv_w.astype(F32),
            conv_b.astype(F32).reshape(1, 2 * HALF), wmv, wmz, wif, gbias]
    specs = [_rows(tm, D_MODEL), _halo_rows(tm, D_MODEL)] + [_resident(a.shape) for a in args[2:]]
    f_half = jax.ShapeDtypeStruct((T, HALF), F32)
    b_half = jax.ShapeDtypeStruct((T, HALF), BF16)
    out_shape = [b_half] * 3 + [f_half] + [b_half] * 8 + [jax.ShapeDtypeStruct((T, 2 * MLSTM_HEADS), F32)]
    out_specs = [_rows(tm, HALF)] * 12 + [_rows(tm, 2 * MLSTM_HEADS)]
    return pl.pallas_call(
        functools.partial(_odd_pre_kernel, tiles_per_seq=L // tm),
        grid=(T // tm,),
        in_specs=specs,
        out_specs=out_specs,
        out_shape=out_shape,
        compiler_params=_cparams("parallel"),
        name="odd_pre",
    )(*args)


def _rwkv_kernel(r_ref, k_ref, v_ref, lw_ref, kk_ref, ka_ref, g_ref, bo_ref, lng_ref, lnb_ref, o_ref, m_s):
    C, N, H = CHUNK, RWKV_N, RWKV_HEADS
    heads = range(H)

    @pl.when(pl.program_id(1) == 0)
    def _():
        m_s[...] = jnp.zeros_like(m_s)

    ri, ci = _iota2((C, C))
    incl, strict, eye = ci <= ri, ci < ri, ci == ri
    lw = lw_ref[...]
    cs = _mm_exact_lhs(incl.astype(BF16), lw)
    cend = cs[C - 1:C, :]
    e_neg = jnp.exp(-cs)
    e_rem = jnp.exp(cend - cs)
    r, k, v, kk, ka = (t[...].astype(F32) for t in (r_ref, k_ref, v_ref, kk_ref, ka_ref))
    al = -kk * jnp.exp(cs - lw)
    rt = r * jnp.exp(cs)
    bt, kt = ka * e_neg, k * e_neg
    bp, kp = ka * e_rem, k * e_rem
    gam = jnp.exp(cend)
    zeros = jnp.zeros((C, N), F32)
    mm = functools.partial(_mmp, passes=RWKV_PASSES)
    sl = [slice(N * h, N * (h + 1)) for h in heads]
    vh = [v[:, s] for s in sl]
    alh = [al[:, s] for s in sl]
    rth = [rt[:, s] for s in sl]
    pm = [mm(jnp.concatenate([alh[h], rth[h]], 0), jnp.concatenate([bt[:, sl[h]], kt[:, sl[h]]], 0), 1, 1)
          for h in heads]
    a_ab = [jnp.where(strict, pm[h][:C, :C], 0.0) for h in heads]
    a_ak = [jnp.where(strict, pm[h][:C, C:], 0.0) for h in heads]
    ri2, ci2 = _iota2((C, 2 * C))
    incl2 = (ci2 & (C - 1)) <= ri2
    a_r = [jnp.where(incl2, pm[h][C:, :], 0.0) for h in heads]
    w = [jnp.concatenate([alh[h], mm(a_ak[h], vh[h])], 1) for h in heads]
    npow = a_ab
    levels = int(math.log2(C))
    for lvl in range(levels):
        if lvl < levels - 1:
            y = [mm(npow[h], jnp.concatenate([w[h], npow[h]], 1)) for h in heads]
            w = [w[h] + y[h][:, :2 * N] for h in heads]
            npow = [y[h][:, 2 * N:] for h in heads]
        else:
            w = [w[h] + mm(npow[h], w[h]) for h in heads]
    zv = [jnp.concatenate([zeros, vh[h]], 1) for h in heads]
    qt = [mm(a_r[h], jnp.concatenate([w[h], zv[h]], 0)) for h in heads]
    qb = [mm(bp[:, sl[h]], w[h], 0, 0) + mm(kp[:, sl[h]], zv[h], 0, 0) for h in heads]
    m0 = [m_s[h] for h in heads]
    o_h = [mm(rth[h] + qt[h][:, :N], m0[h]) + qt[h][:, N:] for h in heads]
    m_new = [_mmp(qb[h][:, :N] + jnp.where(eye, gam[:, sl[h]], 0.0), m0[h], passes=RWKV_STATE_PASSES)
             + qb[h][:, N:] for h in heads]
    outs = []
    for h in heads:
        m_s[h] = m_new[h]
        om = jnp.mean(o_h[h], -1, keepdims=True)
        d = o_h[h] - om
        ov = jnp.mean(d * d, -1, keepdims=True)
        outs.append(d * lax.rsqrt(ov + RWKV_GN_EPS))
    on = jnp.concatenate(outs, 1) * lng_ref[...] + lnb_ref[...]
    o_ref[...] = ((on + bo_ref[...].astype(F32)) * g_ref[...].astype(F32)).astype(BF16)


def _rwkv(r, k, v, lw, kk, ka, g, bonus, ln_g, ln_b, B, L):
    NC = L // CHUNK
    blk = pl.BlockSpec((CHUNK, HALF), lambda b, c: (b * NC + c, 0))
    vec = pl.BlockSpec((1, HALF), lambda b, c: (0, 0))
    return pl.pallas_call(
        _rwkv_kernel,
        grid=(B, NC),
        in_specs=[blk] * 8 + [vec, vec],
        out_specs=blk,
        out_shape=jax.ShapeDtypeStruct((B * L, HALF), BF16),
        scratch_shapes=[pltpu.VMEM((RWKV_HEADS, RWKV_N, RWKV_N), F32)],
        compiler_params=_cparams("parallel", "arbitrary"),
        name="rwkv7",
    )(r, k, v, lw, kk, ka, g, bonus, ln_g.reshape(1, HALF).astype(F32), ln_b.reshape(1, HALF).astype(F32))


def _mlstm_kernel(q_ref, k_ref, v_ref, z_ref, gc_ref, gr_ref, lng_ref, skip_ref, o_ref, c_s, m_s):
    C, H, DH = CHUNK, MLSTM_HEADS, MLSTM_DH

    @pl.when(pl.program_id(1) == 0)
    def _():
        c_s[...] = jnp.zeros_like(c_s)
        m_s[...] = jnp.zeros_like(m_s)

    ri, ci = _iota2((C, C))
    incl = ci <= ri
    gc, gr = gc_ref[...], gr_ref[...]
    b_cols = _mm_exact_lhs(incl.astype(BF16), gc)
    b_rows = _mm_exact_rhs(gr, (ri <= ci).astype(BF16))
    lane = lax.broadcasted_iota(jnp.int32, (1, DH), 1)
    ones_col = jnp.where(lane == 0, 1.0, 0.0).astype(BF16)
    q_all, z_all = q_ref[...].astype(F32), z_ref[...].astype(F32)
    heads = range(H)
    sl = [slice(DH * h, DH * (h + 1)) for h in heads]
    ones_blk = jnp.broadcast_to(ones_col, (C, DH))
    qb = [q_all[:, s].astype(BF16) for s in sl]
    kh = [k_ref[:, s] for s in sl]
    vaug = [jnp.concatenate([v_ref[:, s], ones_blk], 1) for s in sl]
    c_old = [c_s[h] for h in heads]
    qk = [_dg(qb[h], kh[h], 1, 1) for h in heads]
    qc = [jnp.dot(qb[h], c_old[h].astype(BF16), preferred_element_type=F32) for h in heads]
    i_col = [gc[:, h:h + 1] for h in heads]
    i_row = [gr[h:h + 1, :] for h in heads]
    b_col = [b_cols[:, H + h:H + h + 1] for h in heads]
    b_row = [b_rows[H + h:H + h + 1, :] for h in heads]
    m_prev = [m_s[h:h + 1, 0:1] for h in heads]
    dmat = [jnp.where(incl, b_col[h] - b_row[h] + i_row[h], NEG_BIG) for h in heads]
    inter = [b_col[h] + m_prev[h] for h in heads]
    m_t = [jnp.maximum(inter[h], jnp.max(dmat[h], -1, keepdims=True)) for h in heads]
    s = [(qk[h] * jnp.exp(dmat[h] - m_t[h])).astype(BF16) for h in heads]
    sv = [jnp.dot(s[h], vaug[h], preferred_element_type=F32) for h in heads]
    b_end = [b_col[h][C - 1:C, :] for h in heads]
    m_new = [jnp.maximum(b_end[h] + m_prev[h], jnp.max(b_end[h] - b_row[h] + i_row[h], -1, keepdims=True))
             for h in heads]
    kw = [(jnp.exp(b_end[h] - b_col[h] + i_col[h] - m_new[h]) * kh[h].astype(F32)).astype(BF16) for h in heads]
    kv = [_dg(kw[h], vaug[h], 0, 0) for h in heads]
    outs = []
    for h in heads:
        c_s[h] = jnp.exp(b_end[h] + m_prev[h] - m_new[h]) * c_old[h] + kv[h]
        m_s[h:h + 1, :] = jnp.broadcast_to(m_new[h], (1, 128))
        nd = jnp.exp(inter[h] - m_t[h]) * qc[h] + sv[h]
        hid = nd[:, :DH] / jnp.maximum(jnp.abs(nd[:, DH:DH + 1]), jnp.exp(-m_t[h]))
        hm = jnp.mean(hid, -1, keepdims=True)
        d = hid - hm
        hv = jnp.mean(d * d, -1, keepdims=True)
        outs.append(d * lax.rsqrt(hv + LN_EPS))
    hn = jnp.concatenate(outs, 1) * lng_ref[...]
    o_ref[...] = ((hn + skip_ref[...] * q_all) * jax.nn.silu(z_all)).astype(BF16)


def _mlstm(q, k, v, z, gates, ln_g, skip, B, L):
    NC, H2 = L // CHUNK, 2 * MLSTM_HEADS
    blk = pl.BlockSpec((CHUNK, HALF), lambda b, c: (b * NC + c, 0))
    vec = pl.BlockSpec((1, HALF), lambda b, c: (0, 0))
    g_rows = gates.reshape(B, NC, CHUNK, H2).transpose(0, 1, 3, 2)
    return pl.pallas_call(
        _mlstm_kernel,
        grid=(B, NC),
        in_specs=[blk, blk, blk, blk, pl.BlockSpec((CHUNK, H2), lambda b, c: (b * NC + c, 0)),
                  pl.BlockSpec((None, None, H2, CHUNK), lambda b, c: (b, c, 0, 0)), vec, vec],
        out_specs=blk,
        out_shape=jax.ShapeDtypeStruct((B * L, HALF), BF16),
        scratch_shapes=[pltpu.VMEM((MLSTM_HEADS, MLSTM_DH, 2 * MLSTM_DH), F32), pltpu.VMEM((8, 128), F32)],
        compiler_params=_cparams("parallel", "arbitrary"),
        name="mlstm",
    )(q, k, v, z, gates, g_rows, ln_g.reshape(1, HALF).astype(F32), skip.reshape(1, HALF).astype(F32))


def kernel(x, ev_w_in, ev_fox_fb, ev_s5_a_re, ev_s5_a_im, ev_s5_b_re, ev_s5_b_im, ev_s5_c_re, ev_s5_c_im, ev_s5_d, ev_s5_log_dt, ev_s5_w_glu, ev_s5_b_glu, ev_w_out, od_w_in, od_rwkv_mu, od_rwkv_w0, od_rwkv_w2, od_rwkv_a0, od_rwkv_a2, od_rwkv_g2, od_rwkv_k_k, od_rwkv_k_a, od_rwkv_r_k, od_rwkv_ln_g, od_rwkv_ln_b, od_mlstm_conv_w, od_mlstm_conv_b, od_mlstm_ib, od_mlstm_fb, od_mlstm_ln_g, od_mlstm_skip, od_w_out, ln1_g, ln1_b, ffn_w_up, ffn_conv_w, ffn_conv_b, ffn_w_down, ln2_g, ln2_b):
    B, L, _ = x.shape
    depth = ln1_g.shape[0]
    alpha = float((2 * depth) ** 0.25)
    h = x.reshape(B * L, D_MODEL).astype(F32)
    for layer in range(depth):
        i = layer // 2
        if layer % 2 == 0:
            qt, k, vt, fg, u = _even_pre(h, ev_w_in[i])
            f_row = _fox_gate(fg.reshape(B, L, FOX_HEADS).transpose(0, 2, 1), ev_fox_fb[i].astype(F32))
            f_col = jnp.pad(f_row.transpose(0, 2, 1).reshape(B * L, FOX_HEADS), ((0, 0), (0, 128 - FOX_HEADS)))
            fox = _fox_attention(qt, k, vt, f_col, f_row)
            mats = _s5_matrices(ev_s5_a_re[i], ev_s5_a_im[i], ev_s5_b_re[i], ev_s5_b_im[i],
                                ev_s5_c_re[i], ev_s5_c_im[i], ev_s5_d[i], ev_s5_log_dt[i])
            y = _s5(u, mats, B, L)
            h = _mix_out(fox, y, h, ev_w_out[i], ln1_g[layer], ln1_b[layer], alpha,
                         glu=(ev_s5_w_glu[i], ev_s5_b_glu[i]))
        else:
            (r, k, v, lw, kk, ka, g, bonus, mq, mk, mv, mz, gates) = _odd_pre(
                h, L, od_w_in[i], od_rwkv_mu[i], od_rwkv_w0[i], od_rwkv_w2[i], od_rwkv_a0[i], od_rwkv_a2[i],
                od_rwkv_g2[i], od_rwkv_k_k[i], od_rwkv_k_a[i], od_rwkv_r_k[i], od_mlstm_conv_w[i],
                od_mlstm_conv_b[i], od_mlstm_ib[i], od_mlstm_fb[i])
            c = _rwkv(r, k, v, lw, kk, ka, g, bonus, od_rwkv_ln_g[i], od_rwkv_ln_b[i], B, L)
            dm = _mlstm(mq, mk, mv, mz, gates, od_mlstm_ln_g[i], od_mlstm_skip[i], B, L)
            h = _mix_out(c, dm, h, od_w_out[i], ln1_g[layer], ln1_b[layer], alpha)
        h = _conv_ffn(h, L, ffn_w_up[layer], ffn_conv_w[layer], ffn_conv_b[layer], ffn_w_down[layer],
                      ln2_g[layer], ln2_b[layer], alpha)
    return h.reshape(B, L, D_MODEL).astype(x.dtype)
```

```python
import functools
import math

import jax
import jax.numpy as jnp
from jax import lax
from jax.experimental import pallas as pl
from jax.experimental.pallas import tpu as pltpu

F32 = jnp.float32
BF16 = jnp.bfloat16
HIGHEST = lax.Precision.HIGHEST

D_MODEL = 1024
HALF = D_MODEL // 2
FOX_HEADS = 8
FOX_DH = HALF // FOX_HEADS
FOX_TILE = 256
FOX_AUG = 256
S5_GROUPS = 32
S5_GROUP_CH = 16
S5_STATE = 64
S5_CHUNK = 16
S5_SLAB_GROUPS = 128 // S5_GROUP_CH
S5_SLABS = S5_GROUPS // S5_SLAB_GROUPS
S5_TILE_CHUNKS = 16
RWKV_HEADS = 8
RWKV_N = HALF // RWKV_HEADS
RWKV_DECAY_LORA = 64
RWKV_AAA_LORA = 64
RWKV_GATE_LORA = 160
RWKV_GN_EPS = 64e-5
RWKV_PASSES = 1
RWKV_STATE_PASSES = 3
MLSTM_HEADS = 4
MLSTM_DH = HALF // MLSTM_HEADS
MLSTM_CONV = 4
CHUNK = 64
SEQ_ROWS = 4
MLSTM_SEQ_ROWS = 1
D_FF = 2816
FFN_CONV = 3
FFN_TILE = 256
LN_EPS = 1e-5
HALO = 8
NEG_BIG = -1e30
V7X_VMEM_LIMIT_BYTES = 56 * 1024 * 1024


def _cparams(*sem):
    return pltpu.CompilerParams(dimension_semantics=sem, vmem_limit_bytes=V7X_VMEM_LIMIT_BYTES)


def _resident(shape):
    nd = len(shape)
    return pl.BlockSpec(shape, lambda *_: (0,) * nd, pipeline_mode=pl.Buffered(1))


def _rows(tm, width):
    return pl.BlockSpec((tm, width), lambda i: (i, 0))


def _halo_rows(tm, width):
    return pl.BlockSpec((HALO, width), lambda i: (jnp.maximum(i * (tm // HALO) - 1, 0), 0))


def _mm(a, b):
    return jnp.dot(a.astype(BF16), b.astype(BF16), preferred_element_type=F32)


def _dg(a, b, ca, cb):
    return lax.dot_general(a, b, (((ca,), (cb,)), ((), ())), preferred_element_type=F32)


def _hi_lo(a):
    hi = a.astype(BF16)
    lo = (a - hi.astype(F32)).astype(BF16)
    return hi, lo


def _mm3(a, b, ca=1, cb=0):
    ah, al = _hi_lo(a)
    bh, bl = _hi_lo(b)
    return _dg(ah, bh, ca, cb) + _dg(ah, bl, ca, cb) + _dg(al, bh, ca, cb)


def _mmp(a, b, ca=1, cb=0, passes=1):
    if passes == 3:
        return _mm3(a, b, ca, cb)
    return _dg(a.astype(BF16), b.astype(BF16), ca, cb)


def _split3(a):
    a1 = a.astype(BF16)
    r1 = a - a1.astype(F32)
    a2 = r1.astype(BF16)
    a3 = (r1 - a2.astype(F32)).astype(BF16)
    return a1, a2, a3


def _mm_exact_rhs(a, b01):
    a1, a2, a3 = _split3(a)
    return (jnp.dot(a1, b01, preferred_element_type=F32) + jnp.dot(a2, b01, preferred_element_type=F32)
            + jnp.dot(a3, b01, preferred_element_type=F32))


def _mm_exact_lhs(a01, b):
    b1, b2, b3 = _split3(b)
    return (jnp.dot(a01, b1, preferred_element_type=F32) + jnp.dot(a01, b2, preferred_element_type=F32)
            + jnp.dot(a01, b3, preferred_element_type=F32))


def _layer_norm(x, g, b):
    mu = jnp.mean(x, -1, keepdims=True)
    d = x - mu
    var = jnp.mean(d * d, -1, keepdims=True)
    return d * lax.rsqrt(var + LN_EPS) * g + b


def _iota2(shape):
    return lax.broadcasted_iota(jnp.int32, shape, 0), lax.broadcasted_iota(jnp.int32, shape, 1)


def _even_pre_kernel(x_ref, wqt_ref, wk_ref, wvt_ref, wf_ref, wu_ref, qt_ref, k_ref, vt_ref, fg_ref, u_ref):
    xb = x_ref[...].astype(BF16)
    qt_ref[...] = (_dg(wqt_ref[...], xb, 1, 1) * (FOX_DH ** -0.5)).astype(BF16)
    vt_ref[...] = _dg(wvt_ref[...], xb, 1, 1).astype(BF16)
    k_ref[...] = jnp.dot(xb, wk_ref[...], preferred_element_type=F32).astype(BF16)
    u = jnp.dot(xb, wu_ref[...], preferred_element_type=F32).astype(BF16)
    for s in range(S5_SLABS):
        u_ref[s] = u[:, 128 * s:128 * (s + 1)]
    fg_ref[...] = jnp.dot(xb, wf_ref[...], preferred_element_type=F32)[:, :FOX_HEADS]


def _even_pre(x, w_in, tm=512):
    T = x.shape[0]
    wb = w_in.astype(BF16)
    wqt, wk, wvt = wb[:, :HALF].T, wb[:, HALF:2 * HALF], wb[:, 2 * HALF:3 * HALF].T
    wf = jnp.pad(wb[:, 3 * HALF:3 * HALF + FOX_HEADS], ((0, 0), (0, 128 - FOX_HEADS)))
    wu = wb[:, 3 * HALF + FOX_HEADS:]
    half_out = jax.ShapeDtypeStruct((T, HALF), BF16)
    half_t = jax.ShapeDtypeStruct((HALF, T), BF16)
    cols = pl.BlockSpec((HALF, tm), lambda i: (0, i))
    return pl.pallas_call(
        _even_pre_kernel,
        grid=(T // tm,),
        in_specs=[_rows(tm, D_MODEL), _resident(wqt.shape), _resident(wk.shape), _resident(wvt.shape),
                  _resident(wf.shape), _resident(wu.shape)],
        out_specs=[cols, _rows(tm, HALF), cols, _rows(tm, FOX_HEADS),
                   pl.BlockSpec((S5_SLABS, tm, 128), lambda i: (0, i, 0))],
        out_shape=[half_t, half_out, half_t, jax.ShapeDtypeStruct((T, FOX_HEADS), F32),
                   jax.ShapeDtypeStruct((S5_SLABS, T, 128), BF16)],
        compiler_params=_cparams("parallel"),
        name="even_pre",
    )(x, wqt, wk, wvt, wf, wu)


def _fox_gate_kernel(fg_ref, fb_ref, o_ref, *, L):
    ls = jax.nn.log_sigmoid(fg_ref[...] + fb_ref[...])
    r, c = _iota2((128, 128))
    tri = (r <= c).astype(BF16)
    carry = jnp.zeros((FOX_HEADS, 1), F32)
    for j in range(L // 128):
        cum = _mm_exact_rhs(ls[:, j * 128:(j + 1) * 128], tri) + carry
        o_ref[:, j * 128:(j + 1) * 128] = cum
        carry = cum[:, 127:128]


def _fox_gate(fg_t, fb):
    B, H, L = fg_t.shape
    return pl.pallas_call(
        functools.partial(_fox_gate_kernel, L=L),
        grid=(B,),
        in_specs=[pl.BlockSpec((None, H, L), lambda b: (b, 0, 0)), _resident((H, 1))],
        out_specs=pl.BlockSpec((None, H, L), lambda b: (b, 0, 0)),
        out_shape=jax.ShapeDtypeStruct((B, H, L), F32),
        compiler_params=_cparams("parallel"),
        name="fox_gate",
    )(fg_t, fb.reshape(H, 1))


def _fox_kernel(qt_ref, k_ref, vt_ref, fc_ref, fr_ref, o_ref, qa_s, ka_s, m_s, l_s, acc_s, *, L):
    TQ = TK = FOX_TILE
    H, DH, KA = FOX_HEADS, FOX_DH, FOX_AUG
    heads = range(H)

    t1, t2, t3 = (t.astype(F32) for t in _split3(fr_ref[...]))
    r16 = lax.broadcasted_iota(jnp.int32, (16, 1), 0)
    upper = lax.broadcasted_iota(jnp.int32, (2 * DH, 1), 0) < DH
    for h in heads:
        p, hh = divmod(h, 2)
        qpair = qt_ref[2 * DH * p:2 * DH * (p + 1), :]
        keep = upper if hh == 0 else jnp.logical_not(upper)
        qa_s[h, 0:2 * DH, :] = jnp.where(keep, qpair, jnp.zeros_like(qpair))
        ones_rows = (r16 >= 3 + 3 * hh) & (r16 < 6 + 3 * hh)
        blk = jnp.where(r16 == 0, t1[h:h + 1], jnp.where(r16 == 1, t2[h:h + 1], jnp.where(
            r16 == 2, t3[h:h + 1], jnp.where(ones_rows, 1.0, 0.0))))
        qa_s[h, 2 * DH:2 * DH + 16, :] = blk.astype(BF16)
        qa_s[h, 2 * DH + 16:KA, :] = jnp.zeros((KA - 2 * DH - 16, L), BF16)
    c1, c2, c3 = _split3(fc_ref[...])
    rs, cs_ = _iota2((128, 128))
    lane = lax.broadcasted_iota(jnp.int32, (1, 128), 1)
    ones3 = jnp.where(lane < 3, 1.0, 0.0)
    for p in range(H // 2):
        def sel(i, p=p):
            hit = ((rs == 2 * p) & (cs_ == 3 + i)) | ((rs == 2 * p + 1) & (cs_ == 6 + i))
            return jnp.where(hit, -1.0, 0.0).astype(BF16)
        aug = (jnp.dot(c1, sel(0), preferred_element_type=F32) + jnp.dot(c2, sel(1), preferred_element_type=F32)
               + jnp.dot(c3, sel(2), preferred_element_type=F32) + ones3)
        ka_s[:, KA * p:KA * p + 2 * DH] = k_ref[:, 2 * DH * p:2 * DH * (p + 1)]
        ka_s[:, KA * p + 2 * DH:KA * (p + 1)] = aug.astype(BF16)

    ri, ci = _iota2((TK, TQ))
    visible = ri <= ci

    def q_block(qi, _):
        q0 = pl.multiple_of(qi * TQ, TQ)
        m_s[...] = jnp.full(m_s.shape, NEG_BIG, F32)
        l_s[...] = jnp.zeros(l_s.shape, F32)
        acc_s[...] = jnp.zeros(acc_s.shape, F32)

        def tile(k0, masked):
            kt = [ka_s[pl.ds(k0, TK), KA * p:KA * (p + 1)] for p in range(H // 2)]
            st = [jnp.dot(kt[h // 2], qa_s[h, :, pl.ds(q0, TQ)], preferred_element_type=F32) for h in heads]
            if masked:
                st = [jnp.where(visible, s, NEG_BIG) for s in st]
            m_old = [m_s[h:h + 1, :] for h in heads]
            m_new = [jnp.maximum(m_old[h], jnp.max(st[h], 0, keepdims=True)) for h in heads]
            pt = [jnp.exp(st[h] - m_new[h]) for h in heads]
            pv = [jnp.dot(vt_ref[DH * h:DH * (h + 1), pl.ds(k0, TK)], pt[h].astype(BF16),
                          preferred_element_type=F32) for h in heads]
            for h in heads:
                a = jnp.exp(m_old[h] - m_new[h])
                m_s[h:h + 1, :] = m_new[h]
                l_s[h:h + 1, :] = a * l_s[h:h + 1, :] + jnp.sum(pt[h], 0, keepdims=True)
                acc_s[DH * h:DH * (h + 1), :] = a * acc_s[DH * h:DH * (h + 1), :] + pv[h]

        def k_step(j, carry):
            tile(pl.multiple_of(j * TK, TK), False)
            return carry

        lax.fori_loop(0, qi, k_step, 0)
        tile(q0, True)
        for p in range(H // 2):
            o_pair = jnp.concatenate([acc_s[DH * h:DH * (h + 1), :] / l_s[h:h + 1, :] for h in (2 * p, 2 * p + 1)], 0)
            o_ref[pl.ds(q0, TQ), 2 * DH * p:2 * DH * (p + 1)] = o_pair.T.astype(BF16)
        return 0

    lax.fori_loop(0, L // TQ, q_block, 0)


def _fox_attention(qt, k, vt, f_col, f_row):
    B, H, L = f_row.shape
    seq = pl.BlockSpec((L, HALF), lambda b: (b, 0))
    seq_t = pl.BlockSpec((HALF, L), lambda b: (0, b))
    return pl.pallas_call(
        functools.partial(_fox_kernel, L=L),
        grid=(B,),
        in_specs=[seq_t, seq, seq_t, pl.BlockSpec((L, 128), lambda b: (b, 0)),
                  pl.BlockSpec((None, H, L), lambda b: (b, 0, 0))],
        out_specs=seq,
        out_shape=jax.ShapeDtypeStruct((B * L, HALF), BF16),
        scratch_shapes=[pltpu.VMEM((H, FOX_AUG, L), BF16), pltpu.VMEM((L, FOX_AUG * H // 2), BF16),
                        pltpu.VMEM((H, FOX_TILE), F32), pltpu.VMEM((H, FOX_TILE), F32),
                        pltpu.VMEM((HALF, FOX_TILE), F32)],
        compiler_params=_cparams("parallel"),
        name="fox_attention",
    )(qt, k, vt, f_col, f_row)


def _s5_matrices(a_re, a_im, b_re, b_im, c_re, c_im, d, log_dt):
    G, P, Cg, LC = S5_GROUPS, S5_STATE, S5_GROUP_CH, S5_CHUNK
    a_re, a_im, b_re, b_im, c_re, c_im = (t.astype(F32) for t in (a_re, a_im, b_re, b_im, c_re, c_im))
    dt = jnp.exp(log_dt.astype(F32))[:, None]
    mag = jnp.exp(a_re * dt)
    lam_re, lam_im = mag * jnp.cos(a_im * dt), mag * jnp.sin(a_im * dt)
    den = a_re ** 2 + a_im ** 2
    zr = ((lam_re - 1.0) * a_re + lam_im * a_im) / den
    zi = (lam_im * a_re - (lam_re - 1.0) * a_im) / den
    bb_re = zr[..., None] * b_re - zi[..., None] * b_im
    bb_im = zr[..., None] * b_im + zi[..., None] * b_re
    n = jnp.arange(LC + 1, dtype=F32)[:, None, None]
    pw_mag = jnp.exp(n * (a_re * dt)[None])
    pr, pi = pw_mag * jnp.cos(n * (a_im * dt)[None]), pw_mag * jnp.sin(n * (a_im * dt)[None])
    ein = functools.partial(jnp.einsum, precision=HIGHEST)
    cr = c_re[None] * pr[:, :, None, :] - c_im[None] * pi[:, :, None, :]
    ci = -(c_re[None] * pi[:, :, None, :] + c_im[None] * pr[:, :, None, :])
    kn = ein('ngcp,gpd->ngcd', cr[:LC], bb_re) + ein('ngcp,gpd->ngcd', ci[:LC], bb_im)
    s_idx = jnp.arange(LC)
    lag = s_idx[None, :] - s_idx[:, None]
    kt = jnp.where((lag >= 0)[:, :, None, None, None],
                   kn[jnp.clip(lag, 0, LC - 1)], 0.0)
    dmat = (lag == 0)[:, :, None, None, None] * (d.astype(F32).reshape(1, 1, G, Cg, 1)
                                                 * jnp.eye(Cg, dtype=F32)[None, None, None])
    NS, GS = S5_SLABS, S5_SLAB_GROUPS
    eye = jnp.eye(GS, dtype=F32)
    k6 = (kt + dmat).reshape(LC, LC, NS, GS, Cg, Cg)
    kmat = jnp.einsum('stbgoi,gh->bsgitho', k6, eye).reshape(NS, LC * 128, LC * 128)
    rr, ri = pr[LC - 1 - s_idx], pi[LC - 1 - s_idx]
    e_re = rr[..., None] * bb_re[None] - ri[..., None] * bb_im[None]
    e_im = rr[..., None] * bb_im[None] + ri[..., None] * bb_re[None]
    e6 = jnp.stack([e_re, e_im]).reshape(2, LC, NS, GS, P, Cg)
    emat = jnp.einsum('zsbgpc,gh->bsgczhp', e6, eye).reshape(NS, LC * 128, 2 * GS * P)
    f6 = jnp.stack([cr[1:], ci[1:]]).reshape(2, LC, NS, GS, Cg, P)
    fmat = jnp.einsum('ztbgcp,gh->bzgpthc', f6, eye).reshape(NS, 2 * GS * P, LC * 128)
    lr, li = pr[LC].reshape(NS, 1, GS * P), pi[LC].reshape(NS, 1, GS * P)
    lam_a = jnp.concatenate([lr, lr], axis=-1)
    lam_b = jnp.concatenate([-li, li], axis=-1)
    return kmat.astype(BF16), emat.astype(BF16), fmat.astype(BF16), lam_a, lam_b


def _s5_kernel(u_ref, k_ref, e_ref, f_ref, la_ref, lb_ref, y_ref, e_s, hs_s, h_s, *, B):
    @pl.when(pl.program_id(1) == 0)
    def _():
        h_s[...] = jnp.zeros_like(h_s)

    u = u_ref[...]
    e_s[...] = jnp.dot(u, e_ref[...], preferred_element_type=F32)
    la, lb = la_ref[...], lb_ref[...]
    half = h_s.shape[1] // 2
    h = h_s[...]
    for kc in range(u.shape[0] // B):
        rs = slice(kc * B, (kc + 1) * B)
        hs_s[rs, :] = h
        swapped = jnp.concatenate([h[:, half:], h[:, :half]], axis=1)
        h = la * h + lb * swapped + e_s[rs, :]
    h_s[...] = h
    y_ref[...] = (jnp.dot(u, k_ref[...], preferred_element_type=F32)
                  + jnp.dot(hs_s[...].astype(BF16), f_ref[...], preferred_element_type=F32)).astype(BF16)


def _s5(u4, mats, B, L):
    kmat, emat, fmat, lam_a, lam_b = mats
    NS, LC = S5_SLABS, S5_CHUNK
    NK = L // LC
    R, W, S2 = NK * B, LC * 128, emat.shape[2]
    ug = u4.reshape(NS, B, NK, W).transpose(0, 2, 1, 3).reshape(NS, R, W)
    TR = B * min(NK, S5_TILE_CHUNKS)
    rows = pl.BlockSpec((None, TR, W), lambda s, r: (s, r, 0))
    per_s = lambda a, b: pl.BlockSpec((None, a, b), lambda s, r: (s, 0, 0), pipeline_mode=pl.Buffered(1))
    y = pl.pallas_call(
        functools.partial(_s5_kernel, B=B),
        grid=(NS, R // TR),
        in_specs=[rows, per_s(W, W), per_s(W, S2), per_s(S2, W), per_s(1, S2), per_s(1, S2)],
        out_specs=rows,
        out_shape=jax.ShapeDtypeStruct((NS, R, W), BF16),
        scratch_shapes=[pltpu.VMEM((TR, S2), F32), pltpu.VMEM((TR, S2), F32), pltpu.VMEM((B, S2), F32)],
        compiler_params=_cparams("parallel", "arbitrary"),
        name="s5",
    )(ug, kmat, emat, fmat, lam_a, lam_b)
    return y.reshape(NS, NK, B, W).transpose(0, 2, 1, 3).reshape(NS, B * L, 128)


def _mix_kernel(a_ref, b_ref, x_ref, wt_ref, wb_ref, g_ref, beta_ref, *rest, even, alpha):
    if even:
        wg_ref, bg_ref, o_ref = rest
        z = jax.nn.gelu(jnp.concatenate([b_ref[s] for s in range(S5_SLABS)], axis=1).astype(F32))
        second = z * jax.nn.sigmoid(_mm(z, wg_ref[...]) + bg_ref[...])
    else:
        (o_ref,) = rest
        second = b_ref[...]
    mix = _mm(a_ref[...], wt_ref[...]) + _mm(second, wb_ref[...])
    o_ref[...] = _layer_norm(alpha * x_ref[...] + mix, g_ref[...], beta_ref[...])


def _mix_out(a, b, x, w_out, ln_g, ln_b, alpha, glu=None, tm=512):
    T = x.shape[0]
    wb16 = w_out.astype(BF16)
    args = [a, b, x, wb16[:HALF], wb16[HALF:], ln_g.reshape(1, D_MODEL), ln_b.reshape(1, D_MODEL)]
    specs = [_rows(tm, HALF), _rows(tm, HALF), _rows(tm, D_MODEL), _resident((HALF, D_MODEL)),
             _resident((HALF, D_MODEL)), _resident((1, D_MODEL)), _resident((1, D_MODEL))]
    if glu is not None:
        w_glu, b_glu = glu
        specs[1] = pl.BlockSpec((S5_SLABS, tm, 128), lambda i: (0, i, 0))
        args += [w_glu.astype(BF16), b_glu.reshape(1, HALF).astype(F32)]
        specs += [_resident((HALF, HALF)), _resident((1, HALF))]
    return pl.pallas_call(
        functools.partial(_mix_kernel, even=glu is not None, alpha=alpha),
        grid=(T // tm,),
        in_specs=specs,
        out_specs=_rows(tm, D_MODEL),
        out_shape=jax.ShapeDtypeStruct((T, D_MODEL), F32),
        compiler_params=_cparams("parallel"),
        name="mix_out",
    )(*args)


def _ffn_kernel(x_ref, xh_ref, wu_ref, cw_ref, cb_ref, wd_ref, g_ref, beta_ref, o_ref, *, tiles_per_seq, alpha):
    tm = x_ref.shape[0]
    first = (pl.program_id(0) % tiles_per_seq) == 0
    x = x_ref[...]
    xh = jnp.where(first, 0.0, xh_ref[...])
    xe = jnp.concatenate([xh, x], axis=0).astype(BF16)
    xb = xe[HALO:]
    def up(c):
        ue = jnp.dot(xe, wu_ref[:, c * FFN_TILE:(c + 1) * FFN_TILE], preferred_element_type=F32)
        gate = jnp.dot(xb, wu_ref[:, D_FF + c * FFN_TILE:D_FF + (c + 1) * FFN_TILE], preferred_element_type=F32)
        return ue, gate

    acc = jnp.zeros((tm, D_MODEL), F32)
    n_tiles = D_FF // FFN_TILE
    nxt = up(0)
    for c in range(n_tiles):
        cs = slice(c * FFN_TILE, (c + 1) * FFN_TILE)
        ue, gate = nxt
        if c + 1 < n_tiles:
            nxt = up(c + 1)
        cw = cw_ref[:, cs]
        u = (cw[2:3] * ue[HALO:] + cw[1:2] * ue[HALO - 1:HALO - 1 + tm] + cw[0:1] * ue[HALO - 2:HALO - 2 + tm]
             + cb_ref[:, cs])
        acc = acc + _mm(jax.nn.gelu(u) * gate, wd_ref[cs, :])
    o_ref[...] = _layer_norm(alpha * x + acc, g_ref[...], beta_ref[...])


def _conv_ffn(x, L, w_up, conv_w, conv_b, w_down, ln_g, ln_b, alpha, tm=512):
    T = x.shape[0]
    tm = min(tm, L)
    return pl.pallas_call(
        functools.partial(_ffn_kernel, tiles_per_seq=L // tm, alpha=alpha),
        grid=(T // tm,),
        in_specs=[_rows(tm, D_MODEL), _halo_rows(tm, D_MODEL), _resident((D_MODEL, 2 * D_FF)),
                  _resident((FFN_CONV, D_FF)), _resident((1, D_FF)), _resident((D_FF, D_MODEL)),
                  _resident((1, D_MODEL)), _resident((1, D_MODEL))],
        out_specs=_rows(tm, D_MODEL),
        out_shape=jax.ShapeDtypeStruct((T, D_MODEL), F32),
        compiler_params=_cparams("parallel"),
        name="conv_ffn",
    )(x, x, w_up.astype(BF16), conv_w.astype(F32), conv_b.reshape(1, D_FF).astype(F32),
      w_down.astype(BF16), ln_g.reshape(1, D_MODEL), ln_b.reshape(1, D_MODEL))


_LORA_PAD = (128, 128, 256)
_RWKV_PAD = 3 * HALF + sum(_LORA_PAD)


def _odd_pre_kernel(x_ref, xh_ref, wrk_ref, mu_ref, w2_ref, a2_ref, g2_ref, vec_ref, bd_ref,
                    wqk_ref, cw_ref, cb_ref, wv_ref, wz_ref, wif_ref, gb_ref,
                    r_ref, k_ref, v_ref, lw_ref, kk_ref, ka_ref, g_ref, bo_ref,
                    mq_ref, mk_ref, mv_ref, mz_ref, gate_ref, *, tiles_per_seq):
    tm = x_ref.shape[0]
    first = (pl.program_id(0) % tiles_per_seq) == 0
    xh = jnp.where(first, 0.0, xh_ref[...])
    xe = jnp.concatenate([xh, x_ref[...]], axis=0).astype(BF16)
    xb = xe[HALO:]

    pe = jnp.dot(xe, wrk_ref[...], preferred_element_type=F32)
    qke = jnp.dot(xe, wqk_ref[...], preferred_element_type=F32)
    mv_ref[...] = jnp.dot(xb, wv_ref[...], preferred_element_type=F32).astype(BF16)
    mz_ref[...] = jnp.dot(xb, wz_ref[...], preferred_element_type=F32).astype(mz_ref.dtype)
    pre = jnp.dot(xb, wif_ref[...], preferred_element_type=F32)[:, :2 * MLSTM_HEADS] + gb_ref[...]

    cur, prev = pe[HALO:], pe[HALO - 1:HALO - 1 + tm]
    p = cur + (prev - cur) * mu_ref[...]
    r, k, v = p[:, :HALF], p[:, HALF:2 * HALF], p[:, 2 * HALF:3 * HALF]
    o0 = 3 * HALF
    wd = p[:, o0:o0 + _LORA_PAD[0]]
    ad = p[:, o0 + _LORA_PAD[0]:o0 + _LORA_PAD[0] + _LORA_PAD[1]]
    gd = p[:, o0 + _LORA_PAD[0] + _LORA_PAD[1]:]
    w0, a0, k_k, k_a, r_k = (vec_ref[i:i + 1, :] for i in range(5))
    wlog = -jax.nn.softplus(-(w0 + _mm(jnp.tanh(wd), w2_ref[...]))) - 0.5
    lw_ref[...] = -jnp.exp(wlog)
    a = jax.nn.sigmoid(a0 + _mm(ad, a2_ref[...]))
    g_ref[...] = _mm(jax.nn.sigmoid(gd), g2_ref[...]).astype(g_ref.dtype)
    kk = k * k_k
    ss = _mm_exact_rhs(kk * kk, bd_ref[...])
    kk = kk * lax.rsqrt(jnp.maximum(ss, 1e-24))
    kmod = k * (1.0 + (a - 1.0) * k_a)
    bo_ref[...] = (_mm_exact_rhs(r * kmod * r_k, bd_ref[...]) * v).astype(bo_ref.dtype)
    r_ref[...] = r.astype(r_ref.dtype)
    k_ref[...] = kmod.astype(k_ref.dtype)
    v_ref[...] = v.astype(v_ref.dtype)
    kk_ref[...] = kk.astype(kk_ref.dtype)
    ka_ref[...] = (kk * a).astype(ka_ref.dtype)

    cw = cw_ref[...]
    qk = cb_ref[...]
    for j in range(MLSTM_CONV):
        off = HALO - (MLSTM_CONV - 1) + j
        qk = qk + cw[j:j + 1] * qke[off:off + tm]
    qk = jax.nn.silu(qk)
    mq_ref[...] = qk[:, :HALF].astype(mq_ref.dtype)
    mk_ref[...] = (qk[:, HALF:] * (MLSTM_DH ** -0.5)).astype(BF16)
    is_i = lax.broadcasted_iota(jnp.int32, pre.shape, 1) < MLSTM_HEADS
    gate_ref[...] = jnp.where(is_i, pre, jax.nn.log_sigmoid(pre))


def _head_block_ones(width, head):
    idx = jnp.arange(width) // head
    return (idx[:, None] == idx[None, :]).astype(BF16)


def _odd_pre(x, L, w_in, mu, w0, w2, a0, a2, g2, k_k, k_a, r_k, conv_w, conv_b, ib, fb, tm=256):
    T = x.shape[0]
    tm = min(tm, L)
    wb = w_in.astype(BF16)
    o = 3 * HALF
    sizes = (RWKV_DECAY_LORA, RWKV_AAA_LORA, RWKV_GATE_LORA)

    def pad_lora(m, axis):
        parts, s = [], o
        for sz, pd in zip(sizes, _LORA_PAD):
            piece = lax.slice_in_dim(m, s, s + sz, axis=axis)
            widths = [(0, 0)] * m.ndim
            widths[axis] = (0, pd - sz)
            parts.append(jnp.pad(piece, widths))
            s += sz
        return jnp.concatenate([lax.slice_in_dim(m, 0, o, axis=axis)] + parts, axis=axis)

    rwkv_proj = o + sum(sizes)
    wrk = pad_lora(wb[:, :rwkv_proj], 1)
    mu_p = pad_lora(mu.astype(F32).reshape(1, -1), 1)
    padr = lambda m, rows: jnp.pad(m.astype(BF16), ((0, rows - m.shape[0]), (0, 0)))
    w2p, a2p, g2p = padr(w2, _LORA_PAD[0]), padr(a2, _LORA_PAD[1]), padr(g2, _LORA_PAD[2])
    vecs = jnp.stack([w0, a0, k_k, k_a, r_k.reshape(HALF)]).astype(F32)
    vecs = jnp.pad(vecs, ((0, 8 - vecs.shape[0]), (0, 0)))
    bd = _head_block_ones(HALF, RWKV_N)
    wm = wb[:, rwkv_proj:]
    wqk, wmv, wmz = wm[:, :2 * HALF], wm[:, 2 * HALF:3 * HALF], wm[:, 3 * HALF:4 * HALF]
    wif = jnp.pad(wm[:, 4 * HALF:], ((0, 0), (0, 128 - 2 * MLSTM_HEADS)))
    gbias = jnp.concatenate([ib, fb]).astype(F32).reshape(1, 2 * MLSTM_HEADS)
    args = [x, x, wrk, mu_p, w2p, a2p, g2p, vecs, bd, wqk, conv_w.astype(F32),
            conv_b.astype(F32).reshape(1, 2 * HALF), wmv, wmz, wif, gbias]
    specs = [_rows(tm, D_MODEL), _halo_rows(tm, D_MODEL)] + [_resident(a.shape) for a in args[2:]]
    f_half = jax.ShapeDtypeStruct((T, HALF), F32)
    b_half = jax.ShapeDtypeStruct((T, HALF), BF16)
    out_shape = [b_half] * 3 + [f_half] + [b_half] * 8 + [jax.ShapeDtypeStruct((T, 2 * MLSTM_HEADS), F32)]
    out_specs = [_rows(tm, HALF)] * 12 + [_rows(tm, 2 * MLSTM_HEADS)]
    return pl.pallas_call(
        functools.partial(_odd_pre_kernel, tiles_per_seq=L // tm),
        grid=(T // tm,),
        in_specs=specs,
        out_specs=out_specs,
        out_shape=out_shape,
        compiler_params=_cparams("parallel"),
        name="odd_pre",
    )(*args)


def _rwkv_kernel(r_ref, k_ref, v_ref, lw_ref, kk_ref, ka_ref, g_ref, bo_ref, lng_ref, lnb_ref, o_ref, m_s):
    C, N, H = CHUNK, RWKV_N, RWKV_HEADS
    rows = range(r_ref.shape[0])
    units = [(b, h) for b in rows for h in range(H)]

    @pl.when(pl.program_id(1) == 0)
    def _():
        m_s[...] = jnp.zeros_like(m_s)

    ri, ci = _iota2((C, C))
    incl, strict, eye = ci <= ri, ci < ri, ci == ri
    ri2, ci2 = _iota2((C, 2 * C))
    incl2 = (ci2 & (C - 1)) <= ri2
    zeros = jnp.zeros((C, N), F32)
    mm = functools.partial(_mmp, passes=RWKV_PASSES)
    al, rt, bt, kt, bp, kp, gam, v = ({} for _ in range(8))
    for b in rows:
        lw = lw_ref[b]
        cs = _mm_exact_lhs(incl.astype(BF16), lw)
        cend = cs[C - 1:C, :]
        e_neg = jnp.exp(-cs)
        e_rem = jnp.exp(cend - cs)
        r, k, vv, kk, ka = (t[b].astype(F32) for t in (r_ref, k_ref, v_ref, kk_ref, ka_ref))
        al_b = -kk * jnp.exp(cs - lw)
        rt_b = r * jnp.exp(cs)
        gam_b = jnp.exp(cend)
        bt_b, kt_b, bp_b, kp_b = ka * e_neg, k * e_neg, ka * e_rem, k * e_rem
        for h in range(H):
            s = slice(N * h, N * (h + 1))
            u = (b, h)
            al[u], rt[u], v[u], gam[u] = al_b[:, s], rt_b[:, s], vv[:, s], gam_b[:, s]
            bt[u], kt[u], bp[u], kp[u] = bt_b[:, s], kt_b[:, s], bp_b[:, s], kp_b[:, s]
    pm = {u: mm(jnp.concatenate([al[u], rt[u]], 0), jnp.concatenate([bt[u], kt[u]], 0), 1, 1) for u in units}
    a_ab = {u: jnp.where(strict, pm[u][:C, :C], 0.0) for u in units}
    a_ak = {u: jnp.where(strict, pm[u][:C, C:], 0.0) for u in units}
    a_r = {u: jnp.where(incl2, pm[u][C:, :], 0.0) for u in units}
    w = {u: jnp.concatenate([al[u], mm(a_ak[u], v[u])], 1) for u in units}
    npow = a_ab
    levels = int(math.log2(C))
    for lvl in range(levels):
        if lvl < levels - 1:
            y = {u: mm(npow[u], jnp.concatenate([w[u], npow[u]], 1)) for u in units}
            w = {u: w[u] + y[u][:, :2 * N] for u in units}
            npow = {u: y[u][:, 2 * N:] for u in units}
        else:
            w = {u: w[u] + mm(npow[u], w[u]) for u in units}
    zv = {u: jnp.concatenate([zeros, v[u]], 1) for u in units}
    qt = {u: mm(a_r[u], jnp.concatenate([w[u], zv[u]], 0)) for u in units}
    qb = {u: mm(bp[u], w[u], 0, 0) + mm(kp[u], zv[u], 0, 0) for u in units}
    m0 = {u: m_s[u[0], u[1]] for u in units}
    o_h = {u: mm(rt[u] + qt[u][:, :N], m0[u]) + qt[u][:, N:] for u in units}
    m_new = {u: _mmp(qb[u][:, :N] + jnp.where(eye, gam[u], 0.0), m0[u], passes=RWKV_STATE_PASSES) + qb[u][:, N:]
             for u in units}
    for b in rows:
        outs = []
        for h in range(H):
            u = (b, h)
            m_s[b, h] = m_new[u]
            om = jnp.mean(o_h[u], -1, keepdims=True)
            d = o_h[u] - om
            ov = jnp.mean(d * d, -1, keepdims=True)
            outs.append(d * lax.rsqrt(ov + RWKV_GN_EPS))
        on = jnp.concatenate(outs, 1) * lng_ref[...] + lnb_ref[...]
        o_ref[b] = ((on + bo_ref[b].astype(F32)) * g_ref[b].astype(F32)).astype(BF16)


def _rwkv(r, k, v, lw, kk, ka, g, bonus, ln_g, ln_b, B, L):
    NC = L // CHUNK
    RB = SEQ_ROWS if B % SEQ_ROWS == 0 else 1
    seq3 = lambda t: t.reshape(B, L, HALF)
    blk = pl.BlockSpec((RB, CHUNK, HALF), lambda b, c: (b, c, 0))
    vec = pl.BlockSpec((1, HALF), lambda b, c: (0, 0))
    out = pl.pallas_call(
        _rwkv_kernel,
        grid=(B // RB, NC),
        in_specs=[blk] * 8 + [vec, vec],
        out_specs=blk,
        out_shape=jax.ShapeDtypeStruct((B, L, HALF), BF16),
        scratch_shapes=[pltpu.VMEM((RB, RWKV_HEADS, RWKV_N, RWKV_N), F32)],
        compiler_params=_cparams("parallel", "arbitrary"),
        name="rwkv7",
    )(*(seq3(t) for t in (r, k, v, lw, kk, ka, g, bonus)),
      ln_g.reshape(1, HALF).astype(F32), ln_b.reshape(1, HALF).astype(F32))
    return out.reshape(B * L, HALF)


def _mlstm_kernel(q_ref, k_ref, v_ref, z_ref, gc_ref, gr_ref, lng_ref, skip_ref, o_ref, c_s, m_s):
    C, H, DH = CHUNK, MLSTM_HEADS, MLSTM_DH
    rows = range(q_ref.shape[0])
    units = [(b, h) for b in rows for h in range(H)]

    @pl.when(pl.program_id(1) == 0)
    def _():
        c_s[...] = jnp.zeros_like(c_s)
        m_s[...] = jnp.zeros_like(m_s)

    ri, ci = _iota2((C, C))
    incl = ci <= ri
    lower, upper = incl.astype(BF16), (ri <= ci).astype(BF16)
    lane = lax.broadcasted_iota(jnp.int32, (1, DH), 1)
    ones_blk = jnp.broadcast_to(jnp.where(lane == 0, 1.0, 0.0).astype(BF16), (C, DH))
    q_all, qb, kh, vaug, i_col, i_row, b_col, b_row, m_prev = ({} for _ in range(9))
    for b in rows:
        gc, gr = gc_ref[b], gr_ref[b]
        b_cols = _mm_exact_lhs(lower, gc)
        b_rows = _mm_exact_rhs(gr, upper)
        q_all[b] = q_ref[b].astype(F32)
        for h in range(H):
            u, s = (b, h), slice(DH * h, DH * (h + 1))
            qb[u], kh[u] = q_ref[b, :, s], k_ref[b, :, s]
            vaug[u] = jnp.concatenate([v_ref[b, :, s], ones_blk], 1)
            i_col[u], i_row[u] = gc[:, h:h + 1], gr[h:h + 1, :]
            b_col[u], b_row[u] = b_cols[:, H + h:H + h + 1], b_rows[H + h:H + h + 1, :]
            m_prev[u] = m_s[b, h:h + 1, 0:1]
    c_old = {u: c_s[u[0], u[1]] for u in units}
    qk = {u: _dg(qb[u], kh[u], 1, 1) for u in units}
    qc = {u: jnp.dot(qb[u], c_old[u].astype(BF16), preferred_element_type=F32) for u in units}
    dmat = {u: jnp.where(incl, b_col[u] - b_row[u] + i_row[u], NEG_BIG) for u in units}
    inter = {u: b_col[u] + m_prev[u] for u in units}
    m_t = {u: jnp.maximum(inter[u], jnp.max(dmat[u], -1, keepdims=True)) for u in units}
    s = {u: (qk[u] * jnp.exp(dmat[u] - m_t[u])).astype(BF16) for u in units}
    sv = {u: jnp.dot(s[u], vaug[u], preferred_element_type=F32) for u in units}
    b_end = {u: b_col[u][C - 1:C, :] for u in units}
    m_new = {u: jnp.maximum(b_end[u] + m_prev[u], jnp.max(b_end[u] - b_row[u] + i_row[u], -1, keepdims=True))
             for u in units}
    kw = {u: (jnp.exp(b_end[u] - b_col[u] + i_col[u] - m_new[u]) * kh[u].astype(F32)).astype(BF16) for u in units}
    kv = {u: _dg(kw[u], vaug[u], 0, 0) for u in units}
    for b in rows:
        outs = []
        for h in range(H):
            u = (b, h)
            c_s[b, h] = jnp.exp(b_end[u] + m_prev[u] - m_new[u]) * c_old[u] + kv[u]
            m_s[b, h:h + 1, :] = jnp.broadcast_to(m_new[u], (1, 128))
            nd = jnp.exp(inter[u] - m_t[u]) * qc[u] + sv[u]
            hid = nd[:, :DH] / jnp.maximum(jnp.abs(nd[:, DH:DH + 1]), jnp.exp(-m_t[u]))
            hm = jnp.mean(hid, -1, keepdims=True)
            d = hid - hm
            hv = jnp.mean(d * d, -1, keepdims=True)
            outs.append(d * lax.rsqrt(hv + LN_EPS))
        hn = jnp.concatenate(outs, 1) * lng_ref[...]
        o_ref[b] = ((hn + skip_ref[...] * q_all[b]) * jax.nn.silu(z_ref[b].astype(F32))).astype(BF16)


def _mlstm(q, k, v, z, gates, ln_g, skip, B, L):
    NC, H2 = L // CHUNK, 2 * MLSTM_HEADS
    RB = MLSTM_SEQ_ROWS if B % MLSTM_SEQ_ROWS == 0 else 1
    seq3 = lambda t: t.reshape(B, L, t.shape[-1])
    blk = pl.BlockSpec((RB, CHUNK, HALF), lambda b, c: (b, c, 0))
    vec = pl.BlockSpec((1, HALF), lambda b, c: (0, 0))
    g_rows = gates.reshape(B, NC, CHUNK, H2).transpose(0, 1, 3, 2)
    out = pl.pallas_call(
        _mlstm_kernel,
        grid=(B // RB, NC),
        in_specs=[blk, blk, blk, blk, pl.BlockSpec((RB, CHUNK, H2), lambda b, c: (b, c, 0)),
                  pl.BlockSpec((RB, None, H2, CHUNK), lambda b, c: (b, c, 0, 0)), vec, vec],
        out_specs=blk,
        out_shape=jax.ShapeDtypeStruct((B, L, HALF), BF16),
        scratch_shapes=[pltpu.VMEM((RB, MLSTM_HEADS, MLSTM_DH, 2 * MLSTM_DH), F32),
                        pltpu.VMEM((RB, 8, 128), F32)],
        compiler_params=_cparams("parallel", "arbitrary"),
        name="mlstm",
    )(seq3(q), seq3(k), seq3(v), seq3(z), seq3(gates), g_rows,
      ln_g.reshape(1, HALF).astype(F32), skip.reshape(1, HALF).astype(F32))
    return out.reshape(B * L, HALF)


def kernel(x, ev_w_in, ev_fox_fb, ev_s5_a_re, ev_s5_a_im, ev_s5_b_re, ev_s5_b_im, ev_s5_c_re, ev_s5_c_im, ev_s5_d, ev_s5_log_dt, ev_s5_w_glu, ev_s5_b_glu, ev_w_out, od_w_in, od_rwkv_mu, od_rwkv_w0, od_rwkv_w2, od_rwkv_a0, od_rwkv_a2, od_rwkv_g2, od_rwkv_k_k, od_rwkv_k_a, od_rwkv_r_k, od_rwkv_ln_g, od_rwkv_ln_b, od_mlstm_conv_w, od_mlstm_conv_b, od_mlstm_ib, od_mlstm_fb, od_mlstm_ln_g, od_mlstm_skip, od_w_out, ln1_g, ln1_b, ffn_w_up, ffn_conv_w, ffn_conv_b, ffn_w_down, ln2_g, ln2_b):
    B, L, _ = x.shape
    depth = ln1_g.shape[0]
    alpha = float((2 * depth) ** 0.25)
    h = x.reshape(B * L, D_MODEL).astype(F32)
    for layer in range(depth):
        i = layer // 2
        if layer % 2 == 0:
            qt, k, vt, fg, u = _even_pre(h, ev_w_in[i])
            f_row = _fox_gate(fg.reshape(B, L, FOX_HEADS).transpose(0, 2, 1), ev_fox_fb[i].astype(F32))
            f_col = jnp.pad(f_row.transpose(0, 2, 1).reshape(B * L, FOX_HEADS), ((0, 0), (0, 128 - FOX_HEADS)))
            fox = _fox_attention(qt, k, vt, f_col, f_row)
            mats = _s5_matrices(ev_s5_a_re[i], ev_s5_a_im[i], ev_s5_b_re[i], ev_s5_b_im[i],
                                ev_s5_c_re[i], ev_s5_c_im[i], ev_s5_d[i], ev_s5_log_dt[i])
            y = _s5(u, mats, B, L)
            h = _mix_out(fox, y, h, ev_w_out[i], ln1_g[layer], ln1_b[layer], alpha,
                         glu=(ev_s5_w_glu[i], ev_s5_b_glu[i]))
        else:
            (r, k, v, lw, kk, ka, g, bonus, mq, mk, mv, mz, gates) = _odd_pre(
                h, L, od_w_in[i], od_rwkv_mu[i], od_rwkv_w0[i], od_rwkv_w2[i], od_rwkv_a0[i], od_rwkv_a2[i],
                od_rwkv_g2[i], od_rwkv_k_k[i], od_rwkv_k_a[i], od_rwkv_r_k[i], od_mlstm_conv_w[i],
                od_mlstm_conv_b[i], od_mlstm_ib[i], od_mlstm_fb[i])
            c = _rwkv(r, k, v, lw, kk, ka, g, bonus, od_rwkv_ln_g[i], od_rwkv_ln_b[i], B, L)
            dm = _mlstm(mq, mk, mv, mz, gates, od_mlstm_ln_g[i], od_mlstm_skip[i], B, L)
            h = _mix_out(c, dm, h, od_w_out[i], ln1_g[layer], ln1_b[layer], alpha)
        h = _conv_ffn(h, L, ffn_w_up[layer], ffn_conv_w[layer], ffn_conv_b[layer], ffn_w_down[layer],
                      ln2_g[layer], ln2_b[layer], alpha)
    return h.reshape(B, L, D_MODEL).astype(x.dtype)
```

```python
import functools
import math

import jax
import jax.numpy as jnp
from jax import lax
from jax.experimental import pallas as pl
from jax.experimental.pallas import tpu as pltpu

F32 = jnp.float32
BF16 = jnp.bfloat16
HIGHEST = lax.Precision.HIGHEST

D_MODEL = 1024
HALF = D_MODEL // 2
FOX_HEADS = 8
FOX_DH = HALF // FOX_HEADS
FOX_TILE = 256
FOX_AUG = 256
S5_GROUPS = 32
S5_GROUP_CH = 16
S5_STATE = 64
S5_CHUNK = 16
S5_SLAB_GROUPS = 128 // S5_GROUP_CH
S5_SLABS = S5_GROUPS // S5_SLAB_GROUPS
S5_TILE_CHUNKS = 16
RWKV_HEADS = 8
RWKV_N = HALF // RWKV_HEADS
RWKV_DECAY_LORA = 64
RWKV_AAA_LORA = 64
RWKV_GATE_LORA = 160
RWKV_GN_EPS = 64e-5
RWKV_PASSES = 1
RWKV_STATE_PASSES = 3
MLSTM_HEADS = 4
MLSTM_DH = HALF // MLSTM_HEADS
MLSTM_CONV = 4
CHUNK = 64
SEQ_ROWS = 4
MLSTM_SEQ_ROWS = 1
D_FF = 2816
FFN_CONV = 3
FFN_TILE = 256
LN_EPS = 1e-5
HALO = 8
NEG_BIG = -1e30
V7X_VMEM_LIMIT_BYTES = 56 * 1024 * 1024


def _cparams(*sem):
    return pltpu.CompilerParams(dimension_semantics=sem, vmem_limit_bytes=V7X_VMEM_LIMIT_BYTES)


def _resident(shape):
    nd = len(shape)
    return pl.BlockSpec(shape, lambda *_: (0,) * nd, pipeline_mode=pl.Buffered(1))


def _rows(tm, width):
    return pl.BlockSpec((tm, width), lambda i: (i, 0))


def _halo_rows(tm, width):
    return pl.BlockSpec((HALO, width), lambda i: (jnp.maximum(i * (tm // HALO) - 1, 0), 0))


def _mm(a, b):
    return jnp.dot(a.astype(BF16), b.astype(BF16), preferred_element_type=F32)


def _dg(a, b, ca, cb):
    return lax.dot_general(a, b, (((ca,), (cb,)), ((), ())), preferred_element_type=F32)


def _hi_lo(a):
    hi = a.astype(BF16)
    lo = (a - hi.astype(F32)).astype(BF16)
    return hi, lo


def _mm3(a, b, ca=1, cb=0):
    ah, al = _hi_lo(a)
    bh, bl = _hi_lo(b)
    return _dg(ah, bh, ca, cb) + _dg(ah, bl, ca, cb) + _dg(al, bh, ca, cb)


def _mmp(a, b, ca=1, cb=0, passes=1):
    if passes == 3:
        return _mm3(a, b, ca, cb)
    return _dg(a.astype(BF16), b.astype(BF16), ca, cb)


def _split3(a):
    a1 = a.astype(BF16)
    r1 = a - a1.astype(F32)
    a2 = r1.astype(BF16)
    a3 = (r1 - a2.astype(F32)).astype(BF16)
    return a1, a2, a3


def _mm_exact_rhs(a, b01):
    a1, a2, a3 = _split3(a)
    return (jnp.dot(a1, b01, preferred_element_type=F32) + jnp.dot(a2, b01, preferred_element_type=F32)
            + jnp.dot(a3, b01, preferred_element_type=F32))


def _mm_exact_lhs(a01, b):
    b1, b2, b3 = _split3(b)
    return (jnp.dot(a01, b1, preferred_element_type=F32) + jnp.dot(a01, b2, preferred_element_type=F32)
            + jnp.dot(a01, b3, preferred_element_type=F32))


def _layer_norm(x, g, b):
    mu = jnp.mean(x, -1, keepdims=True)
    d = x - mu
    var = jnp.mean(d * d, -1, keepdims=True)
    return d * lax.rsqrt(var + LN_EPS) * g + b


def _iota2(shape):
    return lax.broadcasted_iota(jnp.int32, shape, 0), lax.broadcasted_iota(jnp.int32, shape, 1)


def _even_pre_kernel(x_ref, wqt_ref, wk_ref, wvt_ref, wf_ref, wu_ref, qt_ref, k_ref, vt_ref, fg_ref, u_ref):
    xb = x_ref[...].astype(BF16)
    qt_ref[...] = (_dg(wqt_ref[...], xb, 1, 1) * (FOX_DH ** -0.5)).astype(BF16)
    vt_ref[...] = _dg(wvt_ref[...], xb, 1, 1).astype(BF16)
    k_ref[...] = jnp.dot(xb, wk_ref[...], preferred_element_type=F32).astype(BF16)
    u = jnp.dot(xb, wu_ref[...], preferred_element_type=F32).astype(BF16)
    for s in range(S5_SLABS):
        u_ref[s] = u[:, 128 * s:128 * (s + 1)]
    fg_ref[...] = jnp.dot(xb, wf_ref[...], preferred_element_type=F32)[:, :FOX_HEADS]


def _even_pre(x, w_in, tm=512):
    T = x.shape[0]
    wb = w_in.astype(BF16)
    wqt, wk, wvt = wb[:, :HALF].T, wb[:, HALF:2 * HALF], wb[:, 2 * HALF:3 * HALF].T
    wf = jnp.pad(wb[:, 3 * HALF:3 * HALF + FOX_HEADS], ((0, 0), (0, 128 - FOX_HEADS)))
    wu = wb[:, 3 * HALF + FOX_HEADS:]
    half_out = jax.ShapeDtypeStruct((T, HALF), BF16)
    half_t = jax.ShapeDtypeStruct((HALF, T), BF16)
    cols = pl.BlockSpec((HALF, tm), lambda i: (0, i))
    return pl.pallas_call(
        _even_pre_kernel,
        grid=(T // tm,),
        in_specs=[_rows(tm, D_MODEL), _resident(wqt.shape), _resident(wk.shape), _resident(wvt.shape),
                  _resident(wf.shape), _resident(wu.shape)],
        out_specs=[cols, _rows(tm, HALF), cols, _rows(tm, FOX_HEADS),
                   pl.BlockSpec((S5_SLABS, tm, 128), lambda i: (0, i, 0))],
        out_shape=[half_t, half_out, half_t, jax.ShapeDtypeStruct((T, FOX_HEADS), F32),
                   jax.ShapeDtypeStruct((S5_SLABS, T, 128), BF16)],
        compiler_params=_cparams("parallel"),
        name="even_pre",
    )(x, wqt, wk, wvt, wf, wu)


def _fox_gate_kernel(fg_ref, fb_ref, o_ref, *, L):
    ls = jax.nn.log_sigmoid(fg_ref[...] + fb_ref[...])
    r, c = _iota2((128, 128))
    tri = (r <= c).astype(BF16)
    carry = jnp.zeros((FOX_HEADS, 1), F32)
    for j in range(L // 128):
        cum = _mm_exact_rhs(ls[:, j * 128:(j + 1) * 128], tri) + carry
        o_ref[:, j * 128:(j + 1) * 128] = cum
        carry = cum[:, 127:128]


def _fox_gate(fg_t, fb):
    B, H, L = fg_t.shape
    return pl.pallas_call(
        functools.partial(_fox_gate_kernel, L=L),
        grid=(B,),
        in_specs=[pl.BlockSpec((None, H, L), lambda b: (b, 0, 0)), _resident((H, 1))],
        out_specs=pl.BlockSpec((None, H, L), lambda b: (b, 0, 0)),
        out_shape=jax.ShapeDtypeStruct((B, H, L), F32),
        compiler_params=_cparams("parallel"),
        name="fox_gate",
    )(fg_t, fb.reshape(H, 1))


def _fox_kernel(qt_ref, k_ref, vt_ref, fc_ref, fr_ref, o_ref, qa_s, ka_s, m_s, l_s, acc_s, *, L):
    TQ = TK = FOX_TILE
    H, DH, KA = FOX_HEADS, FOX_DH, FOX_AUG
    heads = range(H)

    t1, t2, t3 = (t.astype(F32) for t in _split3(fr_ref[...]))
    r16 = lax.broadcasted_iota(jnp.int32, (16, 1), 0)
    upper = lax.broadcasted_iota(jnp.int32, (2 * DH, 1), 0) < DH
    for h in heads:
        p, hh = divmod(h, 2)
        qpair = qt_ref[2 * DH * p:2 * DH * (p + 1), :]
        keep = upper if hh == 0 else jnp.logical_not(upper)
        qa_s[h, 0:2 * DH, :] = jnp.where(keep, qpair, jnp.zeros_like(qpair))
        ones_rows = (r16 >= 3 + 3 * hh) & (r16 < 6 + 3 * hh)
        blk = jnp.where(r16 == 0, t1[h:h + 1], jnp.where(r16 == 1, t2[h:h + 1], jnp.where(
            r16 == 2, t3[h:h + 1], jnp.where(ones_rows, 1.0, 0.0))))
        qa_s[h, 2 * DH:2 * DH + 16, :] = blk.astype(BF16)
        qa_s[h, 2 * DH + 16:KA, :] = jnp.zeros((KA - 2 * DH - 16, L), BF16)
    c1, c2, c3 = _split3(fc_ref[...])
    rs, cs_ = _iota2((128, 128))
    lane = lax.broadcasted_iota(jnp.int32, (1, 128), 1)
    ones3 = jnp.where(lane < 3, 1.0, 0.0)
    for p in range(H // 2):
        def sel(i, p=p):
            hit = ((rs == 2 * p) & (cs_ == 3 + i)) | ((rs == 2 * p + 1) & (cs_ == 6 + i))
            return jnp.where(hit, -1.0, 0.0).astype(BF16)
        aug = (jnp.dot(c1, sel(0), preferred_element_type=F32) + jnp.dot(c2, sel(1), preferred_element_type=F32)
               + jnp.dot(c3, sel(2), preferred_element_type=F32) + ones3)
        ka_s[:, KA * p:KA * p + 2 * DH] = k_ref[:, 2 * DH * p:2 * DH * (p + 1)]
        ka_s[:, KA * p + 2 * DH:KA * (p + 1)] = aug.astype(BF16)

    ri, ci = _iota2((TK, TQ))
    visible = ri <= ci

    def q_block(qi, _):
        q0 = pl.multiple_of(qi * TQ, TQ)
        m_s[...] = jnp.full(m_s.shape, NEG_BIG, F32)
        l_s[...] = jnp.zeros(l_s.shape, F32)
        acc_s[...] = jnp.zeros(acc_s.shape, F32)

        def tile(k0, masked):
            kt = [ka_s[pl.ds(k0, TK), KA * p:KA * (p + 1)] for p in range(H // 2)]
            st = [jnp.dot(kt[h // 2], qa_s[h, :, pl.ds(q0, TQ)], preferred_element_type=F32) for h in heads]
            if masked:
                st = [jnp.where(visible, s, NEG_BIG) for s in st]
            m_old = [m_s[h:h + 1, :] for h in heads]
            m_new = [jnp.maximum(m_old[h], jnp.max(st[h], 0, keepdims=True)) for h in heads]
            pt = [jnp.exp(st[h] - m_new[h]) for h in heads]
            pv = [jnp.dot(vt_ref[DH * h:DH * (h + 1), pl.ds(k0, TK)], pt[h].astype(BF16),
                          preferred_element_type=F32) for h in heads]
            for h in heads:
                a = jnp.exp(m_old[h] - m_new[h])
                m_s[h:h + 1, :] = m_new[h]
                l_s[h:h + 1, :] = a * l_s[h:h + 1, :] + jnp.sum(pt[h], 0, keepdims=True)
                acc_s[DH * h:DH * (h + 1), :] = a * acc_s[DH * h:DH * (h + 1), :] + pv[h]

        def k_step(j, carry):
            tile(pl.multiple_of(j * TK, TK), False)
            return carry

        lax.fori_loop(0, qi, k_step, 0)
        tile(q0, True)
        for p in range(H // 2):
            o_pair = jnp.concatenate([acc_s[DH * h:DH * (h + 1), :] / l_s[h:h + 1, :] for h in (2 * p, 2 * p + 1)], 0)
            o_ref[pl.ds(q0, TQ), 2 * DH * p:2 * DH * (p + 1)] = o_pair.T.astype(BF16)
        return 0

    lax.fori_loop(0, L // TQ, q_block, 0)


def _fox_attention(qt, k, vt, f_col, f_row):
    B, H, L = f_row.shape
    seq = pl.BlockSpec((L, HALF), lambda b: (b, 0))
    seq_t = pl.BlockSpec((HALF, L), lambda b: (0, b))
    return pl.pallas_call(
        functools.partial(_fox_kernel, L=L),
        grid=(B,),
        in_specs=[seq_t, seq, seq_t, pl.BlockSpec((L, 128), lambda b: (b, 0)),
                  pl.BlockSpec((None, H, L), lambda b: (b, 0, 0))],
        out_specs=seq,
        out_shape=jax.ShapeDtypeStruct((B * L, HALF), BF16),
        scratch_shapes=[pltpu.VMEM((H, FOX_AUG, L), BF16), pltpu.VMEM((L, FOX_AUG * H // 2), BF16),
                        pltpu.VMEM((H, FOX_TILE), F32), pltpu.VMEM((H, FOX_TILE), F32),
                        pltpu.VMEM((HALF, FOX_TILE), F32)],
        compiler_params=_cparams("parallel"),
        name="fox_attention",
    )(qt, k, vt, f_col, f_row)


def _s5_matrices(a_re, a_im, b_re, b_im, c_re, c_im, d, log_dt):
    G, P, Cg, LC = S5_GROUPS, S5_STATE, S5_GROUP_CH, S5_CHUNK
    a_re, a_im, b_re, b_im, c_re, c_im = (t.astype(F32) for t in (a_re, a_im, b_re, b_im, c_re, c_im))
    dt = jnp.exp(log_dt.astype(F32))[:, None]
    mag = jnp.exp(a_re * dt)
    lam_re, lam_im = mag * jnp.cos(a_im * dt), mag * jnp.sin(a_im * dt)
    den = a_re ** 2 + a_im ** 2
    zr = ((lam_re - 1.0) * a_re + lam_im * a_im) / den
    zi = (lam_im * a_re - (lam_re - 1.0) * a_im) / den
    bb_re = zr[..., None] * b_re - zi[..., None] * b_im
    bb_im = zr[..., None] * b_im + zi[..., None] * b_re
    n = jnp.arange(LC + 1, dtype=F32)[:, None, None]
    pw_mag = jnp.exp(n * (a_re * dt)[None])
    pr, pi = pw_mag * jnp.cos(n * (a_im * dt)[None]), pw_mag * jnp.sin(n * (a_im * dt)[None])
    ein = functools.partial(jnp.einsum, precision=HIGHEST)
    cr = c_re[None] * pr[:, :, None, :] - c_im[None] * pi[:, :, None, :]
    ci = -(c_re[None] * pi[:, :, None, :] + c_im[None] * pr[:, :, None, :])
    kn = ein('ngcp,gpd->ngcd', cr[:LC], bb_re) + ein('ngcp,gpd->ngcd', ci[:LC], bb_im)
    s_idx = jnp.arange(LC)
    lag = s_idx[None, :] - s_idx[:, None]
    kt = jnp.where((lag >= 0)[:, :, None, None, None],
                   kn[jnp.clip(lag, 0, LC - 1)], 0.0)
    dmat = (lag == 0)[:, :, None, None, None] * (d.astype(F32).reshape(1, 1, G, Cg, 1)
                                                 * jnp.eye(Cg, dtype=F32)[None, None, None])
    NS, GS = S5_SLABS, S5_SLAB_GROUPS
    k6 = (kt + dmat).reshape(LC, LC, NS, GS, Cg, Cg)
    kmat = k6.transpose(2, 0, 3, 5, 1, 4).reshape(NS, LC * 128, LC * Cg)
    rr, ri = pr[LC - 1 - s_idx], pi[LC - 1 - s_idx]
    e_re = rr[..., None] * bb_re[None] - ri[..., None] * bb_im[None]
    e_im = rr[..., None] * bb_im[None] + ri[..., None] * bb_re[None]
    e6 = jnp.stack([e_re, e_im]).reshape(2, LC, NS, GS, P, Cg)
    emat = e6.transpose(2, 1, 3, 5, 0, 4).reshape(NS, LC * 128, 2 * P)
    f6 = jnp.stack([cr[1:], ci[1:]]).reshape(2, LC, NS, GS, Cg, P)
    fmat = f6.transpose(2, 0, 3, 5, 1, 4).reshape(NS, 2 * GS * P, LC * Cg)
    lr, li = pr[LC].reshape(NS, 1, GS * P), pi[LC].reshape(NS, 1, GS * P)
    lam_a = jnp.concatenate([lr, lr], axis=-1)
    lam_b = jnp.concatenate([-li, li], axis=-1)
    return kmat.astype(BF16), emat.astype(BF16), fmat.astype(BF16), lam_a, lam_b


def _s5_expand(src_ref, dst_ref, col_of, row_group, col_group):
    n_src, n_dst = src_ref.shape[1], dst_ref.shape[1]
    sr, sc = _iota2((n_src, n_dst))
    select = jnp.where(sr == col_of(sc), 1.0, 0.0).astype(BF16)
    step = 256
    for i in range(src_ref.shape[0] // step):
        rs = slice(i * step, (i + 1) * step)
        r, c = _iota2((step, n_dst))
        same = row_group(r + i * step) == col_group(c)
        full = jnp.dot(src_ref[rs, :], select, preferred_element_type=F32)
        dst_ref[rs, :] = jnp.where(same, full, 0.0).astype(BF16)


def _s5_kernel(u_ref, kc_ref, ec_ref, fc_ref, la_ref, lb_ref, y_ref, k_ref, e_ref, f_ref, e_s, hs_s, h_s, *, B):
    @pl.when(pl.program_id(1) == 0)
    def _():
        h_s[...] = jnp.zeros_like(h_s)
        gs, cg, p = S5_SLAB_GROUPS, S5_GROUP_CH, S5_STATE
        lg = lambda n: n.bit_length() - 1
        lane_group = lambda i: (i >> lg(cg)) & (gs - 1)
        state_group = lambda i: (i >> lg(p)) & (gs - 1)
        frame_ch = lambda c: ((c >> lg(gs * cg)) << lg(cg)) | (c & (cg - 1))
        part_state = lambda c: ((c >> lg(gs * p)) << lg(p)) | (c & (p - 1))
        _s5_expand(kc_ref, k_ref, frame_ch, lane_group, lane_group)
        _s5_expand(ec_ref, e_ref, part_state, lane_group, state_group)
        _s5_expand(fc_ref, f_ref, frame_ch, state_group, lane_group)

    u = u_ref[...]
    e_s[...] = jnp.dot(u, e_ref[...], preferred_element_type=F32)
    la, lb = la_ref[...], lb_ref[...]
    half = h_s.shape[1] // 2
    h = h_s[...]
    for kc in range(u.shape[0] // B):
        rs = slice(kc * B, (kc + 1) * B)
        hs_s[rs, :] = h
        swapped = jnp.concatenate([h[:, half:], h[:, :half]], axis=1)
        h = la * h + lb * swapped + e_s[rs, :]
    h_s[...] = h
    y_ref[...] = (jnp.dot(u, k_ref[...], preferred_element_type=F32)
                  + jnp.dot(hs_s[...].astype(BF16), f_ref[...], preferred_element_type=F32)).astype(BF16)


def _s5(u4, mats, B, L):
    kmat, emat, fmat, lam_a, lam_b = mats
    NS, LC = S5_SLABS, S5_CHUNK
    NK = L // LC
    R, W, S2 = NK * B, LC * 128, 2 * S5_SLAB_GROUPS * S5_STATE
    ug = u4.reshape(NS, B, NK, W).transpose(0, 2, 1, 3).reshape(NS, R, W)
    TR = B * min(NK, S5_TILE_CHUNKS)
    rows = pl.BlockSpec((None, TR, W), lambda s, r: (s, r, 0))
    per_s = lambda a: pl.BlockSpec((None,) + a.shape[1:], lambda s, r: (s, 0, 0), pipeline_mode=pl.Buffered(1))
    y = pl.pallas_call(
        functools.partial(_s5_kernel, B=B),
        grid=(NS, R // TR),
        in_specs=[rows] + [per_s(a) for a in (kmat, emat, fmat, lam_a, lam_b)],
        out_specs=rows,
        out_shape=jax.ShapeDtypeStruct((NS, R, W), BF16),
        scratch_shapes=[pltpu.VMEM((W, W), BF16), pltpu.VMEM((W, S2), BF16), pltpu.VMEM((S2, W), BF16),
                        pltpu.VMEM((TR, S2), F32), pltpu.VMEM((TR, S2), F32), pltpu.VMEM((B, S2), F32)],
        compiler_params=_cparams("parallel", "arbitrary"),
        name="s5",
    )(ug, kmat, emat, fmat, lam_a, lam_b)
    return y.reshape(NS, NK, B, W).transpose(0, 2, 1, 3).reshape(NS, B * L, 128)


def _mix_kernel(a_ref, b_ref, x_ref, wt_ref, wb_ref, g_ref, beta_ref, *rest, even, alpha):
    if even:
        wg_ref, bg_ref, o_ref = rest
        z = jax.nn.gelu(jnp.concatenate([b_ref[s] for s in range(S5_SLABS)], axis=1).astype(F32))
        second = z * jax.nn.sigmoid(_mm(z, wg_ref[...]) + bg_ref[...])
    else:
        (o_ref,) = rest
        second = b_ref[...]
    mix = _mm(a_ref[...], wt_ref[...]) + _mm(second, wb_ref[...])
    o_ref[...] = _layer_norm(alpha * x_ref[...] + mix, g_ref[...], beta_ref[...])


def _mix_out(a, b, x, w_out, ln_g, ln_b, alpha, glu=None, tm=512):
    T = x.shape[0]
    wb16 = w_out.astype(BF16)
    args = [a, b, x, wb16[:HALF], wb16[HALF:], ln_g.reshape(1, D_MODEL), ln_b.reshape(1, D_MODEL)]
    specs = [_rows(tm, HALF), _rows(tm, HALF), _rows(tm, D_MODEL), _resident((HALF, D_MODEL)),
             _resident((HALF, D_MODEL)), _resident((1, D_MODEL)), _resident((1, D_MODEL))]
    if glu is not None:
        w_glu, b_glu = glu
        specs[1] = pl.BlockSpec((S5_SLABS, tm, 128), lambda i: (0, i, 0))
        args += [w_glu.astype(BF16), b_glu.reshape(1, HALF).astype(F32)]
        specs += [_resident((HALF, HALF)), _resident((1, HALF))]
    return pl.pallas_call(
        functools.partial(_mix_kernel, even=glu is not None, alpha=alpha),
        grid=(T // tm,),
        in_specs=specs,
        out_specs=_rows(tm, D_MODEL),
        out_shape=jax.ShapeDtypeStruct((T, D_MODEL), F32),
        compiler_params=_cparams("parallel"),
        name="mix_out",
    )(*args)


def _ffn_kernel(x_ref, xh_ref, wu_ref, cw_ref, cb_ref, wd_ref, g_ref, beta_ref, o_ref, *, tiles_per_seq, alpha):
    tm = x_ref.shape[0]
    first = (pl.program_id(0) % tiles_per_seq) == 0
    x = x_ref[...]
    xh = jnp.where(first, 0.0, xh_ref[...])
    xe = jnp.concatenate([xh, x], axis=0).astype(BF16)
    xb = xe[HALO:]
    def up(c):
        ue = jnp.dot(xe, wu_ref[:, c * FFN_TILE:(c + 1) * FFN_TILE], preferred_element_type=F32)
        gate = jnp.dot(xb, wu_ref[:, D_FF + c * FFN_TILE:D_FF + (c + 1) * FFN_TILE], preferred_element_type=F32)
        return ue, gate

    acc = jnp.zeros((tm, D_MODEL), F32)
    n_tiles = D_FF // FFN_TILE
    nxt = up(0)
    for c in range(n_tiles):
        cs = slice(c * FFN_TILE, (c + 1) * FFN_TILE)
        ue, gate = nxt
        if c + 1 < n_tiles:
            nxt = up(c + 1)
        cw = cw_ref[:, cs]
        u = (cw[2:3] * ue[HALO:] + cw[1:2] * ue[HALO - 1:HALO - 1 + tm] + cw[0:1] * ue[HALO - 2:HALO - 2 + tm]
             + cb_ref[:, cs])
        acc = acc + _mm(jax.nn.gelu(u) * gate, wd_ref[cs, :])
    o_ref[...] = _layer_norm(alpha * x + acc, g_ref[...], beta_ref[...])


def _conv_ffn(x, L, w_up, conv_w, conv_b, w_down, ln_g, ln_b, alpha, tm=1024):
    T = x.shape[0]
    tm = min(tm, L)
    return pl.pallas_call(
        functools.partial(_ffn_kernel, tiles_per_seq=L // tm, alpha=alpha),
        grid=(T // tm,),
        in_specs=[_rows(tm, D_MODEL), _halo_rows(tm, D_MODEL), _resident((D_MODEL, 2 * D_FF)),
                  _resident((FFN_CONV, D_FF)), _resident((1, D_FF)), _resident((D_FF, D_MODEL)),
                  _resident((1, D_MODEL)), _resident((1, D_MODEL))],
        out_specs=_rows(tm, D_MODEL),
        out_shape=jax.ShapeDtypeStruct((T, D_MODEL), F32),
        compiler_params=_cparams("parallel"),
        name="conv_ffn",
    )(x, x, w_up.astype(BF16), conv_w.astype(F32), conv_b.reshape(1, D_FF).astype(F32),
      w_down.astype(BF16), ln_g.reshape(1, D_MODEL), ln_b.reshape(1, D_MODEL))


_LORA_PAD = (128, 128, 256)
_RWKV_PAD = 3 * HALF + sum(_LORA_PAD)


def _odd_pre_kernel(x_ref, xh_ref, wrk_ref, mu_ref, w2_ref, a2_ref, g2_ref, vec_ref, bd_ref,
                    wqk_ref, cw_ref, cb_ref, wv_ref, wz_ref, wif_ref, gb_ref,
                    r_ref, k_ref, v_ref, lw_ref, kk_ref, ka_ref, g_ref, bo_ref,
                    mq_ref, mk_ref, mv_ref, mz_ref, gate_ref, *, tiles_per_seq):
    tm = x_ref.shape[0]
    first = (pl.program_id(0) % tiles_per_seq) == 0
    xh = jnp.where(first, 0.0, xh_ref[...])
    xe = jnp.concatenate([xh, x_ref[...]], axis=0).astype(BF16)
    xb = xe[HALO:]

    pe = jnp.dot(xe, wrk_ref[...], preferred_element_type=F32)
    qke = jnp.dot(xe, wqk_ref[...], preferred_element_type=F32)
    mv_ref[...] = jnp.dot(xb, wv_ref[...], preferred_element_type=F32).astype(BF16)
    mz_ref[...] = jnp.dot(xb, wz_ref[...], preferred_element_type=F32).astype(mz_ref.dtype)
    pre = jnp.dot(xb, wif_ref[...], preferred_element_type=F32)[:, :2 * MLSTM_HEADS] + gb_ref[...]

    cur, prev = pe[HALO:], pe[HALO - 1:HALO - 1 + tm]
    p = cur + (prev - cur) * mu_ref[...]
    r, k, v = p[:, :HALF], p[:, HALF:2 * HALF], p[:, 2 * HALF:3 * HALF]
    o0 = 3 * HALF
    wd = p[:, o0:o0 + _LORA_PAD[0]]
    ad = p[:, o0 + _LORA_PAD[0]:o0 + _LORA_PAD[0] + _LORA_PAD[1]]
    gd = p[:, o0 + _LORA_PAD[0] + _LORA_PAD[1]:]
    w0, a0, k_k, k_a, r_k = (vec_ref[i:i + 1, :] for i in range(5))
    wlog = -jax.nn.softplus(-(w0 + _mm(jnp.tanh(wd), w2_ref[...]))) - 0.5
    lw_ref[...] = -jnp.exp(wlog)
    a = jax.nn.sigmoid(a0 + _mm(ad, a2_ref[...]))
    g_ref[...] = _mm(jax.nn.sigmoid(gd), g2_ref[...]).astype(g_ref.dtype)
    kk = k * k_k
    ss = _mm_exact_rhs(kk * kk, bd_ref[...])
    kk = kk * lax.rsqrt(jnp.maximum(ss, 1e-24))
    kmod = k * (1.0 + (a - 1.0) * k_a)
    bo_ref[...] = (_mm_exact_rhs(r * kmod * r_k, bd_ref[...]) * v).astype(bo_ref.dtype)
    r_ref[...] = r.astype(r_ref.dtype)
    k_ref[...] = kmod.astype(k_ref.dtype)
    v_ref[...] = v.astype(v_ref.dtype)
    kk_ref[...] = kk.astype(kk_ref.dtype)
    ka_ref[...] = (kk * a).astype(ka_ref.dtype)

    cw = cw_ref[...]
    qk = cb_ref[...]
    for j in range(MLSTM_CONV):
        off = HALO - (MLSTM_CONV - 1) + j
        qk = qk + cw[j:j + 1] * qke[off:off + tm]
    qk = jax.nn.silu(qk)
    mq_ref[...] = qk[:, :HALF].astype(mq_ref.dtype)
    mk_ref[...] = (qk[:, HALF:] * (MLSTM_DH ** -0.5)).astype(BF16)
    is_i = lax.broadcasted_iota(jnp.int32, pre.shape, 1) < MLSTM_HEADS
    gate_ref[...] = jnp.where(is_i, pre, jax.nn.log_sigmoid(pre))


def _head_block_ones(width, head):
    idx = jnp.arange(width) // head
    return (idx[:, None] == idx[None, :]).astype(BF16)


def _odd_pre(x, L, w_in, mu, w0, w2, a0, a2, g2, k_k, k_a, r_k, conv_w, conv_b, ib, fb, tm=512):
    T = x.shape[0]
    tm = min(tm, L)
    wb = w_in.astype(BF16)
    o = 3 * HALF
    sizes = (RWKV_DECAY_LORA, RWKV_AAA_LORA, RWKV_GATE_LORA)

    def pad_lora(m, axis):
        parts, s = [], o
        for sz, pd in zip(sizes, _LORA_PAD):
            piece = lax.slice_in_dim(m, s, s + sz, axis=axis)
            widths = [(0, 0)] * m.ndim
            widths[axis] = (0, pd - sz)
            parts.append(jnp.pad(piece, widths))
            s += sz
        return jnp.concatenate([lax.slice_in_dim(m, 0, o, axis=axis)] + parts, axis=axis)

    rwkv_proj = o + sum(sizes)
    wrk = pad_lora(wb[:, :rwkv_proj], 1)
    mu_p = pad_lora(mu.astype(F32).reshape(1, -1), 1)
    padr = lambda m, rows: jnp.pad(m.astype(BF16), ((0, rows - m.shape[0]), (0, 0)))
    w2p, a2p, g2p = padr(w2, _LORA_PAD[0]), padr(a2, _LORA_PAD[1]), padr(g2, _LORA_PAD[2])
    vecs = jnp.stack([w0, a0, k_k, k_a, r_k.reshape(HALF)]).astype(F32)
    vecs = jnp.pad(vecs, ((0, 8 - vecs.shape[0]), (0, 0)))
    bd = _head_block_ones(HALF, RWKV_N)
    wm = wb[:, rwkv_proj:]
    wqk, wmv, wmz = wm[:, :2 * HALF], wm[:, 2 * HALF:3 * HALF], wm[:, 3 * HALF:4 * HALF]
    wif = jnp.pad(wm[:, 4 * HALF:], ((0, 0), (0, 128 - 2 * MLSTM_HEADS)))
    gbias = jnp.concatenate([ib, fb]).astype(F32).reshape(1, 2 * MLSTM_HEADS)
    args = [x, x, wrk, mu_p, w2p, a2p, g2p, vecs, bd, wqk, conv_w.astype(F32),
            conv_b.astype(F32).reshape(1, 2 * HALF), wmv, wmz, wif, gbias]
    specs = [_rows(tm, D_MODEL), _halo_rows(tm, D_MODEL)] + [_resident(a.shape) for a in args[2:]]
    f_half = jax.ShapeDtypeStruct((T, HALF), F32)
    b_half = jax.ShapeDtypeStruct((T, HALF), BF16)
    out_shape = [b_half] * 3 + [f_half] + [b_half] * 8 + [jax.ShapeDtypeStruct((T, 2 * MLSTM_HEADS), F32)]
    out_specs = [_rows(tm, HALF)] * 12 + [_rows(tm, 2 * MLSTM_HEADS)]
    return pl.pallas_call(
        functools.partial(_odd_pre_kernel, tiles_per_seq=L // tm),
        grid=(T // tm,),
        in_specs=specs,
        out_specs=out_specs,
        out_shape=out_shape,
        compiler_params=_cparams("parallel"),
        name="odd_pre",
    )(*args)


def _rwkv_kernel(r_ref, k_ref, v_ref, lw_ref, kk_ref, ka_ref, g_ref, bo_ref, lng_ref, lnb_ref, o_ref, m_s):
    C, N, H = CHUNK, RWKV_N, RWKV_HEADS
    rows = range(r_ref.shape[0])
    units = [(b, h) for b in rows for h in range(H)]

    @pl.when(pl.program_id(1) == 0)
    def _():
        m_s[...] = jnp.zeros_like(m_s)

    ri, ci = _iota2((C, C))
    incl, strict, eye = ci <= ri, ci < ri, ci == ri
    ri2, ci2 = _iota2((C, 2 * C))
    incl2 = (ci2 & (C - 1)) <= ri2
    zeros = jnp.zeros((C, N), F32)
    mm = functools.partial(_mmp, passes=RWKV_PASSES)
    al, rt, bt, kt, bp, kp, gam, v = ({} for _ in range(8))
    for b in rows:
        lw = lw_ref[b]
        cs = _mm_exact_lhs(incl.astype(BF16), lw)
        cend = cs[C - 1:C, :]
        e_neg = jnp.exp(-cs)
        e_rem = jnp.exp(cend - cs)
        r, k, vv, kk, ka = (t[b].astype(F32) for t in (r_ref, k_ref, v_ref, kk_ref, ka_ref))
        al_b = -kk * jnp.exp(cs - lw)
        rt_b = r * jnp.exp(cs)
        gam_b = jnp.exp(cend)
        bt_b, kt_b, bp_b, kp_b = ka * e_neg, k * e_neg, ka * e_rem, k * e_rem
        for h in range(H):
            s = slice(N * h, N * (h + 1))
            u = (b, h)
            al[u], rt[u], v[u], gam[u] = al_b[:, s], rt_b[:, s], vv[:, s], gam_b[:, s]
            bt[u], kt[u], bp[u], kp[u] = bt_b[:, s], kt_b[:, s], bp_b[:, s], kp_b[:, s]
    pm = {u: mm(jnp.concatenate([al[u], rt[u]], 0), jnp.concatenate([bt[u], kt[u]], 0), 1, 1) for u in units}
    a_ab = {u: jnp.where(strict, pm[u][:C, :C], 0.0) for u in units}
    a_ak = {u: jnp.where(strict, pm[u][:C, C:], 0.0) for u in units}
    a_r = {u: jnp.where(incl2, pm[u][C:, :], 0.0) for u in units}
    w = {u: jnp.concatenate([al[u], mm(a_ak[u], v[u])], 1) for u in units}
    npow = a_ab
    levels = int(math.log2(C))
    for lvl in range(levels):
        if lvl < levels - 1:
            y = {u: mm(npow[u], jnp.concatenate([w[u], npow[u]], 1)) for u in units}
            w = {u: w[u] + y[u][:, :2 * N] for u in units}
            npow = {u: y[u][:, 2 * N:] for u in units}
        else:
            w = {u: w[u] + mm(npow[u], w[u]) for u in units}
    zv = {u: jnp.concatenate([zeros, v[u]], 1) for u in units}
    qt = {u: mm(a_r[u], jnp.concatenate([w[u], zv[u]], 0)) for u in units}
    qb = {u: mm(bp[u], w[u], 0, 0) + mm(kp[u], zv[u], 0, 0) for u in units}
    m0 = {u: m_s[u[0], u[1]] for u in units}
    o_h = {u: mm(rt[u] + qt[u][:, :N], m0[u]) + qt[u][:, N:] for u in units}
    m_new = {u: _mmp(qb[u][:, :N] + jnp.where(eye, gam[u], 0.0), m0[u], passes=RWKV_STATE_PASSES) + qb[u][:, N:]
             for u in units}
    for b in rows:
        outs = []
        for h in range(H):
            u = (b, h)
            m_s[b, h] = m_new[u]
            om = jnp.mean(o_h[u], -1, keepdims=True)
            d = o_h[u] - om
            ov = jnp.mean(d * d, -1, keepdims=True)
            outs.append(d * lax.rsqrt(ov + RWKV_GN_EPS))
        on = jnp.concatenate(outs, 1) * lng_ref[...] + lnb_ref[...]
        o_ref[b] = ((on + bo_ref[b].astype(F32)) * g_ref[b].astype(F32)).astype(BF16)


def _rwkv(r, k, v, lw, kk, ka, g, bonus, ln_g, ln_b, B, L):
    NC = L // CHUNK
    RB = SEQ_ROWS if B % SEQ_ROWS == 0 else 1
    seq3 = lambda t: t.reshape(B, L, HALF)
    blk = pl.BlockSpec((RB, CHUNK, HALF), lambda b, c: (b, c, 0))
    vec = pl.BlockSpec((1, HALF), lambda b, c: (0, 0))
    out = pl.pallas_call(
        _rwkv_kernel,
        grid=(B // RB, NC),
        in_specs=[blk] * 8 + [vec, vec],
        out_specs=blk,
        out_shape=jax.ShapeDtypeStruct((B, L, HALF), BF16),
        scratch_shapes=[pltpu.VMEM((RB, RWKV_HEADS, RWKV_N, RWKV_N), F32)],
        compiler_params=_cparams("parallel", "arbitrary"),
        name="rwkv7",
    )(*(seq3(t) for t in (r, k, v, lw, kk, ka, g, bonus)),
      ln_g.reshape(1, HALF).astype(F32), ln_b.reshape(1, HALF).astype(F32))
    return out.reshape(B * L, HALF)


def _mlstm_kernel(q_ref, k_ref, v_ref, z_ref, gc_ref, gr_ref, lng_ref, skip_ref, o_ref, c_s, m_s):
    C, H, DH = CHUNK, MLSTM_HEADS, MLSTM_DH
    rows = range(q_ref.shape[0])
    units = [(b, h) for b in rows for h in range(H)]

    @pl.when(pl.program_id(1) == 0)
    def _():
        c_s[...] = jnp.zeros_like(c_s)
        m_s[...] = jnp.zeros_like(m_s)

    ri, ci = _iota2((C, C))
    incl = ci <= ri
    lower, upper = incl.astype(BF16), (ri <= ci).astype(BF16)
    lane = lax.broadcasted_iota(jnp.int32, (1, DH), 1)
    ones_blk = jnp.broadcast_to(jnp.where(lane == 0, 1.0, 0.0).astype(BF16), (C, DH))
    q_all, qb, kh, vaug, i_col, i_row, b_col, b_row, m_prev = ({} for _ in range(9))
    for b in rows:
        gc, gr = gc_ref[b], gr_ref[b]
        b_cols = _mm_exact_lhs(lower, gc)
        b_rows = _mm_exact_rhs(gr, upper)
        q_all[b] = q_ref[b].astype(F32)
        for h in range(H):
            u, s = (b, h), slice(DH * h, DH * (h + 1))
            qb[u], kh[u] = q_ref[b, :, s], k_ref[b, :, s]
            vaug[u] = jnp.concatenate([v_ref[b, :, s], ones_blk], 1)
            i_col[u], i_row[u] = gc[:, h:h + 1], gr[h:h + 1, :]
            b_col[u], b_row[u] = b_cols[:, H + h:H + h + 1], b_rows[H + h:H + h + 1, :]
            m_prev[u] = m_s[b, h:h + 1, 0:1]
    c_old = {u: c_s[u[0], u[1]] for u in units}
    qk = {u: _dg(qb[u], kh[u], 1, 1) for u in units}
    qc = {u: jnp.dot(qb[u], c_old[u].astype(BF16), preferred_element_type=F32) for u in units}
    dmat = {u: jnp.where(incl, b_col[u] - b_row[u] + i_row[u], NEG_BIG) for u in units}
    inter = {u: b_col[u] + m_prev[u] for u in units}
    m_t = {u: jnp.maximum(inter[u], jnp.max(dmat[u], -1, keepdims=True)) for u in units}
    s = {u: (qk[u] * jnp.exp(dmat[u] - m_t[u])).astype(BF16) for u in units}
    sv = {u: jnp.dot(s[u], vaug[u], preferred_element_type=F32) for u in units}
    b_end = {u: b_col[u][C - 1:C, :] for u in units}
    m_new = {u: jnp.maximum(b_end[u] + m_prev[u], jnp.max(b_end[u] - b_row[u] + i_row[u], -1, keepdims=True))
             for u in units}
    kw = {u: (jnp.exp(b_end[u] - b_col[u] + i_col[u] - m_new[u]) * kh[u].astype(F32)).astype(BF16) for u in units}
    kv = {u: _dg(kw[u], vaug[u], 0, 0) for u in units}
    for b in rows:
        outs = []
        for h in range(H):
            u = (b, h)
            c_s[b, h] = jnp.exp(b_end[u] + m_prev[u] - m_new[u]) * c_old[u] + kv[u]
            m_s[b, h:h + 1, :] = jnp.broadcast_to(m_new[u], (1, 128))
            nd = jnp.exp(inter[u] - m_t[u]) * qc[u] + sv[u]
            hid = nd[:, :DH] / jnp.maximum(jnp.abs(nd[:, DH:DH + 1]), jnp.exp(-m_t[u]))
            hm = jnp.mean(hid, -1, keepdims=True)
            d = hid - hm
            hv = jnp.mean(d * d, -1, keepdims=True)
            outs.append(d * lax.rsqrt(hv + LN_EPS))
        hn = jnp.concatenate(outs, 1) * lng_ref[...]
        o_ref[b] = ((hn + skip_ref[...] * q_all[b]) * jax.nn.silu(z_ref[b].astype(F32))).astype(BF16)


def _mlstm(q, k, v, z, gates, ln_g, skip, B, L):
    NC, H2 = L // CHUNK, 2 * MLSTM_HEADS
    RB = MLSTM_SEQ_ROWS if B % MLSTM_SEQ_ROWS == 0 else 1
    seq3 = lambda t: t.reshape(B, L, t.shape[-1])
    blk = pl.BlockSpec((RB, CHUNK, HALF), lambda b, c: (b, c, 0))
    vec = pl.BlockSpec((1, HALF), lambda b, c: (0, 0))
    g_rows = gates.reshape(B, NC, CHUNK, H2).transpose(0, 1, 3, 2)
    out = pl.pallas_call(
        _mlstm_kernel,
        grid=(B // RB, NC),
        in_specs=[blk, blk, blk, blk, pl.BlockSpec((RB, CHUNK, H2), lambda b, c: (b, c, 0)),
                  pl.BlockSpec((RB, None, H2, CHUNK), lambda b, c: (b, c, 0, 0)), vec, vec],
        out_specs=blk,
        out_shape=jax.ShapeDtypeStruct((B, L, HALF), BF16),
        scratch_shapes=[pltpu.VMEM((RB, MLSTM_HEADS, MLSTM_DH, 2 * MLSTM_DH), F32),
                        pltpu.VMEM((RB, 8, 128), F32)],
        compiler_params=_cparams("parallel", "arbitrary"),
        name="mlstm",
    )(seq3(q), seq3(k), seq3(v), seq3(z), seq3(gates), g_rows,
      ln_g.reshape(1, HALF).astype(F32), skip.reshape(1, HALF).astype(F32))
    return out.reshape(B * L, HALF)


def kernel(x, ev_w_in, ev_fox_fb, ev_s5_a_re, ev_s5_a_im, ev_s5_b_re, ev_s5_b_im, ev_s5_c_re, ev_s5_c_im, ev_s5_d, ev_s5_log_dt, ev_s5_w_glu, ev_s5_b_glu, ev_w_out, od_w_in, od_rwkv_mu, od_rwkv_w0, od_rwkv_w2, od_rwkv_a0, od_rwkv_a2, od_rwkv_g2, od_rwkv_k_k, od_rwkv_k_a, od_rwkv_r_k, od_rwkv_ln_g, od_rwkv_ln_b, od_mlstm_conv_w, od_mlstm_conv_b, od_mlstm_ib, od_mlstm_fb, od_mlstm_ln_g, od_mlstm_skip, od_w_out, ln1_g, ln1_b, ffn_w_up, ffn_conv_w, ffn_conv_b, ffn_w_down, ln2_g, ln2_b):
    B, L, _ = x.shape
    depth = ln1_g.shape[0]
    alpha = float((2 * depth) ** 0.25)
    h = x.reshape(B * L, D_MODEL).astype(F32)
    for layer in range(depth):
        i = layer // 2
        if layer % 2 == 0:
            qt, k, vt, fg, u = _even_pre(h, ev_w_in[i])
            f_row = _fox_gate(fg.reshape(B, L, FOX_HEADS).transpose(0, 2, 1), ev_fox_fb[i].astype(F32))
            f_col = jnp.pad(f_row.transpose(0, 2, 1).reshape(B * L, FOX_HEADS), ((0, 0), (0, 128 - FOX_HEADS)))
            fox = _fox_attention(qt, k, vt, f_col, f_row)
            mats = _s5_matrices(ev_s5_a_re[i], ev_s5_a_im[i], ev_s5_b_re[i], ev_s5_b_im[i],
                                ev_s5_c_re[i], ev_s5_c_im[i], ev_s5_d[i], ev_s5_log_dt[i])
            y = _s5(u, mats, B, L)
            h = _mix_out(fox, y, h, ev_w_out[i], ln1_g[layer], ln1_b[layer], alpha,
                         glu=(ev_s5_w_glu[i], ev_s5_b_glu[i]))
        else:
            (r, k, v, lw, kk, ka, g, bonus, mq, mk, mv, mz, gates) = _odd_pre(
                h, L, od_w_in[i], od_rwkv_mu[i], od_rwkv_w0[i], od_rwkv_w2[i], od_rwkv_a0[i], od_rwkv_a2[i],
                od_rwkv_g2[i], od_rwkv_k_k[i], od_rwkv_k_a[i], od_rwkv_r_k[i], od_mlstm_conv_w[i],
                od_mlstm_conv_b[i], od_mlstm_ib[i], od_mlstm_fb[i])
            c = _rwkv(r, k, v, lw, kk, ka, g, bonus, od_rwkv_ln_g[i], od_rwkv_ln_b[i], B, L)
            dm = _mlstm(mq, mk, mv, mz, gates, od_mlstm_ln_g[i], od_mlstm_skip[i], B, L)
            h = _mix_out(c, dm, h, od_w_out[i], ln1_g[layer], ln1_b[layer], alpha)
        h = _conv_ffn(h, L, ffn_w_up[layer], ffn_conv_w[layer], ffn_conv_b[layer], ffn_w_down[layer],
                      ln2_g[layer], ln2_b[layer], alpha)
    return h.reshape(B, L, D_MODEL).astype(x.dtype)
```

```python
import functools
import math

import jax
import jax.numpy as jnp
from jax import lax
from jax.experimental import pallas as pl
from jax.experimental.pallas import tpu as pltpu

F32 = jnp.float32
BF16 = jnp.bfloat16
HIGHEST = lax.Precision.HIGHEST

D_MODEL = 1024
HALF = D_MODEL // 2
FOX_HEADS = 8
FOX_DH = HALF // FOX_HEADS
FOX_TILE = 256
FOX_AUG = 256
S5_GROUPS = 32
S5_GROUP_CH = 16
S5_STATE = 64
S5_CHUNK = 16
S5_SLAB_GROUPS = 128 // S5_GROUP_CH
S5_SLABS = S5_GROUPS // S5_SLAB_GROUPS
S5_TILE_CHUNKS = 16
RWKV_HEADS = 8
RWKV_N = HALF // RWKV_HEADS
RWKV_DECAY_LORA = 64
RWKV_AAA_LORA = 64
RWKV_GATE_LORA = 160
RWKV_GN_EPS = 64e-5
RWKV_PASSES = 1
RWKV_STATE_PASSES = 3
MLSTM_HEADS = 4
MLSTM_DH = HALF // MLSTM_HEADS
MLSTM_CONV = 4
CHUNK = 64
SEQ_ROWS = 4
MLSTM_SEQ_ROWS = 1
D_FF = 2816
FFN_CONV = 3
FFN_TILE = 256
LN_EPS = 1e-5
HALO = 16
NEG_BIG = -1e30
V7X_VMEM_LIMIT_BYTES = 56 * 1024 * 1024


def _cparams(*sem):
    return pltpu.CompilerParams(dimension_semantics=sem, vmem_limit_bytes=V7X_VMEM_LIMIT_BYTES)


def _resident(shape):
    nd = len(shape)
    return pl.BlockSpec(shape, lambda *_: (0,) * nd, pipeline_mode=pl.Buffered(1))


def _rows(tm, width):
    return pl.BlockSpec((tm, width), lambda i: (i, 0))


def _halo_rows(tm, width):
    return pl.BlockSpec((HALO, width), lambda i: (jnp.maximum(i * (tm // HALO) - 1, 0), 0))


def _mm(a, b):
    return jnp.dot(a.astype(BF16), b.astype(BF16), preferred_element_type=F32)


def _dg(a, b, ca, cb):
    return lax.dot_general(a, b, (((ca,), (cb,)), ((), ())), preferred_element_type=F32)


def _hi_lo(a):
    hi = a.astype(BF16)
    lo = (a - hi.astype(F32)).astype(BF16)
    return hi, lo


def _mm3(a, b, ca=1, cb=0):
    ah, al = _hi_lo(a)
    bh, bl = _hi_lo(b)
    return _dg(ah, bh, ca, cb) + _dg(ah, bl, ca, cb) + _dg(al, bh, ca, cb)


def _mmp(a, b, ca=1, cb=0, passes=1):
    if passes == 3:
        return _mm3(a, b, ca, cb)
    return _dg(a.astype(BF16), b.astype(BF16), ca, cb)


def _split3(a):
    a1 = a.astype(BF16)
    r1 = a - a1.astype(F32)
    a2 = r1.astype(BF16)
    a3 = (r1 - a2.astype(F32)).astype(BF16)
    return a1, a2, a3


def _mm_exact_rhs(a, b01):
    a1, a2, a3 = _split3(a)
    return (jnp.dot(a1, b01, preferred_element_type=F32) + jnp.dot(a2, b01, preferred_element_type=F32)
            + jnp.dot(a3, b01, preferred_element_type=F32))


def _mm_exact_lhs(a01, b):
    b1, b2, b3 = _split3(b)
    return (jnp.dot(a01, b1, preferred_element_type=F32) + jnp.dot(a01, b2, preferred_element_type=F32)
            + jnp.dot(a01, b3, preferred_element_type=F32))


def _layer_norm(x, g, b):
    mu = jnp.mean(x, -1, keepdims=True)
    d = x - mu
    var = jnp.mean(d * d, -1, keepdims=True)
    return d * lax.rsqrt(var + LN_EPS) * g + b


def _iota2(shape):
    return lax.broadcasted_iota(jnp.int32, shape, 0), lax.broadcasted_iota(jnp.int32, shape, 1)


def _even_pre_kernel(x_ref, wqt_ref, wk_ref, wvt_ref, wf_ref, wu_ref, qt_ref, k_ref, vt_ref, fg_ref, u_ref):
    xb = x_ref[...].astype(BF16)
    qt_ref[...] = (_dg(wqt_ref[...], xb, 1, 1) * (FOX_DH ** -0.5)).astype(BF16)
    vt_ref[...] = _dg(wvt_ref[...], xb, 1, 1).astype(BF16)
    k_ref[...] = jnp.dot(xb, wk_ref[...], preferred_element_type=F32).astype(BF16)
    u = jnp.dot(xb, wu_ref[...], preferred_element_type=F32).astype(BF16)
    for s in range(S5_SLABS):
        u_ref[s] = u[:, 128 * s:128 * (s + 1)]
    fg_ref[...] = jnp.dot(xb, wf_ref[...], preferred_element_type=F32)[:, :FOX_HEADS]


def _even_pre(x, w_in, tm=512):
    T = x.shape[0]
    wb = w_in.astype(BF16)
    wqt, wk, wvt = wb[:, :HALF].T, wb[:, HALF:2 * HALF], wb[:, 2 * HALF:3 * HALF].T
    wf = jnp.pad(wb[:, 3 * HALF:3 * HALF + FOX_HEADS], ((0, 0), (0, 128 - FOX_HEADS)))
    wu = wb[:, 3 * HALF + FOX_HEADS:]
    half_out = jax.ShapeDtypeStruct((T, HALF), BF16)
    half_t = jax.ShapeDtypeStruct((HALF, T), BF16)
    cols = pl.BlockSpec((HALF, tm), lambda i: (0, i))
    return pl.pallas_call(
        _even_pre_kernel,
        grid=(T // tm,),
        in_specs=[_rows(tm, D_MODEL), _resident(wqt.shape), _resident(wk.shape), _resident(wvt.shape),
                  _resident(wf.shape), _resident(wu.shape)],
        out_specs=[cols, _rows(tm, HALF), cols, _rows(tm, FOX_HEADS),
                   pl.BlockSpec((S5_SLABS, tm, 128), lambda i: (0, i, 0))],
        out_shape=[half_t, half_out, half_t, jax.ShapeDtypeStruct((T, FOX_HEADS), F32),
                   jax.ShapeDtypeStruct((S5_SLABS, T, 128), BF16)],
        compiler_params=_cparams("parallel"),
        name="even_pre",
    )(x, wqt, wk, wvt, wf, wu)


def _fox_gate_kernel(fg_ref, fb_ref, o_ref, *, L):
    ls = jax.nn.log_sigmoid(fg_ref[...] + fb_ref[...])
    r, c = _iota2((128, 128))
    tri = (r <= c).astype(BF16)
    carry = jnp.zeros((FOX_HEADS, 1), F32)
    for j in range(L // 128):
        cum = _mm_exact_rhs(ls[:, j * 128:(j + 1) * 128], tri) + carry
        o_ref[:, j * 128:(j + 1) * 128] = cum
        carry = cum[:, 127:128]


def _fox_gate(fg_t, fb):
    B, H, L = fg_t.shape
    return pl.pallas_call(
        functools.partial(_fox_gate_kernel, L=L),
        grid=(B,),
        in_specs=[pl.BlockSpec((None, H, L), lambda b: (b, 0, 0)), _resident((H, 1))],
        out_specs=pl.BlockSpec((None, H, L), lambda b: (b, 0, 0)),
        out_shape=jax.ShapeDtypeStruct((B, H, L), F32),
        compiler_params=_cparams("parallel"),
        name="fox_gate",
    )(fg_t, fb.reshape(H, 1))


def _fox_kernel(qt_ref, k_ref, vt_ref, fc_ref, fr_ref, o_ref, qa_s, ka_s, m_s, l_s, acc_s, *, L):
    TQ = TK = FOX_TILE
    H, DH, KA = FOX_HEADS, FOX_DH, FOX_AUG
    heads = range(H)

    t1, t2, t3 = (t.astype(F32) for t in _split3(fr_ref[...]))
    r16 = lax.broadcasted_iota(jnp.int32, (16, 1), 0)
    upper = lax.broadcasted_iota(jnp.int32, (2 * DH, 1), 0) < DH
    for h in heads:
        p, hh = divmod(h, 2)
        qpair = qt_ref[2 * DH * p:2 * DH * (p + 1), :]
        keep = upper if hh == 0 else jnp.logical_not(upper)
        qa_s[h, 0:2 * DH, :] = jnp.where(keep, qpair, jnp.zeros_like(qpair))
        ones_rows = (r16 >= 3 + 3 * hh) & (r16 < 6 + 3 * hh)
        blk = jnp.where(r16 == 0, t1[h:h + 1], jnp.where(r16 == 1, t2[h:h + 1], jnp.where(
            r16 == 2, t3[h:h + 1], jnp.where(ones_rows, 1.0, 0.0))))
        qa_s[h, 2 * DH:2 * DH + 16, :] = blk.astype(BF16)
        qa_s[h, 2 * DH + 16:KA, :] = jnp.zeros((KA - 2 * DH - 16, L), BF16)
    c1, c2, c3 = _split3(fc_ref[...])
    rs, cs_ = _iota2((128, 128))
    lane = lax.broadcasted_iota(jnp.int32, (1, 128), 1)
    ones3 = jnp.where(lane < 3, 1.0, 0.0)
    for p in range(H // 2):
        def sel(i, p=p):
            hit = ((rs == 2 * p) & (cs_ == 3 + i)) | ((rs == 2 * p + 1) & (cs_ == 6 + i))
            return jnp.where(hit, -1.0, 0.0).astype(BF16)
        aug = (jnp.dot(c1, sel(0), preferred_element_type=F32) + jnp.dot(c2, sel(1), preferred_element_type=F32)
               + jnp.dot(c3, sel(2), preferred_element_type=F32) + ones3)
        ka_s[:, KA * p:KA * p + 2 * DH] = k_ref[:, 2 * DH * p:2 * DH * (p + 1)]
        ka_s[:, KA * p + 2 * DH:KA * (p + 1)] = aug.astype(BF16)

    ri, ci = _iota2((TK, TQ))
    visible = ri <= ci

    def q_block(qi, _):
        q0 = pl.multiple_of(qi * TQ, TQ)
        m_s[...] = jnp.full(m_s.shape, NEG_BIG, F32)
        l_s[...] = jnp.zeros(l_s.shape, F32)
        acc_s[...] = jnp.zeros(acc_s.shape, F32)

        def tile(k0, masked):
            kt = [ka_s[pl.ds(k0, TK), KA * p:KA * (p + 1)] for p in range(H // 2)]
            st = [jnp.dot(kt[h // 2], qa_s[h, :, pl.ds(q0, TQ)], preferred_element_type=F32) for h in heads]
            if masked:
                st = [jnp.where(visible, s, NEG_BIG) for s in st]
            m_old = [m_s[h:h + 1, :] for h in heads]
            m_new = [jnp.maximum(m_old[h], jnp.max(st[h], 0, keepdims=True)) for h in heads]
            pt = [jnp.exp(st[h] - m_new[h]) for h in heads]
            pv = [jnp.dot(vt_ref[DH * h:DH * (h + 1), pl.ds(k0, TK)], pt[h].astype(BF16),
                          preferred_element_type=F32) for h in heads]
            for h in heads:
                a = jnp.exp(m_old[h] - m_new[h])
                m_s[h:h + 1, :] = m_new[h]
                l_s[h:h + 1, :] = a * l_s[h:h + 1, :] + jnp.sum(pt[h], 0, keepdims=True)
                acc_s[DH * h:DH * (h + 1), :] = a * acc_s[DH * h:DH * (h + 1), :] + pv[h]

        def k_step(j, carry):
            tile(pl.multiple_of(j * TK, TK), False)
            return carry

        lax.fori_loop(0, qi, k_step, 0)
        tile(q0, True)
        for p in range(H // 2):
            o_pair = jnp.concatenate([acc_s[DH * h:DH * (h + 1), :] / l_s[h:h + 1, :] for h in (2 * p, 2 * p + 1)], 0)
            o_ref[pl.ds(q0, TQ), 2 * DH * p:2 * DH * (p + 1)] = o_pair.T.astype(BF16)
        return 0

    lax.fori_loop(0, L // TQ, q_block, 0)


def _fox_attention(qt, k, vt, f_col, f_row):
    B, H, L = f_row.shape
    seq = pl.BlockSpec((L, HALF), lambda b: (b, 0))
    seq_t = pl.BlockSpec((HALF, L), lambda b: (0, b))
    return pl.pallas_call(
        functools.partial(_fox_kernel, L=L),
        grid=(B,),
        in_specs=[seq_t, seq, seq_t, pl.BlockSpec((L, 128), lambda b: (b, 0)),
                  pl.BlockSpec((None, H, L), lambda b: (b, 0, 0))],
        out_specs=seq,
        out_shape=jax.ShapeDtypeStruct((B * L, HALF), BF16),
        scratch_shapes=[pltpu.VMEM((H, FOX_AUG, L), BF16), pltpu.VMEM((L, FOX_AUG * H // 2), BF16),
                        pltpu.VMEM((H, FOX_TILE), F32), pltpu.VMEM((H, FOX_TILE), F32),
                        pltpu.VMEM((HALF, FOX_TILE), F32)],
        compiler_params=_cparams("parallel"),
        name="fox_attention",
    )(qt, k, vt, f_col, f_row)


def _s5_matrices(a_re, a_im, b_re, b_im, c_re, c_im, d, log_dt):
    G, P, Cg, LC = S5_GROUPS, S5_STATE, S5_GROUP_CH, S5_CHUNK
    a_re, a_im, b_re, b_im, c_re, c_im = (t.astype(F32) for t in (a_re, a_im, b_re, b_im, c_re, c_im))
    dt = jnp.exp(log_dt.astype(F32))[:, None]
    mag = jnp.exp(a_re * dt)
    lam_re, lam_im = mag * jnp.cos(a_im * dt), mag * jnp.sin(a_im * dt)
    den = a_re ** 2 + a_im ** 2
    zr = ((lam_re - 1.0) * a_re + lam_im * a_im) / den
    zi = (lam_im * a_re - (lam_re - 1.0) * a_im) / den
    bb_re = zr[..., None] * b_re - zi[..., None] * b_im
    bb_im = zr[..., None] * b_im + zi[..., None] * b_re
    n = jnp.arange(LC + 1, dtype=F32)[:, None, None]
    pw_mag = jnp.exp(n * (a_re * dt)[None])
    pr, pi = pw_mag * jnp.cos(n * (a_im * dt)[None]), pw_mag * jnp.sin(n * (a_im * dt)[None])
    ein = functools.partial(jnp.einsum, precision=HIGHEST)
    cr = c_re[None] * pr[:, :, None, :] - c_im[None] * pi[:, :, None, :]
    ci = -(c_re[None] * pi[:, :, None, :] + c_im[None] * pr[:, :, None, :])
    kn = ein('ngcp,gpd->ngcd', cr[:LC], bb_re) + ein('ngcp,gpd->ngcd', ci[:LC], bb_im)
    s_idx = jnp.arange(LC)
    lag = s_idx[None, :] - s_idx[:, None]
    kt = jnp.where((lag >= 0)[:, :, None, None, None],
                   kn[jnp.clip(lag, 0, LC - 1)], 0.0)
    dmat = (lag == 0)[:, :, None, None, None] * (d.astype(F32).reshape(1, 1, G, Cg, 1)
                                                 * jnp.eye(Cg, dtype=F32)[None, None, None])
    NS, GS = S5_SLABS, S5_SLAB_GROUPS
    k6 = (kt + dmat).reshape(LC, LC, NS, GS, Cg, Cg)
    kmat = k6.transpose(2, 0, 3, 5, 1, 4).reshape(NS, LC * 128, LC * Cg)
    rr, ri = pr[LC - 1 - s_idx], pi[LC - 1 - s_idx]
    e_re = rr[..., None] * bb_re[None] - ri[..., None] * bb_im[None]
    e_im = rr[..., None] * bb_im[None] + ri[..., None] * bb_re[None]
    e6 = jnp.stack([e_re, e_im]).reshape(2, LC, NS, GS, P, Cg)
    emat = e6.transpose(2, 1, 3, 5, 0, 4).reshape(NS, LC * 128, 2 * P)
    f6 = jnp.stack([cr[1:], ci[1:]]).reshape(2, LC, NS, GS, Cg, P)
    fmat = f6.transpose(2, 0, 3, 5, 1, 4).reshape(NS, 2 * GS * P, LC * Cg)
    lr, li = pr[LC].reshape(NS, 1, GS * P), pi[LC].reshape(NS, 1, GS * P)
    lam_a = jnp.concatenate([lr, lr], axis=-1)
    lam_b = jnp.concatenate([-li, li], axis=-1)
    return kmat.astype(BF16), emat.astype(BF16), fmat.astype(BF16), lam_a, lam_b


def _s5_expand(src_ref, dst_ref, col_of, row_group, col_group):
    n_src, n_dst = src_ref.shape[1], dst_ref.shape[1]
    sr, sc = _iota2((n_src, n_dst))
    select = jnp.where(sr == col_of(sc), 1.0, 0.0).astype(BF16)
    step = 256
    for i in range(src_ref.shape[0] // step):
        rs = slice(i * step, (i + 1) * step)
        r, c = _iota2((step, n_dst))
        same = row_group(r + i * step) == col_group(c)
        full = jnp.dot(src_ref[rs, :], select, preferred_element_type=F32)
        dst_ref[rs, :] = jnp.where(same, full, 0.0).astype(BF16)


def _s5_kernel(u_ref, kc_ref, ec_ref, fc_ref, la_ref, lb_ref, y_ref, k_ref, e_ref, f_ref, e_s, hs_s, h_s, *, B):
    @pl.when(pl.program_id(1) == 0)
    def _():
        h_s[...] = jnp.zeros_like(h_s)
        gs, cg, p = S5_SLAB_GROUPS, S5_GROUP_CH, S5_STATE
        lg = lambda n: n.bit_length() - 1
        lane_group = lambda i: (i >> lg(cg)) & (gs - 1)
        state_group = lambda i: (i >> lg(p)) & (gs - 1)
        frame_ch = lambda c: ((c >> lg(gs * cg)) << lg(cg)) | (c & (cg - 1))
        part_state = lambda c: ((c >> lg(gs * p)) << lg(p)) | (c & (p - 1))
        _s5_expand(kc_ref, k_ref, frame_ch, lane_group, lane_group)
        _s5_expand(ec_ref, e_ref, part_state, lane_group, state_group)
        _s5_expand(fc_ref, f_ref, frame_ch, state_group, lane_group)

    u = u_ref[...]
    e_s[...] = jnp.dot(u, e_ref[...], preferred_element_type=F32)
    la, lb = la_ref[...], lb_ref[...]
    half = h_s.shape[1] // 2
    h = h_s[...]
    for kc in range(u.shape[0] // B):
        rs = slice(kc * B, (kc + 1) * B)
        hs_s[rs, :] = h
        swapped = jnp.concatenate([h[:, half:], h[:, :half]], axis=1)
        h = la * h + lb * swapped + e_s[rs, :]
    h_s[...] = h
    y_ref[...] = (jnp.dot(u, k_ref[...], preferred_element_type=F32)
                  + jnp.dot(hs_s[...].astype(BF16), f_ref[...], preferred_element_type=F32)).astype(BF16)


def _s5(u4, mats, B, L):
    kmat, emat, fmat, lam_a, lam_b = mats
    NS, LC = S5_SLABS, S5_CHUNK
    NK = L // LC
    R, W, S2 = NK * B, LC * 128, 2 * S5_SLAB_GROUPS * S5_STATE
    ug = u4.reshape(NS, B, NK, W).transpose(0, 2, 1, 3).reshape(NS, R, W)
    TR = B * min(NK, S5_TILE_CHUNKS)
    rows = pl.BlockSpec((None, TR, W), lambda s, r: (s, r, 0))
    per_s = lambda a: pl.BlockSpec((None,) + a.shape[1:], lambda s, r: (s, 0, 0), pipeline_mode=pl.Buffered(1))
    y = pl.pallas_call(
        functools.partial(_s5_kernel, B=B),
        grid=(NS, R // TR),
        in_specs=[rows] + [per_s(a) for a in (kmat, emat, fmat, lam_a, lam_b)],
        out_specs=rows,
        out_shape=jax.ShapeDtypeStruct((NS, R, W), BF16),
        scratch_shapes=[pltpu.VMEM((W, W), BF16), pltpu.VMEM((W, S2), BF16), pltpu.VMEM((S2, W), BF16),
                        pltpu.VMEM((TR, S2), F32), pltpu.VMEM((TR, S2), F32), pltpu.VMEM((B, S2), F32)],
        compiler_params=_cparams("parallel", "arbitrary"),
        name="s5",
    )(ug, kmat, emat, fmat, lam_a, lam_b)
    return y.reshape(NS, NK, B, W).transpose(0, 2, 1, 3).reshape(NS, B * L, 128)


def _post_kernel(a_ref, ah_ref, b_ref, bh_ref, x_ref, xh_ref, wt_ref, wb_ref, g1_ref, beta1_ref, *rest,
                 even, tiles_per_seq, alpha):
    if even:
        wg_ref, bg_ref, *rest = rest
    wu_ref, cw_ref, cb_ref, wd_ref, g2_ref, beta2_ref, o_ref = rest
    tm = x_ref.shape[0]
    first = (pl.program_id(0) % tiles_per_seq) == 0
    a = jnp.concatenate([ah_ref[...], a_ref[...]], axis=0)
    x_in = jnp.concatenate([xh_ref[...], x_ref[...]], axis=0)
    if even:
        y = jnp.concatenate([jnp.concatenate([bh_ref[s], b_ref[s]], axis=0) for s in range(S5_SLABS)], axis=1)
        z = jax.nn.gelu(y.astype(F32))
        second = z * jax.nn.sigmoid(_mm(z, wg_ref[...]) + bg_ref[...])
    else:
        second = jnp.concatenate([bh_ref[...], b_ref[...]], axis=0)
    mix = _mm(a, wt_ref[...]) + _mm(second, wb_ref[...])
    x1e = _layer_norm(alpha * x_in + mix, g1_ref[...], beta1_ref[...])
    row = lax.broadcasted_iota(jnp.int32, (HALO + tm, 1), 0)
    x1e = jnp.where((row < HALO) & first, 0.0, x1e)
    x1 = x1e[HALO:]
    xe = x1e.astype(BF16)
    xb = xe[HALO:]

    def up(c):
        ue = jnp.dot(xe, wu_ref[:, c * FFN_TILE:(c + 1) * FFN_TILE], preferred_element_type=F32)
        gate = jnp.dot(xb, wu_ref[:, D_FF + c * FFN_TILE:D_FF + (c + 1) * FFN_TILE], preferred_element_type=F32)
        return ue, gate

    acc = jnp.zeros((tm, D_MODEL), F32)
    n_tiles = D_FF // FFN_TILE
    nxt = up(0)
    for c in range(n_tiles):
        cs = slice(c * FFN_TILE, (c + 1) * FFN_TILE)
        ue, gate = nxt
        if c + 1 < n_tiles:
            nxt = up(c + 1)
        cw = cw_ref[:, cs]
        u = (cw[2:3] * ue[HALO:] + cw[1:2] * ue[HALO - 1:HALO - 1 + tm] + cw[0:1] * ue[HALO - 2:HALO - 2 + tm]
             + cb_ref[:, cs])
        acc = acc + _mm(jax.nn.gelu(u) * gate, wd_ref[cs, :])
    o_ref[...] = _layer_norm(alpha * x1 + acc, g2_ref[...], beta2_ref[...])


def _post(a, b, x, L, w_out, ln1_g, ln1_b, w_up, conv_w, conv_b, w_down, ln2_g, ln2_b, alpha, glu=None, tm=1024):
    T = x.shape[0]
    tm = min(tm, L)
    wo = w_out.astype(BF16)
    row1 = lambda v, n: v.reshape(1, n).astype(F32)
    slab = lambda rows, scale: pl.BlockSpec((S5_SLABS, rows, 128), scale)
    b_specs = [_rows(tm, HALF), _halo_rows(tm, HALF)]
    if glu is not None:
        b_specs = [slab(tm, lambda i: (0, i, 0)),
                   slab(HALO, lambda i: (0, jnp.maximum(i * (tm // HALO) - 1, 0), 0))]
    args = [a, a, b, b, x, x, wo[:HALF], wo[HALF:], row1(ln1_g, D_MODEL), row1(ln1_b, D_MODEL)]
    specs = ([_rows(tm, HALF), _halo_rows(tm, HALF)] + b_specs + [_rows(tm, D_MODEL), _halo_rows(tm, D_MODEL)]
             + [_resident(t.shape) for t in args[6:]])
    tail = [w_up.astype(BF16), conv_w.astype(F32), row1(conv_b, D_FF), w_down.astype(BF16),
            row1(ln2_g, D_MODEL), row1(ln2_b, D_MODEL)]
    if glu is not None:
        tail = [glu[0].astype(BF16), row1(glu[1], HALF)] + tail
    return pl.pallas_call(
        functools.partial(_post_kernel, even=glu is not None, tiles_per_seq=L // tm, alpha=alpha),
        grid=(T // tm,),
        in_specs=specs + [_resident(t.shape) for t in tail],
        out_specs=_rows(tm, D_MODEL),
        out_shape=jax.ShapeDtypeStruct((T, D_MODEL), F32),
        compiler_params=_cparams("parallel"),
        name="post",
    )(*args, *tail)


_LORA_PAD = (128, 128, 256)
_RWKV_PAD = 3 * HALF + sum(_LORA_PAD)


def _odd_pre_kernel(x_ref, xh_ref, wrk_ref, mu_ref, w2_ref, a2_ref, g2_ref, vec_ref, bd_ref,
                    wqk_ref, cw_ref, cb_ref, wv_ref, wz_ref, wif_ref, gb_ref,
                    r_ref, k_ref, v_ref, lw_ref, kk_ref, ka_ref, g_ref, bo_ref,
                    mq_ref, mk_ref, mv_ref, mz_ref, gate_ref, *, tiles_per_seq):
    tm = x_ref.shape[0]
    first = (pl.program_id(0) % tiles_per_seq) == 0
    xh = jnp.where(first, 0.0, xh_ref[...])
    xe = jnp.concatenate([xh, x_ref[...]], axis=0).astype(BF16)
    xb = xe[HALO:]

    pe = jnp.dot(xe, wrk_ref[...], preferred_element_type=F32)
    qke = jnp.dot(xe, wqk_ref[...], preferred_element_type=F32)
    mv_ref[...] = jnp.dot(xb, wv_ref[...], preferred_element_type=F32).astype(BF16)
    mz_ref[...] = jnp.dot(xb, wz_ref[...], preferred_element_type=F32).astype(mz_ref.dtype)
    pre = jnp.dot(xb, wif_ref[...], preferred_element_type=F32)[:, :2 * MLSTM_HEADS] + gb_ref[...]

    cur, prev = pe[HALO:], pe[HALO - 1:HALO - 1 + tm]
    p = cur + (prev - cur) * mu_ref[...]
    r, k, v = p[:, :HALF], p[:, HALF:2 * HALF], p[:, 2 * HALF:3 * HALF]
    o0 = 3 * HALF
    wd = p[:, o0:o0 + _LORA_PAD[0]]
    ad = p[:, o0 + _LORA_PAD[0]:o0 + _LORA_PAD[0] + _LORA_PAD[1]]
    gd = p[:, o0 + _LORA_PAD[0] + _LORA_PAD[1]:]
    w0, a0, k_k, k_a, r_k = (vec_ref[i:i + 1, :] for i in range(5))
    wlog = -jax.nn.softplus(-(w0 + _mm(jnp.tanh(wd), w2_ref[...]))) - 0.5
    lw_ref[...] = -jnp.exp(wlog)
    a = jax.nn.sigmoid(a0 + _mm(ad, a2_ref[...]))
    g_ref[...] = _mm(jax.nn.sigmoid(gd), g2_ref[...]).astype(g_ref.dtype)
    kk = k * k_k
    ss = _mm_exact_rhs(kk * kk, bd_ref[...])
    kk = kk * lax.rsqrt(jnp.maximum(ss, 1e-24))
    kmod = k * (1.0 + (a - 1.0) * k_a)
    bo_ref[...] = (_mm_exact_rhs(r * kmod * r_k, bd_ref[...]) * v).astype(bo_ref.dtype)
    r_ref[...] = r.astype(r_ref.dtype)
    k_ref[...] = kmod.astype(k_ref.dtype)
    v_ref[...] = v.astype(v_ref.dtype)
    kk_ref[...] = kk.astype(kk_ref.dtype)
    ka_ref[...] = (kk * a).astype(ka_ref.dtype)

    cw = cw_ref[...]
    qk = cb_ref[...]
    for j in range(MLSTM_CONV):
        off = HALO - (MLSTM_CONV - 1) + j
        qk = qk + cw[j:j + 1] * qke[off:off + tm]
    qk = jax.nn.silu(qk)
    mq_ref[...] = qk[:, :HALF].astype(mq_ref.dtype)
    mk_ref[...] = (qk[:, HALF:] * (MLSTM_DH ** -0.5)).astype(BF16)
    is_i = lax.broadcasted_iota(jnp.int32, pre.shape, 1) < MLSTM_HEADS
    gate_ref[...] = jnp.where(is_i, pre, jax.nn.log_sigmoid(pre))


def _head_block_ones(width, head):
    idx = jnp.arange(width) // head
    return (idx[:, None] == idx[None, :]).astype(BF16)


def _odd_pre(x, L, w_in, mu, w0, w2, a0, a2, g2, k_k, k_a, r_k, conv_w, conv_b, ib, fb, tm=512):
    T = x.shape[0]
    tm = min(tm, L)
    wb = w_in.astype(BF16)
    o = 3 * HALF
    sizes = (RWKV_DECAY_LORA, RWKV_AAA_LORA, RWKV_GATE_LORA)

    def pad_lora(m, axis):
        parts, s = [], o
        for sz, pd in zip(sizes, _LORA_PAD):
            piece = lax.slice_in_dim(m, s, s + sz, axis=axis)
            widths = [(0, 0)] * m.ndim
            widths[axis] = (0, pd - sz)
            parts.append(jnp.pad(piece, widths))
            s += sz
        return jnp.concatenate([lax.slice_in_dim(m, 0, o, axis=axis)] + parts, axis=axis)

    rwkv_proj = o + sum(sizes)
    wrk = pad_lora(wb[:, :rwkv_proj], 1)
    mu_p = pad_lora(mu.astype(F32).reshape(1, -1), 1)
    padr = lambda m, rows: jnp.pad(m.astype(BF16), ((0, rows - m.shape[0]), (0, 0)))
    w2p, a2p, g2p = padr(w2, _LORA_PAD[0]), padr(a2, _LORA_PAD[1]), padr(g2, _LORA_PAD[2])
    vecs = jnp.stack([w0, a0, k_k, k_a, r_k.reshape(HALF)]).astype(F32)
    vecs = jnp.pad(vecs, ((0, 8 - vecs.shape[0]), (0, 0)))
    bd = _head_block_ones(HALF, RWKV_N)
    wm = wb[:, rwkv_proj:]
    wqk, wmv, wmz = wm[:, :2 * HALF], wm[:, 2 * HALF:3 * HALF], wm[:, 3 * HALF:4 * HALF]
    wif = jnp.pad(wm[:, 4 * HALF:], ((0, 0), (0, 128 - 2 * MLSTM_HEADS)))
    gbias = jnp.concatenate([ib, fb]).astype(F32).reshape(1, 2 * MLSTM_HEADS)
    args = [x, x, wrk, mu_p, w2p, a2p, g2p, vecs, bd, wqk, conv_w.astype(F32),
            conv_b.astype(F32).reshape(1, 2 * HALF), wmv, wmz, wif, gbias]
    specs = [_rows(tm, D_MODEL), _halo_rows(tm, D_MODEL)] + [_resident(a.shape) for a in args[2:]]
    f_half = jax.ShapeDtypeStruct((T, HALF), F32)
    b_half = jax.ShapeDtypeStruct((T, HALF), BF16)
    out_shape = [b_half] * 3 + [f_half] + [b_half] * 8 + [jax.ShapeDtypeStruct((T, 2 * MLSTM_HEADS), F32)]
    out_specs = [_rows(tm, HALF)] * 12 + [_rows(tm, 2 * MLSTM_HEADS)]
    return pl.pallas_call(
        functools.partial(_odd_pre_kernel, tiles_per_seq=L // tm),
        grid=(T // tm,),
        in_specs=specs,
        out_specs=out_specs,
        out_shape=out_shape,
        compiler_params=_cparams("parallel"),
        name="odd_pre",
    )(*args)


def _rwkv_kernel(r_ref, k_ref, v_ref, lw_ref, kk_ref, ka_ref, g_ref, bo_ref, lng_ref, lnb_ref, o_ref, m_s):
    C, N, H = CHUNK, RWKV_N, RWKV_HEADS
    rows = range(r_ref.shape[0])
    units = [(b, h) for b in rows for h in range(H)]

    @pl.when(pl.program_id(1) == 0)
    def _():
        m_s[...] = jnp.zeros_like(m_s)

    ri, ci = _iota2((C, C))
    incl, strict, eye = ci <= ri, ci < ri, ci == ri
    ri2, ci2 = _iota2((C, 2 * C))
    incl2 = (ci2 & (C - 1)) <= ri2
    zeros = jnp.zeros((C, N), F32)
    mm = functools.partial(_mmp, passes=RWKV_PASSES)
    al, rt, bt, kt, bp, kp, gam, v = ({} for _ in range(8))
    for b in rows:
        lw = lw_ref[b]
        cs = _mm_exact_lhs(incl.astype(BF16), lw)
        cend = cs[C - 1:C, :]
        e_neg = jnp.exp(-cs)
        e_rem = jnp.exp(cend - cs)
        r, k, vv, kk, ka = (t[b].astype(F32) for t in (r_ref, k_ref, v_ref, kk_ref, ka_ref))
        al_b = -kk * jnp.exp(cs - lw)
        rt_b = r * jnp.exp(cs)
        gam_b = jnp.exp(cend)
        bt_b, kt_b, bp_b, kp_b = ka * e_neg, k * e_neg, ka * e_rem, k * e_rem
        for h in range(H):
            s = slice(N * h, N * (h + 1))
            u = (b, h)
            al[u], rt[u], v[u], gam[u] = al_b[:, s], rt_b[:, s], vv[:, s], gam_b[:, s]
            bt[u], kt[u], bp[u], kp[u] = bt_b[:, s], kt_b[:, s], bp_b[:, s], kp_b[:, s]
    pm = {u: mm(jnp.concatenate([al[u], rt[u]], 0), jnp.concatenate([bt[u], kt[u]], 0), 1, 1) for u in units}
    a_ab = {u: jnp.where(strict, pm[u][:C, :C], 0.0) for u in units}
    a_ak = {u: jnp.where(strict, pm[u][:C, C:], 0.0) for u in units}
    a_r = {u: jnp.where(incl2, pm[u][C:, :], 0.0) for u in units}
    w = {u: jnp.concatenate([al[u], mm(a_ak[u], v[u])], 1) for u in units}
    npow = a_ab
    levels = int(math.log2(C))
    for lvl in range(levels):
        if lvl < levels - 1:
            y = {u: mm(npow[u], jnp.concatenate([w[u], npow[u]], 1)) for u in units}
            w = {u: w[u] + y[u][:, :2 * N] for u in units}
            npow = {u: y[u][:, 2 * N:] for u in units}
        else:
            w = {u: w[u] + mm(npow[u], w[u]) for u in units}
    zv = {u: jnp.concatenate([zeros, v[u]], 1) for u in units}
    qt = {u: mm(a_r[u], jnp.concatenate([w[u], zv[u]], 0)) for u in units}
    qb = {u: mm(bp[u], w[u], 0, 0) + mm(kp[u], zv[u], 0, 0) for u in units}
    m0 = {u: m_s[u[0], u[1]] for u in units}
    o_h = {u: mm(rt[u] + qt[u][:, :N], m0[u]) + qt[u][:, N:] for u in units}
    m_new = {u: _mmp(qb[u][:, :N] + jnp.where(eye, gam[u], 0.0), m0[u], passes=RWKV_STATE_PASSES) + qb[u][:, N:]
             for u in units}
    for b in rows:
        outs = []
        for h in range(H):
            u = (b, h)
            m_s[b, h] = m_new[u]
            om = jnp.mean(o_h[u], -1, keepdims=True)
            d = o_h[u] - om
            ov = jnp.mean(d * d, -1, keepdims=True)
            outs.append(d * lax.rsqrt(ov + RWKV_GN_EPS))
        on = jnp.concatenate(outs, 1) * lng_ref[...] + lnb_ref[...]
        o_ref[b] = ((on + bo_ref[b].astype(F32)) * g_ref[b].astype(F32)).astype(BF16)


def _rwkv(r, k, v, lw, kk, ka, g, bonus, ln_g, ln_b, B, L):
    NC = L // CHUNK
    RB = SEQ_ROWS if B % SEQ_ROWS == 0 else 1
    seq3 = lambda t: t.reshape(B, L, HALF)
    blk = pl.BlockSpec((RB, CHUNK, HALF), lambda b, c: (b, c, 0))
    vec = pl.BlockSpec((1, HALF), lambda b, c: (0, 0))
    out = pl.pallas_call(
        _rwkv_kernel,
        grid=(B // RB, NC),
        in_specs=[blk] * 8 + [vec, vec],
        out_specs=blk,
        out_shape=jax.ShapeDtypeStruct((B, L, HALF), BF16),
        scratch_shapes=[pltpu.VMEM((RB, RWKV_HEADS, RWKV_N, RWKV_N), F32)],
        compiler_params=_cparams("parallel", "arbitrary"),
        name="rwkv7",
    )(*(seq3(t) for t in (r, k, v, lw, kk, ka, g, bonus)),
      ln_g.reshape(1, HALF).astype(F32), ln_b.reshape(1, HALF).astype(F32))
    return out.reshape(B * L, HALF)


def _mlstm_kernel(q_ref, k_ref, v_ref, z_ref, gc_ref, gr_ref, lng_ref, skip_ref, o_ref, c_s, m_s):
    C, H, DH = CHUNK, MLSTM_HEADS, MLSTM_DH
    rows = range(q_ref.shape[0])
    units = [(b, h) for b in rows for h in range(H)]

    @pl.when(pl.program_id(1) == 0)
    def _():
        c_s[...] = jnp.zeros_like(c_s)
        m_s[...] = jnp.zeros_like(m_s)

    ri, ci = _iota2((C, C))
    incl = ci <= ri
    lower, upper = incl.astype(BF16), (ri <= ci).astype(BF16)
    lane = lax.broadcasted_iota(jnp.int32, (1, DH), 1)
    ones_blk = jnp.broadcast_to(jnp.where(lane == 0, 1.0, 0.0).astype(BF16), (C, DH))
    q_all, qb, kh, vaug, i_col, i_row, b_col, b_row, m_prev = ({} for _ in range(9))
    for b in rows:
        gc, gr = gc_ref[b], gr_ref[b]
        b_cols = _mm_exact_lhs(lower, gc)
        b_rows = _mm_exact_rhs(gr, upper)
        q_all[b] = q_ref[b].astype(F32)
        for h in range(H):
            u, s = (b, h), slice(DH * h, DH * (h + 1))
            qb[u], kh[u] = q_ref[b, :, s], k_ref[b, :, s]
            vaug[u] = jnp.concatenate([v_ref[b, :, s], ones_blk], 1)
            i_col[u], i_row[u] = gc[:, h:h + 1], gr[h:h + 1, :]
            b_col[u], b_row[u] = b_cols[:, H + h:H + h + 1], b_rows[H + h:H + h + 1, :]
            m_prev[u] = m_s[b, h:h + 1, 0:1]
    c_old = {u: c_s[u[0], u[1]] for u in units}
    qk = {u: _dg(qb[u], kh[u], 1, 1) for u in units}
    qc = {u: jnp.dot(qb[u], c_old[u].astype(BF16), preferred_element_type=F32) for u in units}
    dmat = {u: jnp.where(incl, b_col[u] - b_row[u] + i_row[u], NEG_BIG) for u in units}
    inter = {u: b_col[u] + m_prev[u] for u in units}
    m_t = {u: jnp.maximum(inter[u], jnp.max(dmat[u], -1, keepdims=True)) for u in units}
    s = {u: (qk[u] * jnp.exp(dmat[u] - m_t[u])).astype(BF16) for u in units}
    sv = {u: jnp.dot(s[u], vaug[u], preferred_element_type=F32) for u in units}
    b_end = {u: b_col[u][C - 1:C, :] for u in units}
    m_new = {u: jnp.maximum(b_end[u] + m_prev[u], jnp.max(b_end[u] - b_row[u] + i_row[u], -1, keepdims=True))
             for u in units}
    kw = {u: (jnp.exp(b_end[u] - b_col[u] + i_col[u] - m_new[u]) * kh[u].astype(F32)).astype(BF16) for u in units}
    kv = {u: _dg(kw[u], vaug[u], 0, 0) for u in units}
    for b in rows:
        outs = []
        for h in range(H):
            u = (b, h)
            c_s[b, h] = jnp.exp(b_end[u] + m_prev[u] - m_new[u]) * c_old[u] + kv[u]
            m_s[b, h:h + 1, :] = jnp.broadcast_to(m_new[u], (1, 128))
            nd = jnp.exp(inter[u] - m_t[u]) * qc[u] + sv[u]
            hid = nd[:, :DH] / jnp.maximum(jnp.abs(nd[:, DH:DH + 1]), jnp.exp(-m_t[u]))
            hm = jnp.mean(hid, -1, keepdims=True)
            d = hid - hm
            hv = jnp.mean(d * d, -1, keepdims=True)
            outs.append(d * lax.rsqrt(hv + LN_EPS))
        hn = jnp.concatenate(outs, 1) * lng_ref[...]
        o_ref[b] = ((hn + skip_ref[...] * q_all[b]) * jax.nn.silu(z_ref[b].astype(F32))).astype(BF16)


def _mlstm(q, k, v, z, gates, ln_g, skip, B, L):
    NC, H2 = L // CHUNK, 2 * MLSTM_HEADS
    RB = MLSTM_SEQ_ROWS if B % MLSTM_SEQ_ROWS == 0 else 1
    seq3 = lambda t: t.reshape(B, L, t.shape[-1])
    blk = pl.BlockSpec((RB, CHUNK, HALF), lambda b, c: (b, c, 0))
    vec = pl.BlockSpec((1, HALF), lambda b, c: (0, 0))
    g_rows = gates.reshape(B, NC, CHUNK, H2).transpose(0, 1, 3, 2)
    out = pl.pallas_call(
        _mlstm_kernel,
        grid=(B // RB, NC),
        in_specs=[blk, blk, blk, blk, pl.BlockSpec((RB, CHUNK, H2), lambda b, c: (b, c, 0)),
                  pl.BlockSpec((RB, None, H2, CHUNK), lambda b, c: (b, c, 0, 0)), vec, vec],
        out_specs=blk,
        out_shape=jax.ShapeDtypeStruct((B, L, HALF), BF16),
        scratch_shapes=[pltpu.VMEM((RB, MLSTM_HEADS, MLSTM_DH, 2 * MLSTM_DH), F32),
                        pltpu.VMEM((RB, 8, 128), F32)],
        compiler_params=_cparams("parallel", "arbitrary"),
        name="mlstm",
    )(seq3(q), seq3(k), seq3(v), seq3(z), seq3(gates), g_rows,
      ln_g.reshape(1, HALF).astype(F32), skip.reshape(1, HALF).astype(F32))
    return out.reshape(B * L, HALF)


def kernel(x, ev_w_in, ev_fox_fb, ev_s5_a_re, ev_s5_a_im, ev_s5_b_re, ev_s5_b_im, ev_s5_c_re, ev_s5_c_im, ev_s5_d, ev_s5_log_dt, ev_s5_w_glu, ev_s5_b_glu, ev_w_out, od_w_in, od_rwkv_mu, od_rwkv_w0, od_rwkv_w2, od_rwkv_a0, od_rwkv_a2, od_rwkv_g2, od_rwkv_k_k, od_rwkv_k_a, od_rwkv_r_k, od_rwkv_ln_g, od_rwkv_ln_b, od_mlstm_conv_w, od_mlstm_conv_b, od_mlstm_ib, od_mlstm_fb, od_mlstm_ln_g, od_mlstm_skip, od_w_out, ln1_g, ln1_b, ffn_w_up, ffn_conv_w, ffn_conv_b, ffn_w_down, ln2_g, ln2_b):
    B, L, _ = x.shape
    depth = ln1_g.shape[0]
    alpha = float((2 * depth) ** 0.25)
    h = x.reshape(B * L, D_MODEL).astype(F32)
    for layer in range(depth):
        i = layer // 2
        if layer % 2 == 0:
            qt, k, vt, fg, u = _even_pre(h, ev_w_in[i])
            f_row = _fox_gate(fg.reshape(B, L, FOX_HEADS).transpose(0, 2, 1), ev_fox_fb[i].astype(F32))
            f_col = jnp.pad(f_row.transpose(0, 2, 1).reshape(B * L, FOX_HEADS), ((0, 0), (0, 128 - FOX_HEADS)))
            fox = _fox_attention(qt, k, vt, f_col, f_row)
            mats = _s5_matrices(ev_s5_a_re[i], ev_s5_a_im[i], ev_s5_b_re[i], ev_s5_b_im[i],
                                ev_s5_c_re[i], ev_s5_c_im[i], ev_s5_d[i], ev_s5_log_dt[i])
            y = _s5(u, mats, B, L)
            mixed, glu, w_out = (fox, y), (ev_s5_w_glu[i], ev_s5_b_glu[i]), ev_w_out[i]
        else:
            (r, k, v, lw, kk, ka, g, bonus, mq, mk, mv, mz, gates) = _odd_pre(
                h, L, od_w_in[i], od_rwkv_mu[i], od_rwkv_w0[i], od_rwkv_w2[i], od_rwkv_a0[i], od_rwkv_a2[i],
                od_rwkv_g2[i], od_rwkv_k_k[i], od_rwkv_k_a[i], od_rwkv_r_k[i], od_mlstm_conv_w[i],
                od_mlstm_conv_b[i], od_mlstm_ib[i], od_mlstm_fb[i])
            c = _rwkv(r, k, v, lw, kk, ka, g, bonus, od_rwkv_ln_g[i], od_rwkv_ln_b[i], B, L)
            dm = _mlstm(mq, mk, mv, mz, gates, od_mlstm_ln_g[i], od_mlstm_skip[i], B, L)
            mixed, glu, w_out = (c, dm), None, od_w_out[i]
        h = _post(mixed[0], mixed[1], h, L, w_out, ln1_g[layer], ln1_b[layer], ffn_w_up[layer], ffn_conv_w[layer],
                  ffn_conv_b[layer], ffn_w_down[layer], ln2_g[layer], ln2_b[layer], alpha, glu=glu)
    return h.reshape(B, L, D_MODEL).astype(x.dtype)
```

```python
import functools
import math

import jax
import jax.numpy as jnp
from jax import lax
from jax.experimental import pallas as pl
from jax.experimental.pallas import tpu as pltpu

F32 = jnp.float32
BF16 = jnp.bfloat16
HIGHEST = lax.Precision.HIGHEST

D_MODEL = 1024
HALF = D_MODEL // 2
FOX_HEADS = 8
FOX_DH = HALF // FOX_HEADS
FOX_TILE = 256
FOX_AUG = 256
S5_GROUPS = 32
S5_GROUP_CH = 16
S5_STATE = 64
S5_CHUNK = 16
S5_SLAB_GROUPS = 128 // S5_GROUP_CH
S5_SLABS = S5_GROUPS // S5_SLAB_GROUPS
S5_TILE_CHUNKS = 16
RWKV_HEADS = 8
RWKV_N = HALF // RWKV_HEADS
RWKV_DECAY_LORA = 64
RWKV_AAA_LORA = 64
RWKV_GATE_LORA = 160
RWKV_GN_EPS = 64e-5
RWKV_PASSES = 1
RWKV_STATE_PASSES = 3
MLSTM_HEADS = 4
MLSTM_DH = HALF // MLSTM_HEADS
MLSTM_CONV = 4
CHUNK = 64
SEQ_ROWS = 4
MLSTM_SEQ_ROWS = 1
D_FF = 2816
FFN_CONV = 3
FFN_TILE = 256
LN_EPS = 1e-5
HALO = 16
NEG_BIG = -1e30
V7X_VMEM_LIMIT_BYTES = 56 * 1024 * 1024


def _cparams(*sem):
    return pltpu.CompilerParams(dimension_semantics=sem, vmem_limit_bytes=V7X_VMEM_LIMIT_BYTES)


def _resident(shape):
    nd = len(shape)
    return pl.BlockSpec(shape, lambda *_: (0,) * nd, pipeline_mode=pl.Buffered(1))


def _rows(tm, width):
    return pl.BlockSpec((tm, width), lambda i: (i, 0))


def _halo_rows(tm, width):
    return pl.BlockSpec((HALO, width), lambda i: (jnp.maximum(i * (tm // HALO) - 1, 0), 0))


def _mm(a, b):
    return jnp.dot(a.astype(BF16), b.astype(BF16), preferred_element_type=F32)


def _dg(a, b, ca, cb):
    return lax.dot_general(a, b, (((ca,), (cb,)), ((), ())), preferred_element_type=F32)


def _hi_lo(a):
    hi = a.astype(BF16)
    lo = (a - hi.astype(F32)).astype(BF16)
    return hi, lo


def _mm3(a, b, ca=1, cb=0):
    ah, al = _hi_lo(a)
    bh, bl = _hi_lo(b)
    return _dg(ah, bh, ca, cb) + _dg(ah, bl, ca, cb) + _dg(al, bh, ca, cb)


def _mmp(a, b, ca=1, cb=0, passes=1):
    if passes == 3:
        return _mm3(a, b, ca, cb)
    return _dg(a.astype(BF16), b.astype(BF16), ca, cb)


def _split3(a):
    a1 = a.astype(BF16)
    r1 = a - a1.astype(F32)
    a2 = r1.astype(BF16)
    a3 = (r1 - a2.astype(F32)).astype(BF16)
    return a1, a2, a3


def _mm_exact_rhs(a, b01):
    a1, a2, a3 = _split3(a)
    return (jnp.dot(a1, b01, preferred_element_type=F32) + jnp.dot(a2, b01, preferred_element_type=F32)
            + jnp.dot(a3, b01, preferred_element_type=F32))


def _mm_exact_lhs(a01, b):
    b1, b2, b3 = _split3(b)
    return (jnp.dot(a01, b1, preferred_element_type=F32) + jnp.dot(a01, b2, preferred_element_type=F32)
            + jnp.dot(a01, b3, preferred_element_type=F32))


def _layer_norm(x, g, b):
    mu = jnp.mean(x, -1, keepdims=True)
    d = x - mu
    var = jnp.mean(d * d, -1, keepdims=True)
    return d * lax.rsqrt(var + LN_EPS) * g + b


def _shift_rows(x, n):
    return x if n == 0 else pltpu.roll(x, n, 0)


def _iota2(shape):
    return lax.broadcasted_iota(jnp.int32, shape, 0), lax.broadcasted_iota(jnp.int32, shape, 1)


def _even_pre_kernel(x_ref, wqt_ref, wk_ref, wvt_ref, wf_ref, wu_ref, qt_ref, k_ref, vt_ref, fg_ref, u_ref):
    xb = x_ref[...].astype(BF16)
    qt_ref[...] = (_dg(wqt_ref[...], xb, 1, 1) * (FOX_DH ** -0.5)).astype(BF16)
    vt_ref[...] = _dg(wvt_ref[...], xb, 1, 1).astype(BF16)
    k_ref[...] = jnp.dot(xb, wk_ref[...], preferred_element_type=F32).astype(BF16)
    u = jnp.dot(xb, wu_ref[...], preferred_element_type=F32).astype(BF16)
    for s in range(S5_SLABS):
        u_ref[s] = u[:, 128 * s:128 * (s + 1)]
    fg_ref[...] = jnp.dot(xb, wf_ref[...], preferred_element_type=F32)[:, :FOX_HEADS]


def _even_pre(x, w_in, tm=512):
    T = x.shape[0]
    wb = w_in.astype(BF16)
    wqt, wk, wvt = wb[:, :HALF].T, wb[:, HALF:2 * HALF], wb[:, 2 * HALF:3 * HALF].T
    wf = jnp.pad(wb[:, 3 * HALF:3 * HALF + FOX_HEADS], ((0, 0), (0, 128 - FOX_HEADS)))
    wu = wb[:, 3 * HALF + FOX_HEADS:]
    half_out = jax.ShapeDtypeStruct((T, HALF), BF16)
    half_t = jax.ShapeDtypeStruct((HALF, T), BF16)
    cols = pl.BlockSpec((HALF, tm), lambda i: (0, i))
    return pl.pallas_call(
        _even_pre_kernel,
        grid=(T // tm,),
        in_specs=[_rows(tm, D_MODEL), _resident(wqt.shape), _resident(wk.shape), _resident(wvt.shape),
                  _resident(wf.shape), _resident(wu.shape)],
        out_specs=[cols, _rows(tm, HALF), cols, _rows(tm, FOX_HEADS),
                   pl.BlockSpec((S5_SLABS, tm, 128), lambda i: (0, i, 0))],
        out_shape=[half_t, half_out, half_t, jax.ShapeDtypeStruct((T, FOX_HEADS), F32),
                   jax.ShapeDtypeStruct((S5_SLABS, T, 128), BF16)],
        compiler_params=_cparams("parallel"),
        name="even_pre",
    )(x, wqt, wk, wvt, wf, wu)


def _fox_gate_kernel(fg_ref, fb_ref, o_ref, *, L):
    ls = jax.nn.log_sigmoid(fg_ref[...] + fb_ref[...])
    r, c = _iota2((128, 128))
    tri = (r <= c).astype(BF16)
    carry = jnp.zeros((FOX_HEADS, 1), F32)
    for j in range(L // 128):
        cum = _mm_exact_rhs(ls[:, j * 128:(j + 1) * 128], tri) + carry
        o_ref[:, j * 128:(j + 1) * 128] = cum
        carry = cum[:, 127:128]


def _fox_gate(fg_t, fb):
    B, H, L = fg_t.shape
    return pl.pallas_call(
        functools.partial(_fox_gate_kernel, L=L),
        grid=(B,),
        in_specs=[pl.BlockSpec((None, H, L), lambda b: (b, 0, 0)), _resident((H, 1))],
        out_specs=pl.BlockSpec((None, H, L), lambda b: (b, 0, 0)),
        out_shape=jax.ShapeDtypeStruct((B, H, L), F32),
        compiler_params=_cparams("parallel"),
        name="fox_gate",
    )(fg_t, fb.reshape(H, 1))


def _fox_kernel(qt_ref, k_ref, vt_ref, fc_ref, fr_ref, o_ref, qa_s, ka_s, m_s, l_s, acc_s, st_s, *, L):
    TQ = TK = FOX_TILE
    H, DH, KA = FOX_HEADS, FOX_DH, FOX_AUG
    heads = range(H)

    t1, t2, t3 = (t.astype(F32) for t in _split3(fr_ref[...]))
    r16 = lax.broadcasted_iota(jnp.int32, (16, 1), 0)
    upper = lax.broadcasted_iota(jnp.int32, (2 * DH, 1), 0) < DH
    for h in heads:
        p, hh = divmod(h, 2)
        qpair = qt_ref[2 * DH * p:2 * DH * (p + 1), :]
        keep = upper if hh == 0 else jnp.logical_not(upper)
        qa_s[h, 0:2 * DH, :] = jnp.where(keep, qpair, jnp.zeros_like(qpair))
        ones_rows = (r16 >= 3 + 3 * hh) & (r16 < 6 + 3 * hh)
        blk = jnp.where(r16 == 0, t1[h:h + 1], jnp.where(r16 == 1, t2[h:h + 1], jnp.where(
            r16 == 2, t3[h:h + 1], jnp.where(ones_rows, 1.0, 0.0))))
        qa_s[h, 2 * DH:2 * DH + 16, :] = blk.astype(BF16)
        qa_s[h, 2 * DH + 16:KA, :] = jnp.zeros((KA - 2 * DH - 16, L), BF16)
    c1, c2, c3 = _split3(fc_ref[...])
    rs, cs_ = _iota2((128, 128))
    lane = lax.broadcasted_iota(jnp.int32, (1, 128), 1)
    ones3 = jnp.where(lane < 3, 1.0, 0.0)
    for p in range(H // 2):
        def sel(i, p=p):
            hit = ((rs == 2 * p) & (cs_ == 3 + i)) | ((rs == 2 * p + 1) & (cs_ == 6 + i))
            return jnp.where(hit, -1.0, 0.0).astype(BF16)
        aug = (jnp.dot(c1, sel(0), preferred_element_type=F32) + jnp.dot(c2, sel(1), preferred_element_type=F32)
               + jnp.dot(c3, sel(2), preferred_element_type=F32) + ones3)
        ka_s[:, KA * p:KA * p + 2 * DH] = k_ref[:, 2 * DH * p:2 * DH * (p + 1)]
        ka_s[:, KA * p + 2 * DH:KA * (p + 1)] = aug.astype(BF16)

    ri, ci = _iota2((TK, TQ))
    visible = ri <= ci

    def q_block(qi, _):
        q0 = pl.multiple_of(qi * TQ, TQ)
        m_s[...] = jnp.full(m_s.shape, NEG_BIG, F32)
        l_s[...] = jnp.zeros(l_s.shape, F32)
        acc_s[...] = jnp.zeros(acc_s.shape, F32)

        def scores(slot, k0, masked):
            kt = [ka_s[pl.ds(k0, TK), KA * p:KA * (p + 1)] for p in range(H // 2)]
            for h in heads:
                s = jnp.dot(kt[h // 2], qa_s[h, :, pl.ds(q0, TQ)], preferred_element_type=F32)
                st_s[slot, h] = jnp.where(visible, s, NEG_BIG) if masked else s

        def consume(slot, k0):
            k0 = pl.multiple_of(k0, TK)
            st = [st_s[slot, h] for h in heads]
            m_old = [m_s[h:h + 1, :] for h in heads]
            m_new = [jnp.maximum(m_old[h], jnp.max(st[h], 0, keepdims=True)) for h in heads]
            pt = [jnp.exp(st[h] - m_new[h]) for h in heads]
            pv = [jnp.dot(vt_ref[DH * h:DH * (h + 1), pl.ds(k0, TK)], pt[h].astype(BF16),
                          preferred_element_type=F32) for h in heads]
            for h in heads:
                a = jnp.exp(m_old[h] - m_new[h])
                m_s[h:h + 1, :] = m_new[h]
                l_s[h:h + 1, :] = a * l_s[h:h + 1, :] + jnp.sum(pt[h], 0, keepdims=True)
                acc_s[DH * h:DH * (h + 1), :] = a * acc_s[DH * h:DH * (h + 1), :] + pv[h]

        def pair_step(jj, k_prev):
            ka = pl.multiple_of(2 * jj * TK, TK)
            kb = pl.multiple_of(ka + TK, TK)
            scores(1, ka, False)
            consume(0, k_prev)
            scores(0, kb, False)
            consume(1, ka)
            return kb

        scores(0, q0, True)
        k_prev = lax.fori_loop(0, qi // 2, pair_step, q0)

        @pl.when(qi % 2 == 1)
        def _():
            k_odd = pl.multiple_of((qi - 1) * TK, TK)
            scores(1, k_odd, False)
            consume(0, k_prev)
            consume(1, k_odd)

        @pl.when(qi % 2 == 0)
        def _():
            consume(0, k_prev)
        for p in range(H // 2):
            o_pair = jnp.concatenate([acc_s[DH * h:DH * (h + 1), :] / l_s[h:h + 1, :] for h in (2 * p, 2 * p + 1)], 0)
            o_ref[pl.ds(q0, TQ), 2 * DH * p:2 * DH * (p + 1)] = o_pair.T.astype(BF16)
        return 0

    lax.fori_loop(0, L // TQ, q_block, 0)


def _fox_attention(qt, k, vt, f_col, f_row):
    B, H, L = f_row.shape
    assert L % FOX_TILE == 0, L
    seq = pl.BlockSpec((L, HALF), lambda b: (b, 0))
    seq_t = pl.BlockSpec((HALF, L), lambda b: (0, b))
    return pl.pallas_call(
        functools.partial(_fox_kernel, L=L),
        grid=(B,),
        in_specs=[seq_t, seq, seq_t, pl.BlockSpec((L, 128), lambda b: (b, 0)),
                  pl.BlockSpec((None, H, L), lambda b: (b, 0, 0))],
        out_specs=seq,
        out_shape=jax.ShapeDtypeStruct((B * L, HALF), BF16),
        scratch_shapes=[pltpu.VMEM((H, FOX_AUG, L), BF16), pltpu.VMEM((L, FOX_AUG * H // 2), BF16),
                        pltpu.VMEM((H, FOX_TILE), F32), pltpu.VMEM((H, FOX_TILE), F32),
                        pltpu.VMEM((HALF, FOX_TILE), F32), pltpu.VMEM((2, H, FOX_TILE, FOX_TILE), F32)],
        compiler_params=_cparams("parallel"),
        name="fox_attention",
    )(qt, k, vt, f_col, f_row)


def _s5_matrices(a_re, a_im, b_re, b_im, c_re, c_im, d, log_dt):
    G, P, Cg, LC = S5_GROUPS, S5_STATE, S5_GROUP_CH, S5_CHUNK
    a_re, a_im, b_re, b_im, c_re, c_im = (t.astype(F32) for t in (a_re, a_im, b_re, b_im, c_re, c_im))
    dt = jnp.exp(log_dt.astype(F32))[:, None]
    mag = jnp.exp(a_re * dt)
    lam_re, lam_im = mag * jnp.cos(a_im * dt), mag * jnp.sin(a_im * dt)
    den = a_re ** 2 + a_im ** 2
    zr = ((lam_re - 1.0) * a_re + lam_im * a_im) / den
    zi = (lam_im * a_re - (lam_re - 1.0) * a_im) / den
    bb_re = zr[..., None] * b_re - zi[..., None] * b_im
    bb_im = zr[..., None] * b_im + zi[..., None] * b_re
    n = jnp.arange(LC + 1, dtype=F32)[:, None, None]
    pw_mag = jnp.exp(n * (a_re * dt)[None])
    pr, pi = pw_mag * jnp.cos(n * (a_im * dt)[None]), pw_mag * jnp.sin(n * (a_im * dt)[None])
    ein = functools.partial(jnp.einsum, precision=HIGHEST)
    cr = c_re[None] * pr[:, :, None, :] - c_im[None] * pi[:, :, None, :]
    ci = -(c_re[None] * pi[:, :, None, :] + c_im[None] * pr[:, :, None, :])
    kn = ein('ngcp,gpd->ngcd', cr[:LC], bb_re) + ein('ngcp,gpd->ngcd', ci[:LC], bb_im)
    s_idx = jnp.arange(LC)
    lag = s_idx[None, :] - s_idx[:, None]
    kt = jnp.where((lag >= 0)[:, :, None, None, None],
                   kn[jnp.clip(lag, 0, LC - 1)], 0.0)
    dmat = (lag == 0)[:, :, None, None, None] * (d.astype(F32).reshape(1, 1, G, Cg, 1)
                                                 * jnp.eye(Cg, dtype=F32)[None, None, None])
    NS, GS = S5_SLABS, S5_SLAB_GROUPS
    k6 = (kt + dmat).reshape(LC, LC, NS, GS, Cg, Cg)
    kmat = k6.transpose(2, 0, 3, 5, 1, 4).reshape(NS, LC * 128, LC * Cg)
    rr, ri = pr[LC - 1 - s_idx], pi[LC - 1 - s_idx]
    e_re = rr[..., None] * bb_re[None] - ri[..., None] * bb_im[None]
    e_im = rr[..., None] * bb_im[None] + ri[..., None] * bb_re[None]
    e6 = jnp.stack([e_re, e_im]).reshape(2, LC, NS, GS, P, Cg)
    emat = e6.transpose(2, 1, 3, 5, 0, 4).reshape(NS, LC * 128, 2 * P)
    f6 = jnp.stack([cr[1:], ci[1:]]).reshape(2, LC, NS, GS, Cg, P)
    fmat = f6.transpose(2, 0, 3, 5, 1, 4).reshape(NS, 2 * GS * P, LC * Cg)
    lr, li = pr[LC].reshape(NS, 1, GS * P), pi[LC].reshape(NS, 1, GS * P)
    lam_a = jnp.concatenate([lr, lr], axis=-1)
    lam_b = jnp.concatenate([-li, li], axis=-1)
    return kmat.astype(BF16), emat.astype(BF16), fmat.astype(BF16), lam_a, lam_b


def _s5_expand(src_ref, dst_ref, col_of, row_group, col_group):
    n_src, n_dst = src_ref.shape[1], dst_ref.shape[1]
    sr, sc = _iota2((n_src, n_dst))
    select = jnp.where(sr == col_of(sc), 1.0, 0.0).astype(BF16)
    step = 256
    for i in range(src_ref.shape[0] // step):
        rs = slice(i * step, (i + 1) * step)
        r, c = _iota2((step, n_dst))
        same = row_group(r + i * step) == col_group(c)
        full = jnp.dot(src_ref[rs, :], select, preferred_element_type=F32)
        dst_ref[rs, :] = jnp.where(same, full, 0.0).astype(BF16)


def _s5_kernel(u_ref, kc_ref, ec_ref, fc_ref, la_ref, lb_ref, y_ref, k_ref, e_ref, f_ref, e_s, hs_s, h_s, *, B):
    @pl.when(pl.program_id(1) == 0)
    def _():
        h_s[...] = jnp.zeros_like(h_s)
        gs, cg, p = S5_SLAB_GROUPS, S5_GROUP_CH, S5_STATE
        lg = lambda n: n.bit_length() - 1
        lane_group = lambda i: (i >> lg(cg)) & (gs - 1)
        state_group = lambda i: (i >> lg(p)) & (gs - 1)
        frame_ch = lambda c: ((c >> lg(gs * cg)) << lg(cg)) | (c & (cg - 1))
        part_state = lambda c: ((c >> lg(gs * p)) << lg(p)) | (c & (p - 1))
        _s5_expand(kc_ref, k_ref, frame_ch, lane_group, lane_group)
        _s5_expand(ec_ref, e_ref, part_state, lane_group, state_group)
        _s5_expand(fc_ref, f_ref, frame_ch, state_group, lane_group)

    u = u_ref[...]
    e_s[...] = jnp.dot(u, e_ref[...], preferred_element_type=F32)
    la, lb = la_ref[...], lb_ref[...]
    half = h_s.shape[1] // 2
    h = h_s[...]
    for kc in range(u.shape[0] // B):
        rs = slice(kc * B, (kc + 1) * B)
        hs_s[rs, :] = h
        swapped = jnp.concatenate([h[:, half:], h[:, :half]], axis=1)
        h = la * h + lb * swapped + e_s[rs, :]
    h_s[...] = h
    y_ref[...] = (jnp.dot(u, k_ref[...], preferred_element_type=F32)
                  + jnp.dot(hs_s[...].astype(BF16), f_ref[...], preferred_element_type=F32)).astype(BF16)


def _s5(u4, mats, B, L):
    kmat, emat, fmat, lam_a, lam_b = mats
    NS, LC = S5_SLABS, S5_CHUNK
    NK = L // LC
    assert L % LC == 0 and (NK <= S5_TILE_CHUNKS or NK % S5_TILE_CHUNKS == 0) and B % 8 == 0, (B, L)
    R, W, S2 = NK * B, LC * 128, 2 * S5_SLAB_GROUPS * S5_STATE
    ug = u4.reshape(NS, B, NK, W).transpose(0, 2, 1, 3).reshape(NS, R, W)
    TR = B * min(NK, S5_TILE_CHUNKS)
    rows = pl.BlockSpec((None, TR, W), lambda s, r: (s, r, 0))
    per_s = lambda a: pl.BlockSpec((None,) + a.shape[1:], lambda s, r: (s, 0, 0), pipeline_mode=pl.Buffered(1))
    y = pl.pallas_call(
        functools.partial(_s5_kernel, B=B),
        grid=(NS, R // TR),
        in_specs=[rows] + [per_s(a) for a in (kmat, emat, fmat, lam_a, lam_b)],
        out_specs=rows,
        out_shape=jax.ShapeDtypeStruct((NS, R, W), BF16),
        scratch_shapes=[pltpu.VMEM((W, W), BF16), pltpu.VMEM((W, S2), BF16), pltpu.VMEM((S2, W), BF16),
                        pltpu.VMEM((TR, S2), F32), pltpu.VMEM((TR, S2), F32), pltpu.VMEM((B, S2), F32)],
        compiler_params=_cparams("parallel", "arbitrary"),
        name="s5",
    )(ug, kmat, emat, fmat, lam_a, lam_b)
    return y.reshape(NS, NK, B, W).transpose(0, 2, 1, 3).reshape(NS, B * L, 128)


def _post_kernel(a_ref, ah_ref, b_ref, bh_ref, x_ref, xh_ref, wt_ref, wb_ref, g1_ref, beta1_ref, *rest,
                 even, tiles_per_seq, alpha):
    if even:
        wg_ref, bg_ref, *rest = rest
    wu_ref, cw_ref, cb_ref, wd_ref, g2_ref, beta2_ref, o_ref = rest
    tm = x_ref.shape[0]
    first = (pl.program_id(0) % tiles_per_seq) == 0
    a = jnp.concatenate([ah_ref[...], a_ref[...]], axis=0)
    x_in = jnp.concatenate([xh_ref[...], x_ref[...]], axis=0)
    if even:
        y = jnp.concatenate([jnp.concatenate([bh_ref[s], b_ref[s]], axis=0) for s in range(S5_SLABS)], axis=1)
        z = jax.nn.gelu(y.astype(F32))
        second = z * jax.nn.sigmoid(_mm(z, wg_ref[...]) + bg_ref[...])
    else:
        second = jnp.concatenate([bh_ref[...], b_ref[...]], axis=0)
    mix = _mm(a, wt_ref[...]) + _mm(second, wb_ref[...])
    x1e = _layer_norm(alpha * x_in + mix, g1_ref[...], beta1_ref[...])
    row = lax.broadcasted_iota(jnp.int32, (HALO + tm, 1), 0)
    x1e = jnp.where((row < HALO) & first, 0.0, x1e)
    x1 = x1e[HALO:]
    xe = x1e.astype(BF16)
    xb = xe[HALO:]

    def up(c):
        ue = jnp.dot(xe, wu_ref[:, c * FFN_TILE:(c + 1) * FFN_TILE], preferred_element_type=F32)
        gate = jnp.dot(xb, wu_ref[:, D_FF + c * FFN_TILE:D_FF + (c + 1) * FFN_TILE], preferred_element_type=F32)
        return ue, gate

    acc = jnp.zeros((tm, D_MODEL), F32)
    n_tiles = D_FF // FFN_TILE
    nxt = up(0)
    for c in range(n_tiles):
        cs = slice(c * FFN_TILE, (c + 1) * FFN_TILE)
        ue, gate = nxt
        if c + 1 < n_tiles:
            nxt = up(c + 1)
        cw = cw_ref[:, cs]
        u = (cw[2:3] * ue[HALO:] + cw[1:2] * _shift_rows(ue, 1)[HALO:] + cw[0:1] * _shift_rows(ue, 2)[HALO:]
             + cb_ref[:, cs])
        acc = acc + _mm(jax.nn.gelu(u) * gate, wd_ref[cs, :])
    o_ref[...] = _layer_norm(alpha * x1 + acc, g2_ref[...], beta2_ref[...])


def _post(a, b, x, L, w_out, ln1_g, ln1_b, w_up, conv_w, conv_b, w_down, ln2_g, ln2_b, alpha, glu=None, tm=1024):
    T = x.shape[0]
    tm = min(tm, L)
    assert L % tm == 0 and tm % HALO == 0, (L, tm)
    wo = w_out.astype(BF16)
    row1 = lambda v, n: v.reshape(1, n).astype(F32)
    slab = lambda rows, scale: pl.BlockSpec((S5_SLABS, rows, 128), scale)
    b_specs = [_rows(tm, HALF), _halo_rows(tm, HALF)]
    if glu is not None:
        b_specs = [slab(tm, lambda i: (0, i, 0)),
                   slab(HALO, lambda i: (0, jnp.maximum(i * (tm // HALO) - 1, 0), 0))]
    args = [a, a, b, b, x, x, wo[:HALF], wo[HALF:], row1(ln1_g, D_MODEL), row1(ln1_b, D_MODEL)]
    specs = ([_rows(tm, HALF), _halo_rows(tm, HALF)] + b_specs + [_rows(tm, D_MODEL), _halo_rows(tm, D_MODEL)]
             + [_resident(t.shape) for t in args[6:]])
    tail = [w_up.astype(BF16), conv_w.astype(F32), row1(conv_b, D_FF), w_down.astype(BF16),
            row1(ln2_g, D_MODEL), row1(ln2_b, D_MODEL)]
    if glu is not None:
        tail = [glu[0].astype(BF16), row1(glu[1], HALF)] + tail
    return pl.pallas_call(
        functools.partial(_post_kernel, even=glu is not None, tiles_per_seq=L // tm, alpha=alpha),
        grid=(T // tm,),
        in_specs=specs + [_resident(t.shape) for t in tail],
        out_specs=_rows(tm, D_MODEL),
        out_shape=jax.ShapeDtypeStruct((T, D_MODEL), F32),
        compiler_params=_cparams("parallel"),
        name="post",
    )(*args, *tail)


_LORA_PAD = (128, 128, 256)
_RWKV_PAD = 3 * HALF + sum(_LORA_PAD)


def _odd_pre_kernel(x_ref, xh_ref, wrk_ref, mu_ref, w2_ref, a2_ref, g2_ref, vec_ref, bd_ref,
                    wqk_ref, cw_ref, cb_ref, wv_ref, wz_ref, wif_ref, gb_ref,
                    r_ref, k_ref, v_ref, lw_ref, kk_ref, ka_ref, g_ref, bo_ref,
                    mq_ref, mk_ref, mv_ref, mz_ref, gate_ref, *, tiles_per_seq):
    tm = x_ref.shape[0]
    first = (pl.program_id(0) % tiles_per_seq) == 0
    xh = jnp.where(first, 0.0, xh_ref[...])
    xe = jnp.concatenate([xh, x_ref[...]], axis=0).astype(BF16)
    xb = xe[HALO:]

    pe = jnp.dot(xe, wrk_ref[...], preferred_element_type=F32)
    qke = jnp.dot(xe, wqk_ref[...], preferred_element_type=F32)
    mv_ref[...] = jnp.dot(xb, wv_ref[...], preferred_element_type=F32).astype(BF16)
    mz_ref[...] = jnp.dot(xb, wz_ref[...], preferred_element_type=F32).astype(mz_ref.dtype)
    pre = jnp.dot(xb, wif_ref[...], preferred_element_type=F32)[:, :2 * MLSTM_HEADS] + gb_ref[...]

    cur, prev = pe[HALO:], _shift_rows(pe, 1)[HALO:]
    p = cur + (prev - cur) * mu_ref[...]
    r, k, v = p[:, :HALF], p[:, HALF:2 * HALF], p[:, 2 * HALF:3 * HALF]
    o0 = 3 * HALF
    wd = p[:, o0:o0 + _LORA_PAD[0]]
    ad = p[:, o0 + _LORA_PAD[0]:o0 + _LORA_PAD[0] + _LORA_PAD[1]]
    gd = p[:, o0 + _LORA_PAD[0] + _LORA_PAD[1]:]
    w0, a0, k_k, k_a, r_k = (vec_ref[i:i + 1, :] for i in range(5))
    wlog = -jax.nn.softplus(-(w0 + _mm(jnp.tanh(wd), w2_ref[...]))) - 0.5
    lw_ref[...] = -jnp.exp(wlog)
    a = jax.nn.sigmoid(a0 + _mm(ad, a2_ref[...]))
    g_ref[...] = _mm(jax.nn.sigmoid(gd), g2_ref[...]).astype(g_ref.dtype)
    kk = k * k_k
    ss = _mm_exact_rhs(kk * kk, bd_ref[...])
    kk = kk * lax.rsqrt(jnp.maximum(ss, 1e-24))
    kmod = k * (1.0 + (a - 1.0) * k_a)
    bo_ref[...] = (_mm_exact_rhs(r * kmod * r_k, bd_ref[...]) * v).astype(bo_ref.dtype)
    r_ref[...] = r.astype(r_ref.dtype)
    k_ref[...] = kmod.astype(k_ref.dtype)
    v_ref[...] = v.astype(v_ref.dtype)
    kk_ref[...] = kk.astype(kk_ref.dtype)
    ka_ref[...] = (kk * a).astype(ka_ref.dtype)

    cw = cw_ref[...]
    qk = cb_ref[...]
    for j in range(MLSTM_CONV):
        qk = qk + cw[j:j + 1] * _shift_rows(qke, MLSTM_CONV - 1 - j)[HALO:]
    qk = jax.nn.silu(qk)
    mq_ref[...] = qk[:, :HALF].astype(mq_ref.dtype)
    mk_ref[...] = (qk[:, HALF:] * (MLSTM_DH ** -0.5)).astype(BF16)
    is_i = lax.broadcasted_iota(jnp.int32, pre.shape, 1) < MLSTM_HEADS
    gate_ref[...] = jnp.where(is_i, pre, jax.nn.log_sigmoid(pre))


def _head_block_ones(width, head):
    idx = jnp.arange(width) // head
    return (idx[:, None] == idx[None, :]).astype(BF16)


def _odd_pre(x, L, w_in, mu, w0, w2, a0, a2, g2, k_k, k_a, r_k, conv_w, conv_b, ib, fb, tm=512):
    T = x.shape[0]
    tm = min(tm, L)
    assert L % tm == 0 and tm % HALO == 0, (L, tm)
    wb = w_in.astype(BF16)
    o = 3 * HALF
    sizes = (RWKV_DECAY_LORA, RWKV_AAA_LORA, RWKV_GATE_LORA)

    def pad_lora(m, axis):
        parts, s = [], o
        for sz, pd in zip(sizes, _LORA_PAD):
            piece = lax.slice_in_dim(m, s, s + sz, axis=axis)
            widths = [(0, 0)] * m.ndim
            widths[axis] = (0, pd - sz)
            parts.append(jnp.pad(piece, widths))
            s += sz
        return jnp.concatenate([lax.slice_in_dim(m, 0, o, axis=axis)] + parts, axis=axis)

    rwkv_proj = o + sum(sizes)
    wrk = pad_lora(wb[:, :rwkv_proj], 1)
    mu_p = pad_lora(mu.astype(F32).reshape(1, -1), 1)
    padr = lambda m, rows: jnp.pad(m.astype(BF16), ((0, rows - m.shape[0]), (0, 0)))
    w2p, a2p, g2p = padr(w2, _LORA_PAD[0]), padr(a2, _LORA_PAD[1]), padr(g2, _LORA_PAD[2])
    vecs = jnp.stack([w0, a0, k_k, k_a, r_k.reshape(HALF)]).astype(F32)
    vecs = jnp.pad(vecs, ((0, 8 - vecs.shape[0]), (0, 0)))
    bd = _head_block_ones(HALF, RWKV_N)
    wm = wb[:, rwkv_proj:]
    wqk, wmv, wmz = wm[:, :2 * HALF], wm[:, 2 * HALF:3 * HALF], wm[:, 3 * HALF:4 * HALF]
    wif = jnp.pad(wm[:, 4 * HALF:], ((0, 0), (0, 128 - 2 * MLSTM_HEADS)))
    gbias = jnp.concatenate([ib, fb]).astype(F32).reshape(1, 2 * MLSTM_HEADS)
    args = [x, x, wrk, mu_p, w2p, a2p, g2p, vecs, bd, wqk, conv_w.astype(F32),
            conv_b.astype(F32).reshape(1, 2 * HALF), wmv, wmz, wif, gbias]
    specs = [_rows(tm, D_MODEL), _halo_rows(tm, D_MODEL)] + [_resident(a.shape) for a in args[2:]]
    f_half = jax.ShapeDtypeStruct((T, HALF), F32)
    b_half = jax.ShapeDtypeStruct((T, HALF), BF16)
    out_shape = [b_half] * 3 + [f_half] + [b_half] * 8 + [jax.ShapeDtypeStruct((T, 2 * MLSTM_HEADS), F32)]
    out_specs = [_rows(tm, HALF)] * 12 + [_rows(tm, 2 * MLSTM_HEADS)]
    return pl.pallas_call(
        functools.partial(_odd_pre_kernel, tiles_per_seq=L // tm),
        grid=(T // tm,),
        in_specs=specs,
        out_specs=out_specs,
        out_shape=out_shape,
        compiler_params=_cparams("parallel"),
        name="odd_pre",
    )(*args)


def _rwkv_kernel(r_ref, k_ref, v_ref, lw_ref, kk_ref, ka_ref, g_ref, bo_ref, lng_ref, lnb_ref, o_ref, m_s):
    C, N, H = CHUNK, RWKV_N, RWKV_HEADS
    rows = range(r_ref.shape[0])
    units = [(b, h) for b in rows for h in range(H)]

    @pl.when(pl.program_id(1) == 0)
    def _():
        m_s[...] = jnp.zeros_like(m_s)

    ri, ci = _iota2((C, C))
    incl, strict, eye = ci <= ri, ci < ri, ci == ri
    ri2, ci2 = _iota2((C, 2 * C))
    incl2 = (ci2 & (C - 1)) <= ri2
    zeros = jnp.zeros((C, N), F32)
    mm = functools.partial(_mmp, passes=RWKV_PASSES)
    al, rt, bt, kt, bp, kp, gam, v = ({} for _ in range(8))
    for b in rows:
        lw = lw_ref[b]
        cs = _mm_exact_lhs(incl.astype(BF16), lw)
        cend = cs[C - 1:C, :]
        e_neg = jnp.exp(-cs)
        e_rem = jnp.exp(cend - cs)
        r, k, vv, kk, ka = (t[b].astype(F32) for t in (r_ref, k_ref, v_ref, kk_ref, ka_ref))
        al_b = -kk * jnp.exp(cs - lw)
        rt_b = r * jnp.exp(cs)
        gam_b = jnp.exp(cend)
        bt_b, kt_b, bp_b, kp_b = ka * e_neg, k * e_neg, ka * e_rem, k * e_rem
        for h in range(H):
            s = slice(N * h, N * (h + 1))
            u = (b, h)
            al[u], rt[u], v[u], gam[u] = al_b[:, s], rt_b[:, s], vv[:, s], gam_b[:, s]
            bt[u], kt[u], bp[u], kp[u] = bt_b[:, s], kt_b[:, s], bp_b[:, s], kp_b[:, s]
    pm = {u: mm(jnp.concatenate([al[u], rt[u]], 0), jnp.concatenate([bt[u], kt[u]], 0), 1, 1) for u in units}
    a_ab = {u: jnp.where(strict, pm[u][:C, :C], 0.0) for u in units}
    a_ak = {u: jnp.where(strict, pm[u][:C, C:], 0.0) for u in units}
    a_r = {u: jnp.where(incl2, pm[u][C:, :], 0.0) for u in units}
    w = {u: jnp.concatenate([al[u], mm(a_ak[u], v[u])], 1) for u in units}
    npow = a_ab
    levels = int(math.log2(C))
    for lvl in range(levels):
        if lvl < levels - 1:
            y = {u: mm(npow[u], jnp.concatenate([w[u], npow[u]], 1)) for u in units}
            w = {u: w[u] + y[u][:, :2 * N] for u in units}
            npow = {u: y[u][:, 2 * N:] for u in units}
        else:
            w = {u: w[u] + mm(npow[u], w[u]) for u in units}
    zv = {u: jnp.concatenate([zeros, v[u]], 1) for u in units}
    qt = {u: mm(a_r[u], jnp.concatenate([w[u], zv[u]], 0)) for u in units}
    qb = {u: mm(bp[u], w[u], 0, 0) + mm(kp[u], zv[u], 0, 0) for u in units}
    m0 = {u: m_s[u[0], u[1]] for u in units}
    o_h = {u: mm(rt[u] + qt[u][:, :N], m0[u]) + qt[u][:, N:] for u in units}
    m_new = {u: _mmp(qb[u][:, :N] + jnp.where(eye, gam[u], 0.0), m0[u], passes=RWKV_STATE_PASSES) + qb[u][:, N:]
             for u in units}
    for b in rows:
        outs = []
        for h in range(H):
            u = (b, h)
            m_s[b, h] = m_new[u]
            om = jnp.mean(o_h[u], -1, keepdims=True)
            d = o_h[u] - om
            ov = jnp.mean(d * d, -1, keepdims=True)
            outs.append(d * lax.rsqrt(ov + RWKV_GN_EPS))
        on = jnp.concatenate(outs, 1) * lng_ref[...] + lnb_ref[...]
        o_ref[b] = ((on + bo_ref[b].astype(F32)) * g_ref[b].astype(F32)).astype(BF16)


def _rwkv(r, k, v, lw, kk, ka, g, bonus, ln_g, ln_b, B, L):
    NC = L // CHUNK
    assert L % CHUNK == 0, L
    RB = SEQ_ROWS if B % SEQ_ROWS == 0 else 1
    seq3 = lambda t: t.reshape(B, L, HALF)
    blk = pl.BlockSpec((RB, CHUNK, HALF), lambda b, c: (b, c, 0))
    vec = pl.BlockSpec((1, HALF), lambda b, c: (0, 0))
    out = pl.pallas_call(
        _rwkv_kernel,
        grid=(B // RB, NC),
        in_specs=[blk] * 8 + [vec, vec],
        out_specs=blk,
        out_shape=jax.ShapeDtypeStruct((B, L, HALF), BF16),
        scratch_shapes=[pltpu.VMEM((RB, RWKV_HEADS, RWKV_N, RWKV_N), F32)],
        compiler_params=_cparams("parallel", "arbitrary"),
        name="rwkv7",
    )(*(seq3(t) for t in (r, k, v, lw, kk, ka, g, bonus)),
      ln_g.reshape(1, HALF).astype(F32), ln_b.reshape(1, HALF).astype(F32))
    return out.reshape(B * L, HALF)


def _mlstm_kernel(q_ref, k_ref, v_ref, z_ref, gc_ref, gr_ref, lng_ref, skip_ref, o_ref, c_s, m_s):
    C, H, DH = CHUNK, MLSTM_HEADS, MLSTM_DH
    rows = range(q_ref.shape[0])
    units = [(b, h) for b in rows for h in range(H)]

    @pl.when(pl.program_id(1) == 0)
    def _():
        c_s[...] = jnp.zeros_like(c_s)
        m_s[...] = jnp.zeros_like(m_s)

    ri, ci = _iota2((C, C))
    incl = ci <= ri
    lower, upper = incl.astype(BF16), (ri <= ci).astype(BF16)
    lane = lax.broadcasted_iota(jnp.int32, (1, DH), 1)
    ones_blk = jnp.broadcast_to(jnp.where(lane == 0, 1.0, 0.0).astype(BF16), (C, DH))
    q_all, qb, kh, vaug, i_col, i_row, b_col, b_row, m_prev = ({} for _ in range(9))
    for b in rows:
        gc, gr = gc_ref[b], gr_ref[b]
        b_cols = _mm_exact_lhs(lower, gc)
        b_rows = _mm_exact_rhs(gr, upper)
        q_all[b] = q_ref[b].astype(F32)
        for h in range(H):
            u, s = (b, h), slice(DH * h, DH * (h + 1))
            qb[u], kh[u] = q_ref[b, :, s], k_ref[b, :, s]
            vaug[u] = jnp.concatenate([v_ref[b, :, s], ones_blk], 1)
            i_col[u], i_row[u] = gc[:, h:h + 1], gr[h:h + 1, :]
            b_col[u], b_row[u] = b_cols[:, H + h:H + h + 1], b_rows[H + h:H + h + 1, :]
            m_prev[u] = m_s[b, h:h + 1, 0:1]
    c_old = {u: c_s[u[0], u[1]] for u in units}
    qk = {u: _dg(qb[u], kh[u], 1, 1) for u in units}
    qc = {u: jnp.dot(qb[u], c_old[u].astype(BF16), preferred_element_type=F32) for u in units}
    dmat = {u: jnp.where(incl, b_col[u] - b_row[u] + i_row[u], NEG_BIG) for u in units}
    inter = {u: b_col[u] + m_prev[u] for u in units}
    m_t = {u: jnp.maximum(inter[u], jnp.max(dmat[u], -1, keepdims=True)) for u in units}
    s = {u: (qk[u] * jnp.exp(dmat[u] - m_t[u])).astype(BF16) for u in units}
    sv = {u: jnp.dot(s[u], vaug[u], preferred_element_type=F32) for u in units}
    b_end = {u: b_col[u][C - 1:C, :] for u in units}
    m_new = {u: jnp.maximum(b_end[u] + m_prev[u], jnp.max(b_end[u] - b_row[u] + i_row[u], -1, keepdims=True))
             for u in units}
    kw = {u: (jnp.exp(b_end[u] - b_col[u] + i_col[u] - m_new[u]) * kh[u].astype(F32)).astype(BF16) for u in units}
    kv = {u: _dg(kw[u], vaug[u], 0, 0) for u in units}
    for b in rows:
        outs = []
        for h in range(H):
            u = (b, h)
            c_s[b, h] = jnp.exp(b_end[u] + m_prev[u] - m_new[u]) * c_old[u] + kv[u]
            m_s[b, h:h + 1, :] = jnp.broadcast_to(m_new[u], (1, 128))
            nd = jnp.exp(inter[u] - m_t[u]) * qc[u] + sv[u]
            hid = nd[:, :DH] / jnp.maximum(jnp.abs(nd[:, DH:DH + 1]), jnp.exp(-m_t[u]))
            hm = jnp.mean(hid, -1, keepdims=True)
            d = hid - hm
            hv = jnp.mean(d * d, -1, keepdims=True)
            outs.append(d * lax.rsqrt(hv + LN_EPS))
        hn = jnp.concatenate(outs, 1) * lng_ref[...]
        o_ref[b] = ((hn + skip_ref[...] * q_all[b]) * jax.nn.silu(z_ref[b].astype(F32))).astype(BF16)


def _mlstm(q, k, v, z, gates, ln_g, skip, B, L):
    NC, H2 = L // CHUNK, 2 * MLSTM_HEADS
    assert L % CHUNK == 0, L
    RB = MLSTM_SEQ_ROWS if B % MLSTM_SEQ_ROWS == 0 else 1
    seq3 = lambda t: t.reshape(B, L, t.shape[-1])
    blk = pl.BlockSpec((RB, CHUNK, HALF), lambda b, c: (b, c, 0))
    vec = pl.BlockSpec((1, HALF), lambda b, c: (0, 0))
    g_rows = gates.reshape(B, NC, CHUNK, H2).transpose(0, 1, 3, 2)
    out = pl.pallas_call(
        _mlstm_kernel,
        grid=(B // RB, NC),
        in_specs=[blk, blk, blk, blk, pl.BlockSpec((RB, CHUNK, H2), lambda b, c: (b, c, 0)),
                  pl.BlockSpec((RB, None, H2, CHUNK), lambda b, c: (b, c, 0, 0)), vec, vec],
        out_specs=blk,
        out_shape=jax.ShapeDtypeStruct((B, L, HALF), BF16),
        scratch_shapes=[pltpu.VMEM((RB, MLSTM_HEADS, MLSTM_DH, 2 * MLSTM_DH), F32),
                        pltpu.VMEM((RB, 8, 128), F32)],
        compiler_params=_cparams("parallel", "arbitrary"),
        name="mlstm",
    )(seq3(q), seq3(k), seq3(v), seq3(z), seq3(gates), g_rows,
      ln_g.reshape(1, HALF).astype(F32), skip.reshape(1, HALF).astype(F32))
    return out.reshape(B * L, HALF)


def kernel(x, ev_w_in, ev_fox_fb, ev_s5_a_re, ev_s5_a_im, ev_s5_b_re, ev_s5_b_im, ev_s5_c_re, ev_s5_c_im, ev_s5_d, ev_s5_log_dt, ev_s5_w_glu, ev_s5_b_glu, ev_w_out, od_w_in, od_rwkv_mu, od_rwkv_w0, od_rwkv_w2, od_rwkv_a0, od_rwkv_a2, od_rwkv_g2, od_rwkv_k_k, od_rwkv_k_a, od_rwkv_r_k, od_rwkv_ln_g, od_rwkv_ln_b, od_mlstm_conv_w, od_mlstm_conv_b, od_mlstm_ib, od_mlstm_fb, od_mlstm_ln_g, od_mlstm_skip, od_w_out, ln1_g, ln1_b, ffn_w_up, ffn_conv_w, ffn_conv_b, ffn_w_down, ln2_g, ln2_b):
    B, L, _ = x.shape
    depth = ln1_g.shape[0]
    alpha = float((2 * depth) ** 0.25)
    h = x.reshape(B * L, D_MODEL).astype(F32)
    for layer in range(depth):
        i = layer // 2
        if layer % 2 == 0:
            qt, k, vt, fg, u = _even_pre(h, ev_w_in[i])
            f_row = _fox_gate(fg.reshape(B, L, FOX_HEADS).transpose(0, 2, 1), ev_fox_fb[i].astype(F32))
            f_col = jnp.pad(f_row.transpose(0, 2, 1).reshape(B * L, FOX_HEADS), ((0, 0), (0, 128 - FOX_HEADS)))
            fox = _fox_attention(qt, k, vt, f_col, f_row)
            mats = _s5_matrices(ev_s5_a_re[i], ev_s5_a_im[i], ev_s5_b_re[i], ev_s5_b_im[i],
                                ev_s5_c_re[i], ev_s5_c_im[i], ev_s5_d[i], ev_s5_log_dt[i])
            y = _s5(u, mats, B, L)
            mixed, glu, w_out = (fox, y), (ev_s5_w_glu[i], ev_s5_b_glu[i]), ev_w_out[i]
        else:
            (r, k, v, lw, kk, ka, g, bonus, mq, mk, mv, mz, gates) = _odd_pre(
                h, L, od_w_in[i], od_rwkv_mu[i], od_rwkv_w0[i], od_rwkv_w2[i], od_rwkv_a0[i], od_rwkv_a2[i],
                od_rwkv_g2[i], od_rwkv_k_k[i], od_rwkv_k_a[i], od_rwkv_r_k[i], od_mlstm_conv_w[i],
                od_mlstm_conv_b[i], od_mlstm_ib[i], od_mlstm_fb[i])
            c = _rwkv(r, k, v, lw, kk, ka, g, bonus, od_rwkv_ln_g[i], od_rwkv_ln_b[i], B, L)
            dm = _mlstm(mq, mk, mv, mz, gates, od_mlstm_ln_g[i], od_mlstm_skip[i], B, L)
            mixed, glu, w_out = (c, dm), None, od_w_out[i]
        h = _post(mixed[0], mixed[1], h, L, w_out, ln1_g[layer], ln1_b[layer], ffn_w_up[layer], ffn_conv_w[layer],
                  ffn_conv_b[layer], ffn_w_down[layer], ln2_g[layer], ln2_b[layer], alpha, glu=glu)
    return h.reshape(B, L, D_MODEL).astype(x.dtype)
```

```python
import functools
import math

import jax
import jax.numpy as jnp
from jax import lax
from jax.experimental import pallas as pl
from jax.experimental.pallas import tpu as pltpu

F32 = jnp.float32
BF16 = jnp.bfloat16
HIGHEST = lax.Precision.HIGHEST

D_MODEL = 1024
HALF = D_MODEL // 2
FOX_HEADS = 8
FOX_DH = HALF // FOX_HEADS
FOX_TILE = 256
FOX_AUG = 256
S5_GROUPS = 32
S5_GROUP_CH = 16
S5_STATE = 64
S5_CHUNK = 16
S5_SLAB_GROUPS = 128 // S5_GROUP_CH
S5_SLABS = S5_GROUPS // S5_SLAB_GROUPS
S5_TILE_CHUNKS = 16
RWKV_HEADS = 8
RWKV_N = HALF // RWKV_HEADS
RWKV_DECAY_LORA = 64
RWKV_AAA_LORA = 64
RWKV_GATE_LORA = 160
RWKV_GN_EPS = 64e-5
RWKV_PASSES = 1
RWKV_STATE_PASSES = 3
MLSTM_HEADS = 4
MLSTM_DH = HALF // MLSTM_HEADS
MLSTM_CONV = 4
CHUNK = 64
SEQ_ROWS = 4
MLSTM_SEQ_ROWS = 4
D_FF = 2816
FFN_CONV = 3
FFN_TILE = 256
LN_EPS = 1e-5
HALO = 16
NEG_BIG = -1e30
V7X_VMEM_LIMIT_BYTES = 56 * 1024 * 1024


def _cparams(*sem):
    return pltpu.CompilerParams(dimension_semantics=sem, vmem_limit_bytes=V7X_VMEM_LIMIT_BYTES)


def _resident(shape):
    nd = len(shape)
    return pl.BlockSpec(shape, lambda *_: (0,) * nd, pipeline_mode=pl.Buffered(1))


def _rows(tm, width):
    return pl.BlockSpec((tm, width), lambda i: (i, 0))


def _halo_rows(tm, width):
    return pl.BlockSpec((HALO, width), lambda i: (jnp.maximum(i * (tm // HALO) - 1, 0), 0))


def _mm(a, b):
    return jnp.dot(a.astype(BF16), b.astype(BF16), preferred_element_type=F32)


def _dg(a, b, ca, cb):
    return lax.dot_general(a, b, (((ca,), (cb,)), ((), ())), preferred_element_type=F32)


def _hi_lo(a):
    hi = a.astype(BF16)
    lo = (a - hi.astype(F32)).astype(BF16)
    return hi, lo


def _mm3(a, b, ca=1, cb=0):
    ah, al = _hi_lo(a)
    bh, bl = _hi_lo(b)
    return _dg(ah, bh, ca, cb) + _dg(ah, bl, ca, cb) + _dg(al, bh, ca, cb)


def _mmp(a, b, ca=1, cb=0, passes=1):
    if passes == 3:
        return _mm3(a, b, ca, cb)
    return _dg(a.astype(BF16), b.astype(BF16), ca, cb)


def _split3(a):
    a1 = a.astype(BF16)
    r1 = a - a1.astype(F32)
    a2 = r1.astype(BF16)
    a3 = (r1 - a2.astype(F32)).astype(BF16)
    return a1, a2, a3


def _mm_exact_rhs(a, b01):
    a1, a2, a3 = _split3(a)
    return (jnp.dot(a1, b01, preferred_element_type=F32) + jnp.dot(a2, b01, preferred_element_type=F32)
            + jnp.dot(a3, b01, preferred_element_type=F32))


def _mm_exact_lhs(a01, b):
    b1, b2, b3 = _split3(b)
    return (jnp.dot(a01, b1, preferred_element_type=F32) + jnp.dot(a01, b2, preferred_element_type=F32)
            + jnp.dot(a01, b3, preferred_element_type=F32))


def _layer_norm(x, g, b):
    mu = jnp.mean(x, -1, keepdims=True)
    d = x - mu
    var = jnp.mean(d * d, -1, keepdims=True)
    return d * lax.rsqrt(var + LN_EPS) * g + b


def _shift_rows(x, n):
    return x if n == 0 else pltpu.roll(x, n, 0)


def _iota2(shape):
    return lax.broadcasted_iota(jnp.int32, shape, 0), lax.broadcasted_iota(jnp.int32, shape, 1)


def _even_pre_kernel(x_ref, wqt_ref, wk_ref, wvt_ref, wf_ref, wu_ref, qt_ref, k_ref, vt_ref, fg_ref, u_ref):
    xb = x_ref[...].astype(BF16)
    qt_ref[...] = (_dg(wqt_ref[...], xb, 1, 1) * (FOX_DH ** -0.5)).astype(BF16)
    vt_ref[...] = _dg(wvt_ref[...], xb, 1, 1).astype(BF16)
    k_ref[...] = jnp.dot(xb, wk_ref[...], preferred_element_type=F32).astype(BF16)
    u = jnp.dot(xb, wu_ref[...], preferred_element_type=F32).astype(BF16)
    for s in range(S5_SLABS):
        u_ref[s] = u[:, 128 * s:128 * (s + 1)]
    fg_ref[...] = jnp.dot(xb, wf_ref[...], preferred_element_type=F32)[:, :FOX_HEADS]


def _even_pre(x, w_in, tm=512):
    T = x.shape[0]
    wb = w_in.astype(BF16)
    wqt, wk, wvt = wb[:, :HALF].T, wb[:, HALF:2 * HALF], wb[:, 2 * HALF:3 * HALF].T
    wf = jnp.pad(wb[:, 3 * HALF:3 * HALF + FOX_HEADS], ((0, 0), (0, 128 - FOX_HEADS)))
    wu = wb[:, 3 * HALF + FOX_HEADS:]
    half_out = jax.ShapeDtypeStruct((T, HALF), BF16)
    half_t = jax.ShapeDtypeStruct((HALF, T), BF16)
    cols = pl.BlockSpec((HALF, tm), lambda i: (0, i))
    return pl.pallas_call(
        _even_pre_kernel,
        grid=(T // tm,),
        in_specs=[_rows(tm, D_MODEL), _resident(wqt.shape), _resident(wk.shape), _resident(wvt.shape),
                  _resident(wf.shape), _resident(wu.shape)],
        out_specs=[cols, _rows(tm, HALF), cols, _rows(tm, FOX_HEADS),
                   pl.BlockSpec((S5_SLABS, tm, 128), lambda i: (0, i, 0))],
        out_shape=[half_t, half_out, half_t, jax.ShapeDtypeStruct((T, FOX_HEADS), F32),
                   jax.ShapeDtypeStruct((S5_SLABS, T, 128), BF16)],
        compiler_params=_cparams("parallel"),
        name="even_pre",
    )(x, wqt, wk, wvt, wf, wu)


def _fox_gate_kernel(fg_ref, fb_ref, o_ref, *, L):
    ls = jax.nn.log_sigmoid(fg_ref[...] + fb_ref[...])
    r, c = _iota2((128, 128))
    tri = (r <= c).astype(BF16)
    carry = jnp.zeros((FOX_HEADS, 1), F32)
    for j in range(L // 128):
        cum = _mm_exact_rhs(ls[:, j * 128:(j + 1) * 128], tri) + carry
        o_ref[:, j * 128:(j + 1) * 128] = cum
        carry = cum[:, 127:128]


def _fox_gate(fg_t, fb):
    B, H, L = fg_t.shape
    return pl.pallas_call(
        functools.partial(_fox_gate_kernel, L=L),
        grid=(B,),
        in_specs=[pl.BlockSpec((None, H, L), lambda b: (b, 0, 0)), _resident((H, 1))],
        out_specs=pl.BlockSpec((None, H, L), lambda b: (b, 0, 0)),
        out_shape=jax.ShapeDtypeStruct((B, H, L), F32),
        compiler_params=_cparams("parallel"),
        name="fox_gate",
    )(fg_t, fb.reshape(H, 1))


def _fox_kernel(qt_ref, k_ref, vt_ref, fc_ref, fr_ref, o_ref, qa_s, ka_s, m_s, l_s, acc_s, st_s, *, L):
    TQ = TK = FOX_TILE
    H, DH, KA = FOX_HEADS, FOX_DH, FOX_AUG
    heads = range(H)

    t1, t2, t3 = (t.astype(F32) for t in _split3(fr_ref[...]))
    r16 = lax.broadcasted_iota(jnp.int32, (16, 1), 0)
    upper = lax.broadcasted_iota(jnp.int32, (2 * DH, 1), 0) < DH
    for h in heads:
        p, hh = divmod(h, 2)
        qpair = qt_ref[2 * DH * p:2 * DH * (p + 1), :]
        keep = upper if hh == 0 else jnp.logical_not(upper)
        qa_s[h, 0:2 * DH, :] = jnp.where(keep, qpair, jnp.zeros_like(qpair))
        ones_rows = (r16 >= 3 + 3 * hh) & (r16 < 6 + 3 * hh)
        blk = jnp.where(r16 == 0, t1[h:h + 1], jnp.where(r16 == 1, t2[h:h + 1], jnp.where(
            r16 == 2, t3[h:h + 1], jnp.where(ones_rows, 1.0, 0.0))))
        qa_s[h, 2 * DH:2 * DH + 16, :] = blk.astype(BF16)
        qa_s[h, 2 * DH + 16:KA, :] = jnp.zeros((KA - 2 * DH - 16, L), BF16)
    c1, c2, c3 = _split3(fc_ref[...])
    rs, cs_ = _iota2((128, 128))
    lane = lax.broadcasted_iota(jnp.int32, (1, 128), 1)
    ones3 = jnp.where(lane < 3, 1.0, 0.0)
    for p in range(H // 2):
        def sel(i, p=p):
            hit = ((rs == 2 * p) & (cs_ == 3 + i)) | ((rs == 2 * p + 1) & (cs_ == 6 + i))
            return jnp.where(hit, -1.0, 0.0).astype(BF16)
        aug = (jnp.dot(c1, sel(0), preferred_element_type=F32) + jnp.dot(c2, sel(1), preferred_element_type=F32)
               + jnp.dot(c3, sel(2), preferred_element_type=F32) + ones3)
        ka_s[:, KA * p:KA * p + 2 * DH] = k_ref[:, 2 * DH * p:2 * DH * (p + 1)]
        ka_s[:, KA * p + 2 * DH:KA * (p + 1)] = aug.astype(BF16)

    ri, ci = _iota2((TK, TQ))
    visible = ri <= ci

    def q_block(qi, _):
        q0 = pl.multiple_of(qi * TQ, TQ)
        m_s[...] = jnp.full(m_s.shape, NEG_BIG, F32)
        l_s[...] = jnp.zeros(l_s.shape, F32)
        acc_s[...] = jnp.zeros(acc_s.shape, F32)

        def scores(slot, k0, masked):
            kt = [ka_s[pl.ds(k0, TK), KA * p:KA * (p + 1)] for p in range(H // 2)]
            for h in heads:
                s = jnp.dot(kt[h // 2], qa_s[h, :, pl.ds(q0, TQ)], preferred_element_type=F32)
                st_s[slot, h] = jnp.where(visible, s, NEG_BIG) if masked else s

        def consume(slot, k0):
            k0 = pl.multiple_of(k0, TK)
            st = [st_s[slot, h] for h in heads]
            m_old = [m_s[h:h + 1, :] for h in heads]
            m_new = [jnp.maximum(m_old[h], jnp.max(st[h], 0, keepdims=True)) for h in heads]
            pt = [jnp.exp(st[h] - m_new[h]) for h in heads]
            pv = [jnp.dot(vt_ref[DH * h:DH * (h + 1), pl.ds(k0, TK)], pt[h].astype(BF16),
                          preferred_element_type=F32) for h in heads]
            for h in heads:
                a = jnp.exp(m_old[h] - m_new[h])
                m_s[h:h + 1, :] = m_new[h]
                l_s[h:h + 1, :] = a * l_s[h:h + 1, :] + jnp.sum(pt[h], 0, keepdims=True)
                acc_s[DH * h:DH * (h + 1), :] = a * acc_s[DH * h:DH * (h + 1), :] + pv[h]

        def pair_step(jj, k_prev):
            ka = pl.multiple_of(2 * jj * TK, TK)
            kb = pl.multiple_of(ka + TK, TK)
            scores(1, ka, False)
            consume(0, k_prev)
            scores(0, kb, False)
            consume(1, ka)
            return kb

        scores(0, q0, True)
        k_prev = lax.fori_loop(0, qi // 2, pair_step, q0)

        @pl.when(qi % 2 == 1)
        def _():
            k_odd = pl.multiple_of((qi - 1) * TK, TK)
            scores(1, k_odd, False)
            consume(0, k_prev)
            consume(1, k_odd)

        @pl.when(qi % 2 == 0)
        def _():
            consume(0, k_prev)
        for p in range(H // 2):
            o_pair = jnp.concatenate([acc_s[DH * h:DH * (h + 1), :] / l_s[h:h + 1, :] for h in (2 * p, 2 * p + 1)], 0)
            o_ref[pl.ds(q0, TQ), 2 * DH * p:2 * DH * (p + 1)] = o_pair.T.astype(BF16)
        return 0

    lax.fori_loop(0, L // TQ, q_block, 0)


def _fox_attention(qt, k, vt, f_col, f_row):
    B, H, L = f_row.shape
    assert L % FOX_TILE == 0, L
    seq = pl.BlockSpec((L, HALF), lambda b: (b, 0))
    seq_t = pl.BlockSpec((HALF, L), lambda b: (0, b))
    return pl.pallas_call(
        functools.partial(_fox_kernel, L=L),
        grid=(B,),
        in_specs=[seq_t, seq, seq_t, pl.BlockSpec((L, 128), lambda b: (b, 0)),
                  pl.BlockSpec((None, H, L), lambda b: (b, 0, 0))],
        out_specs=seq,
        out_shape=jax.ShapeDtypeStruct((B * L, HALF), BF16),
        scratch_shapes=[pltpu.VMEM((H, FOX_AUG, L), BF16), pltpu.VMEM((L, FOX_AUG * H // 2), BF16),
                        pltpu.VMEM((H, FOX_TILE), F32), pltpu.VMEM((H, FOX_TILE), F32),
                        pltpu.VMEM((HALF, FOX_TILE), F32), pltpu.VMEM((2, H, FOX_TILE, FOX_TILE), F32)],
        compiler_params=_cparams("parallel"),
        name="fox_attention",
    )(qt, k, vt, f_col, f_row)


def _s5_matrices(a_re, a_im, b_re, b_im, c_re, c_im, d, log_dt):
    G, P, Cg, LC = S5_GROUPS, S5_STATE, S5_GROUP_CH, S5_CHUNK
    a_re, a_im, b_re, b_im, c_re, c_im = (t.astype(F32) for t in (a_re, a_im, b_re, b_im, c_re, c_im))
    dt = jnp.exp(log_dt.astype(F32))[:, None]
    mag = jnp.exp(a_re * dt)
    lam_re, lam_im = mag * jnp.cos(a_im * dt), mag * jnp.sin(a_im * dt)
    den = a_re ** 2 + a_im ** 2
    zr = ((lam_re - 1.0) * a_re + lam_im * a_im) / den
    zi = (lam_im * a_re - (lam_re - 1.0) * a_im) / den
    bb_re = zr[..., None] * b_re - zi[..., None] * b_im
    bb_im = zr[..., None] * b_im + zi[..., None] * b_re
    n = jnp.arange(LC + 1, dtype=F32)[:, None, None]
    pw_mag = jnp.exp(n * (a_re * dt)[None])
    pr, pi = pw_mag * jnp.cos(n * (a_im * dt)[None]), pw_mag * jnp.sin(n * (a_im * dt)[None])
    ein = functools.partial(jnp.einsum, precision=HIGHEST)
    cr = c_re[None] * pr[:, :, None, :] - c_im[None] * pi[:, :, None, :]
    ci = -(c_re[None] * pi[:, :, None, :] + c_im[None] * pr[:, :, None, :])
    kn = ein('ngcp,gpd->ngcd', cr[:LC], bb_re) + ein('ngcp,gpd->ngcd', ci[:LC], bb_im)
    s_idx = jnp.arange(LC)
    lag = s_idx[None, :] - s_idx[:, None]
    kt = jnp.where((lag >= 0)[:, :, None, None, None],
                   kn[jnp.clip(lag, 0, LC - 1)], 0.0)
    dmat = (lag == 0)[:, :, None, None, None] * (d.astype(F32).reshape(1, 1, G, Cg, 1)
                                                 * jnp.eye(Cg, dtype=F32)[None, None, None])
    NS, GS = S5_SLABS, S5_SLAB_GROUPS
    k6 = (kt + dmat).reshape(LC, LC, NS, GS, Cg, Cg)
    kmat = k6.transpose(2, 0, 3, 5, 1, 4).reshape(NS, LC * 128, LC * Cg)
    rr, ri = pr[LC - 1 - s_idx], pi[LC - 1 - s_idx]
    e_re = rr[..., None] * bb_re[None] - ri[..., None] * bb_im[None]
    e_im = rr[..., None] * bb_im[None] + ri[..., None] * bb_re[None]
    e6 = jnp.stack([e_re, e_im]).reshape(2, LC, NS, GS, P, Cg)
    emat = e6.transpose(2, 1, 3, 5, 0, 4).reshape(NS, LC * 128, 2 * P)
    f6 = jnp.stack([cr[1:], ci[1:]]).reshape(2, LC, NS, GS, Cg, P)
    fmat = f6.transpose(2, 0, 3, 5, 1, 4).reshape(NS, 2 * GS * P, LC * Cg)
    lr, li = pr[LC].reshape(NS, 1, GS * P), pi[LC].reshape(NS, 1, GS * P)
    lam_a = jnp.concatenate([lr, lr], axis=-1)
    lam_b = jnp.concatenate([-li, li], axis=-1)
    return kmat.astype(BF16), emat.astype(BF16), fmat.astype(BF16), lam_a, lam_b


def _s5_expand(src_ref, dst_ref, col_of, row_group, col_group):
    n_src, n_dst = src_ref.shape[1], dst_ref.shape[1]
    sr, sc = _iota2((n_src, n_dst))
    select = jnp.where(sr == col_of(sc), 1.0, 0.0).astype(BF16)
    step = 256
    for i in range(src_ref.shape[0] // step):
        rs = slice(i * step, (i + 1) * step)
        r, c = _iota2((step, n_dst))
        same = row_group(r + i * step) == col_group(c)
        full = jnp.dot(src_ref[rs, :], select, preferred_element_type=F32)
        dst_ref[rs, :] = jnp.where(same, full, 0.0).astype(BF16)


def _s5_kernel(u_ref, kc_ref, ec_ref, fc_ref, la_ref, lb_ref, y_ref, k_ref, e_ref, f_ref, e_s, hs_s, h_s, *, B):
    @pl.when(pl.program_id(1) == 0)
    def _():
        h_s[...] = jnp.zeros_like(h_s)
        gs, cg, p = S5_SLAB_GROUPS, S5_GROUP_CH, S5_STATE
        lg = lambda n: n.bit_length() - 1
        lane_group = lambda i: (i >> lg(cg)) & (gs - 1)
        state_group = lambda i: (i >> lg(p)) & (gs - 1)
        frame_ch = lambda c: ((c >> lg(gs * cg)) << lg(cg)) | (c & (cg - 1))
        part_state = lambda c: ((c >> lg(gs * p)) << lg(p)) | (c & (p - 1))
        _s5_expand(kc_ref, k_ref, frame_ch, lane_group, lane_group)
        _s5_expand(ec_ref, e_ref, part_state, lane_group, state_group)
        _s5_expand(fc_ref, f_ref, frame_ch, state_group, lane_group)

    u = u_ref[...]
    e_s[...] = jnp.dot(u, e_ref[...], preferred_element_type=F32)
    la, lb = la_ref[...], lb_ref[...]
    half = h_s.shape[1] // 2
    h = h_s[...]
    for kc in range(u.shape[0] // B):
        rs = slice(kc * B, (kc + 1) * B)
        hs_s[rs, :] = h
        swapped = jnp.concatenate([h[:, half:], h[:, :half]], axis=1)
        h = la * h + lb * swapped + e_s[rs, :]
    h_s[...] = h
    y_ref[...] = (jnp.dot(u, k_ref[...], preferred_element_type=F32)
                  + jnp.dot(hs_s[...].astype(BF16), f_ref[...], preferred_element_type=F32)).astype(BF16)


def _s5(u4, mats, B, L):
    kmat, emat, fmat, lam_a, lam_b = mats
    NS, LC = S5_SLABS, S5_CHUNK
    NK = L // LC
    assert L % LC == 0 and (NK <= S5_TILE_CHUNKS or NK % S5_TILE_CHUNKS == 0) and B % 8 == 0, (B, L)
    R, W, S2 = NK * B, LC * 128, 2 * S5_SLAB_GROUPS * S5_STATE
    ug = u4.reshape(NS, B, NK, W).transpose(0, 2, 1, 3).reshape(NS, R, W)
    TR = B * min(NK, S5_TILE_CHUNKS)
    rows = pl.BlockSpec((None, TR, W), lambda s, r: (s, r, 0))
    per_s = lambda a: pl.BlockSpec((None,) + a.shape[1:], lambda s, r: (s, 0, 0), pipeline_mode=pl.Buffered(1))
    y = pl.pallas_call(
        functools.partial(_s5_kernel, B=B),
        grid=(NS, R // TR),
        in_specs=[rows] + [per_s(a) for a in (kmat, emat, fmat, lam_a, lam_b)],
        out_specs=rows,
        out_shape=jax.ShapeDtypeStruct((NS, R, W), BF16),
        scratch_shapes=[pltpu.VMEM((W, W), BF16), pltpu.VMEM((W, S2), BF16), pltpu.VMEM((S2, W), BF16),
                        pltpu.VMEM((TR, S2), F32), pltpu.VMEM((TR, S2), F32), pltpu.VMEM((B, S2), F32)],
        compiler_params=_cparams("parallel", "arbitrary"),
        name="s5",
    )(ug, kmat, emat, fmat, lam_a, lam_b)
    return y.reshape(NS, NK, B, W).transpose(0, 2, 1, 3).reshape(NS, B * L, 128)


def _post_kernel(a_ref, ah_ref, b_ref, bh_ref, x_ref, xh_ref, wt_ref, wb_ref, g1_ref, beta1_ref, *rest,
                 even, tiles_per_seq, alpha):
    if even:
        wg_ref, bg_ref, *rest = rest
    wu_ref, cw_ref, cb_ref, wd_ref, g2_ref, beta2_ref, o_ref = rest
    tm = x_ref.shape[0]
    first = (pl.program_id(0) % tiles_per_seq) == 0
    a = jnp.concatenate([ah_ref[...], a_ref[...]], axis=0)
    x_in = jnp.concatenate([xh_ref[...], x_ref[...]], axis=0)
    if even:
        y = jnp.concatenate([jnp.concatenate([bh_ref[s], b_ref[s]], axis=0) for s in range(S5_SLABS)], axis=1)
        z = jax.nn.gelu(y.astype(F32))
        second = z * jax.nn.sigmoid(_mm(z, wg_ref[...]) + bg_ref[...])
    else:
        second = jnp.concatenate([bh_ref[...], b_ref[...]], axis=0)
    mix = _mm(a, wt_ref[...]) + _mm(second, wb_ref[...])
    x1e = _layer_norm(alpha * x_in + mix, g1_ref[...], beta1_ref[...])
    row = lax.broadcasted_iota(jnp.int32, (HALO + tm, 1), 0)
    x1e = jnp.where((row < HALO) & first, 0.0, x1e)
    x1 = x1e[HALO:]
    xe = x1e.astype(BF16)
    xb = xe[HALO:]

    def up(c):
        ue = jnp.dot(xe, wu_ref[:, c * FFN_TILE:(c + 1) * FFN_TILE], preferred_element_type=F32)
        gate = jnp.dot(xb, wu_ref[:, D_FF + c * FFN_TILE:D_FF + (c + 1) * FFN_TILE], preferred_element_type=F32)
        return ue, gate

    acc = jnp.zeros((tm, D_MODEL), F32)
    n_tiles = D_FF // FFN_TILE
    nxt = up(0)
    for c in range(n_tiles):
        cs = slice(c * FFN_TILE, (c + 1) * FFN_TILE)
        ue, gate = nxt
        if c + 1 < n_tiles:
            nxt = up(c + 1)
        cw = cw_ref[:, cs]
        u = (cw[2:3] * ue[HALO:] + cw[1:2] * _shift_rows(ue, 1)[HALO:] + cw[0:1] * _shift_rows(ue, 2)[HALO:]
             + cb_ref[:, cs])
        acc = acc + _mm(jax.nn.gelu(u) * gate, wd_ref[cs, :])
    o_ref[...] = _layer_norm(alpha * x1 + acc, g2_ref[...], beta2_ref[...])


def _post(a, b, x, L, w_out, ln1_g, ln1_b, w_up, conv_w, conv_b, w_down, ln2_g, ln2_b, alpha, glu=None, tm=1024):
    T = x.shape[0]
    tm = min(tm, L)
    assert L % tm == 0 and tm % HALO == 0, (L, tm)
    wo = w_out.astype(BF16)
    row1 = lambda v, n: v.reshape(1, n).astype(F32)
    slab = lambda rows, scale: pl.BlockSpec((S5_SLABS, rows, 128), scale)
    b_specs = [_rows(tm, HALF), _halo_rows(tm, HALF)]
    if glu is not None:
        b_specs = [slab(tm, lambda i: (0, i, 0)),
                   slab(HALO, lambda i: (0, jnp.maximum(i * (tm // HALO) - 1, 0), 0))]
    args = [a, a, b, b, x, x, wo[:HALF], wo[HALF:], row1(ln1_g, D_MODEL), row1(ln1_b, D_MODEL)]
    specs = ([_rows(tm, HALF), _halo_rows(tm, HALF)] + b_specs + [_rows(tm, D_MODEL), _halo_rows(tm, D_MODEL)]
             + [_resident(t.shape) for t in args[6:]])
    tail = [w_up.astype(BF16), conv_w.astype(F32), row1(conv_b, D_FF), w_down.astype(BF16),
            row1(ln2_g, D_MODEL), row1(ln2_b, D_MODEL)]
    if glu is not None:
        tail = [glu[0].astype(BF16), row1(glu[1], HALF)] + tail
    return pl.pallas_call(
        functools.partial(_post_kernel, even=glu is not None, tiles_per_seq=L // tm, alpha=alpha),
        grid=(T // tm,),
        in_specs=specs + [_resident(t.shape) for t in tail],
        out_specs=_rows(tm, D_MODEL),
        out_shape=jax.ShapeDtypeStruct((T, D_MODEL), F32),
        compiler_params=_cparams("parallel"),
        name="post",
    )(*args, *tail)


_LORA_PAD = (128, 128, 256)
_RWKV_PAD = 3 * HALF + sum(_LORA_PAD)


def _odd_pre_kernel(x_ref, xh_ref, wrk_ref, mu_ref, w2_ref, a2_ref, g2_ref, vec_ref, bd_ref,
                    wqk_ref, cw_ref, cb_ref, wv_ref, wz_ref, wif_ref, gb_ref,
                    r_ref, k_ref, v_ref, lw_ref, kk_ref, ka_ref, g_ref, bo_ref,
                    mq_ref, mk_ref, mv_ref, mz_ref, gate_ref, *, tiles_per_seq):
    tm = x_ref.shape[0]
    first = (pl.program_id(0) % tiles_per_seq) == 0
    xh = jnp.where(first, 0.0, xh_ref[...])
    xe = jnp.concatenate([xh, x_ref[...]], axis=0).astype(BF16)
    xb = xe[HALO:]

    pe = jnp.dot(xe, wrk_ref[...], preferred_element_type=F32)
    qke = jnp.dot(xe, wqk_ref[...], preferred_element_type=F32)
    mv_ref[...] = jnp.dot(xb, wv_ref[...], preferred_element_type=F32).astype(BF16)
    mz_ref[...] = jnp.dot(xb, wz_ref[...], preferred_element_type=F32).astype(mz_ref.dtype)
    pre = jnp.dot(xb, wif_ref[...], preferred_element_type=F32)[:, :2 * MLSTM_HEADS] + gb_ref[...]

    cur, prev = pe[HALO:], _shift_rows(pe, 1)[HALO:]
    p = cur + (prev - cur) * mu_ref[...]
    r, k, v = p[:, :HALF], p[:, HALF:2 * HALF], p[:, 2 * HALF:3 * HALF]
    o0 = 3 * HALF
    wd = p[:, o0:o0 + _LORA_PAD[0]]
    ad = p[:, o0 + _LORA_PAD[0]:o0 + _LORA_PAD[0] + _LORA_PAD[1]]
    gd = p[:, o0 + _LORA_PAD[0] + _LORA_PAD[1]:]
    w0, a0, k_k, k_a, r_k = (vec_ref[i:i + 1, :] for i in range(5))
    wlog = -jax.nn.softplus(-(w0 + _mm(jnp.tanh(wd), w2_ref[...]))) - 0.5
    lw_ref[...] = -jnp.exp(wlog)
    a = jax.nn.sigmoid(a0 + _mm(ad, a2_ref[...]))
    g_ref[...] = _mm(jax.nn.sigmoid(gd), g2_ref[...]).astype(g_ref.dtype)
    kk = k * k_k
    ss = _mm_exact_rhs(kk * kk, bd_ref[...])
    kk = kk * lax.rsqrt(jnp.maximum(ss, 1e-24))
    kmod = k * (1.0 + (a - 1.0) * k_a)
    bo_ref[...] = (_mm_exact_rhs(r * kmod * r_k, bd_ref[...]) * v).astype(bo_ref.dtype)
    r_ref[...] = r.astype(r_ref.dtype)
    k_ref[...] = kmod.astype(k_ref.dtype)
    v_ref[...] = v.astype(v_ref.dtype)
    kk_ref[...] = kk.astype(kk_ref.dtype)
    ka_ref[...] = (kk * a).astype(ka_ref.dtype)

    cw = cw_ref[...]
    qk = cb_ref[...]
    for j in range(MLSTM_CONV):
        qk = qk + cw[j:j + 1] * _shift_rows(qke, MLSTM_CONV - 1 - j)[HALO:]
    qk = jax.nn.silu(qk)
    mq_ref[...] = qk[:, :HALF].astype(mq_ref.dtype)
    mk_ref[...] = (qk[:, HALF:] * (MLSTM_DH ** -0.5)).astype(BF16)
    is_i = lax.broadcasted_iota(jnp.int32, pre.shape, 1) < MLSTM_HEADS
    gate_ref[...] = jnp.where(is_i, pre, jax.nn.log_sigmoid(pre))


def _head_block_ones(width, head):
    idx = jnp.arange(width) // head
    return (idx[:, None] == idx[None, :]).astype(BF16)


def _odd_pre(x, L, w_in, mu, w0, w2, a0, a2, g2, k_k, k_a, r_k, conv_w, conv_b, ib, fb, tm=512):
    T = x.shape[0]
    tm = min(tm, L)
    assert L % tm == 0 and tm % HALO == 0, (L, tm)
    wb = w_in.astype(BF16)
    o = 3 * HALF
    sizes = (RWKV_DECAY_LORA, RWKV_AAA_LORA, RWKV_GATE_LORA)

    def pad_lora(m, axis):
        parts, s = [], o
        for sz, pd in zip(sizes, _LORA_PAD):
            piece = lax.slice_in_dim(m, s, s + sz, axis=axis)
            widths = [(0, 0)] * m.ndim
            widths[axis] = (0, pd - sz)
            parts.append(jnp.pad(piece, widths))
            s += sz
        return jnp.concatenate([lax.slice_in_dim(m, 0, o, axis=axis)] + parts, axis=axis)

    rwkv_proj = o + sum(sizes)
    wrk = pad_lora(wb[:, :rwkv_proj], 1)
    mu_p = pad_lora(mu.astype(F32).reshape(1, -1), 1)
    padr = lambda m, rows: jnp.pad(m.astype(BF16), ((0, rows - m.shape[0]), (0, 0)))
    w2p, a2p, g2p = padr(w2, _LORA_PAD[0]), padr(a2, _LORA_PAD[1]), padr(g2, _LORA_PAD[2])
    vecs = jnp.stack([w0, a0, k_k, k_a, r_k.reshape(HALF)]).astype(F32)
    vecs = jnp.pad(vecs, ((0, 8 - vecs.shape[0]), (0, 0)))
    bd = _head_block_ones(HALF, RWKV_N)
    wm = wb[:, rwkv_proj:]
    wqk, wmv, wmz = wm[:, :2 * HALF], wm[:, 2 * HALF:3 * HALF], wm[:, 3 * HALF:4 * HALF]
    wif = jnp.pad(wm[:, 4 * HALF:], ((0, 0), (0, 128 - 2 * MLSTM_HEADS)))
    gbias = jnp.concatenate([ib, fb]).astype(F32).reshape(1, 2 * MLSTM_HEADS)
    args = [x, x, wrk, mu_p, w2p, a2p, g2p, vecs, bd, wqk, conv_w.astype(F32),
            conv_b.astype(F32).reshape(1, 2 * HALF), wmv, wmz, wif, gbias]
    specs = [_rows(tm, D_MODEL), _halo_rows(tm, D_MODEL)] + [_resident(a.shape) for a in args[2:]]
    f_half = jax.ShapeDtypeStruct((T, HALF), F32)
    b_half = jax.ShapeDtypeStruct((T, HALF), BF16)
    out_shape = [b_half] * 3 + [f_half] + [b_half] * 8 + [jax.ShapeDtypeStruct((T, 2 * MLSTM_HEADS), F32)]
    out_specs = [_rows(tm, HALF)] * 12 + [_rows(tm, 2 * MLSTM_HEADS)]
    return pl.pallas_call(
        functools.partial(_odd_pre_kernel, tiles_per_seq=L // tm),
        grid=(T // tm,),
        in_specs=specs,
        out_specs=out_specs,
        out_shape=out_shape,
        compiler_params=_cparams("parallel"),
        name="odd_pre",
    )(*args)


def _rwkv_kernel(r_ref, k_ref, v_ref, lw_ref, kk_ref, ka_ref, g_ref, bo_ref, lng_ref, lnb_ref, o_ref, m_s):
    C, N, H = CHUNK, RWKV_N, RWKV_HEADS
    rows = range(r_ref.shape[0])
    units = [(b, h) for b in rows for h in range(H)]

    @pl.when(pl.program_id(1) == 0)
    def _():
        m_s[...] = jnp.zeros_like(m_s)

    ri, ci = _iota2((C, C))
    incl, strict, eye = ci <= ri, ci < ri, ci == ri
    ri2, ci2 = _iota2((C, 2 * C))
    incl2 = (ci2 & (C - 1)) <= ri2
    zeros = jnp.zeros((C, N), F32)
    mm = functools.partial(_mmp, passes=RWKV_PASSES)
    al, rt, bt, kt, bp, kp, gam, v = ({} for _ in range(8))
    for b in rows:
        lw = lw_ref[b]
        cs = _mm_exact_lhs(incl.astype(BF16), lw)
        cend = cs[C - 1:C, :]
        e_neg = jnp.exp(-cs)
        e_rem = jnp.exp(cend - cs)
        r, k, vv, kk, ka = (t[b].astype(F32) for t in (r_ref, k_ref, v_ref, kk_ref, ka_ref))
        al_b = -kk * jnp.exp(cs - lw)
        rt_b = r * jnp.exp(cs)
        gam_b = jnp.exp(cend)
        bt_b, kt_b, bp_b, kp_b = ka * e_neg, k * e_neg, ka * e_rem, k * e_rem
        for h in range(H):
            s = slice(N * h, N * (h + 1))
            u = (b, h)
            al[u], rt[u], v[u], gam[u] = al_b[:, s], rt_b[:, s], vv[:, s], gam_b[:, s]
            bt[u], kt[u], bp[u], kp[u] = bt_b[:, s], kt_b[:, s], bp_b[:, s], kp_b[:, s]
    pm = {u: mm(jnp.concatenate([al[u], rt[u]], 0), jnp.concatenate([bt[u], kt[u]], 0), 1, 1) for u in units}
    a_ab = {u: jnp.where(strict, pm[u][:C, :C], 0.0) for u in units}
    a_ak = {u: jnp.where(strict, pm[u][:C, C:], 0.0) for u in units}
    a_r = {u: jnp.where(incl2, pm[u][C:, :], 0.0) for u in units}
    w = {u: jnp.concatenate([al[u], mm(a_ak[u], v[u])], 1) for u in units}
    npow = a_ab
    levels = int(math.log2(C))
    for lvl in range(levels):
        if lvl < levels - 1:
            y = {u: mm(npow[u], jnp.concatenate([w[u], npow[u]], 1)) for u in units}
            w = {u: w[u] + y[u][:, :2 * N] for u in units}
            npow = {u: y[u][:, 2 * N:] for u in units}
        else:
            w = {u: w[u] + mm(npow[u], w[u]) for u in units}
    zv = {u: jnp.concatenate([zeros, v[u]], 1) for u in units}
    qt = {u: mm(a_r[u], jnp.concatenate([w[u], zv[u]], 0)) for u in units}
    qb = {u: mm(bp[u], w[u], 0, 0) + mm(kp[u], zv[u], 0, 0) for u in units}
    m0 = {u: m_s[u[0], u[1]] for u in units}
    o_h = {u: mm(rt[u] + qt[u][:, :N], m0[u]) + qt[u][:, N:] for u in units}
    m_new = {u: _mmp(qb[u][:, :N] + jnp.where(eye, gam[u], 0.0), m0[u], passes=RWKV_STATE_PASSES) + qb[u][:, N:]
             for u in units}
    for u in units:
        m_s[u[0], u[1]] = m_new[u]
    om = {u: jnp.mean(o_h[u], -1, keepdims=True) for u in units}
    d = {u: o_h[u] - om[u] for u in units}
    ov = {u: jnp.mean(d[u] * d[u], -1, keepdims=True) for u in units}
    for b in rows:
        outs = [d[(b, h)] * lax.rsqrt(ov[(b, h)] + RWKV_GN_EPS) for h in range(H)]
        on = jnp.concatenate(outs, 1) * lng_ref[...] + lnb_ref[...]
        o_ref[b] = ((on + bo_ref[b].astype(F32)) * g_ref[b].astype(F32)).astype(BF16)


def _rwkv(r, k, v, lw, kk, ka, g, bonus, ln_g, ln_b, B, L):
    NC = L // CHUNK
    assert L % CHUNK == 0, L
    RB = SEQ_ROWS if B % SEQ_ROWS == 0 else 1
    seq3 = lambda t: t.reshape(B, L, HALF)
    blk = pl.BlockSpec((RB, CHUNK, HALF), lambda b, c: (b, c, 0))
    vec = pl.BlockSpec((1, HALF), lambda b, c: (0, 0))
    out = pl.pallas_call(
        _rwkv_kernel,
        grid=(B // RB, NC),
        in_specs=[blk] * 8 + [vec, vec],
        out_specs=blk,
        out_shape=jax.ShapeDtypeStruct((B, L, HALF), BF16),
        scratch_shapes=[pltpu.VMEM((RB, RWKV_HEADS, RWKV_N, RWKV_N), F32)],
        compiler_params=_cparams("parallel", "arbitrary"),
        name="rwkv7",
    )(*(seq3(t) for t in (r, k, v, lw, kk, ka, g, bonus)),
      ln_g.reshape(1, HALF).astype(F32), ln_b.reshape(1, HALF).astype(F32))
    return out.reshape(B * L, HALF)


def _mlstm_kernel(q_ref, k_ref, v_ref, z_ref, gc_ref, gr_ref, lng_ref, skip_ref, o_ref, c_s, m_s):
    C, H, DH = CHUNK, MLSTM_HEADS, MLSTM_DH
    rows = range(q_ref.shape[0])
    units = [(b, h) for b in rows for h in range(H)]

    @pl.when(pl.program_id(1) == 0)
    def _():
        c_s[...] = jnp.zeros_like(c_s)
        m_s[...] = jnp.zeros_like(m_s)

    ri, ci = _iota2((C, C))
    incl = ci <= ri
    lower, upper = incl.astype(BF16), (ri <= ci).astype(BF16)
    lane = lax.broadcasted_iota(jnp.int32, (1, DH), 1)
    ones_blk = jnp.broadcast_to(jnp.where(lane == 0, 1.0, 0.0).astype(BF16), (C, DH))
    q_all, qb, kh, vaug, i_col, i_row, b_col, b_row, m_prev = ({} for _ in range(9))
    for b in rows:
        gc, gr = gc_ref[b], gr_ref[b]
        b_cols = _mm_exact_lhs(lower, gc)
        b_rows = _mm_exact_rhs(gr, upper)
        q_all[b] = q_ref[b].astype(F32)
        for h in range(H):
            u, s = (b, h), slice(DH * h, DH * (h + 1))
            qb[u], kh[u] = q_ref[b, :, s], k_ref[b, :, s]
            vaug[u] = jnp.concatenate([v_ref[b, :, s], ones_blk], 1)
            i_col[u], i_row[u] = gc[:, h:h + 1], gr[h:h + 1, :]
            b_col[u], b_row[u] = b_cols[:, H + h:H + h + 1], b_rows[H + h:H + h + 1, :]
            m_prev[u] = m_s[b, h:h + 1, 0:1]
    c_old = {u: c_s[u[0], u[1]] for u in units}
    qk = {u: _dg(qb[u], kh[u], 1, 1) for u in units}
    qc = {u: jnp.dot(qb[u], c_old[u].astype(BF16), preferred_element_type=F32) for u in units}
    dmat = {u: jnp.where(incl, b_col[u] - b_row[u] + i_row[u], NEG_BIG) for u in units}
    inter = {u: b_col[u] + m_prev[u] for u in units}
    m_t = {u: jnp.maximum(inter[u], jnp.max(dmat[u], -1, keepdims=True)) for u in units}
    s = {u: (qk[u] * jnp.exp(dmat[u] - m_t[u])).astype(BF16) for u in units}
    sv = {u: jnp.dot(s[u], vaug[u], preferred_element_type=F32) for u in units}
    b_end = {u: b_col[u][C - 1:C, :] for u in units}
    m_new = {u: jnp.maximum(b_end[u] + m_prev[u], jnp.max(b_end[u] - b_row[u] + i_row[u], -1, keepdims=True))
             for u in units}
    kw = {u: (jnp.exp(b_end[u] - b_col[u] + i_col[u] - m_new[u]) * kh[u].astype(F32)).astype(BF16) for u in units}
    kv = {u: _dg(kw[u], vaug[u], 0, 0) for u in units}
    for u in units:
        b, h = u
        c_s[b, h] = jnp.exp(b_end[u] + m_prev[u] - m_new[u]) * c_old[u] + kv[u]
        m_s[b, h:h + 1, :] = jnp.broadcast_to(m_new[u], (1, 128))
    nd = {u: jnp.exp(inter[u] - m_t[u]) * qc[u] + sv[u] for u in units}
    den = {u: jnp.maximum(jnp.abs(nd[u][:, DH:DH + 1]), jnp.exp(-m_t[u])) for u in units}
    hid = {u: nd[u][:, :DH] / den[u] for u in units}
    hm = {u: jnp.mean(hid[u], -1, keepdims=True) for u in units}
    d = {u: hid[u] - hm[u] for u in units}
    hv = {u: jnp.mean(d[u] * d[u], -1, keepdims=True) for u in units}
    for b in rows:
        outs = [d[(b, h)] * lax.rsqrt(hv[(b, h)] + LN_EPS) for h in range(H)]
        hn = jnp.concatenate(outs, 1) * lng_ref[...]
        o_ref[b] = ((hn + skip_ref[...] * q_all[b]) * jax.nn.silu(z_ref[b].astype(F32))).astype(BF16)


def _mlstm(q, k, v, z, gates, ln_g, skip, B, L):
    NC, H2 = L // CHUNK, 2 * MLSTM_HEADS
    assert L % CHUNK == 0, L
    RB = MLSTM_SEQ_ROWS if B % MLSTM_SEQ_ROWS == 0 else 1
    seq3 = lambda t: t.reshape(B, L, t.shape[-1])
    blk = pl.BlockSpec((RB, CHUNK, HALF), lambda b, c: (b, c, 0))
    vec = pl.BlockSpec((1, HALF), lambda b, c: (0, 0))
    g_rows = gates.reshape(B, NC, CHUNK, H2).transpose(0, 1, 3, 2)
    out = pl.pallas_call(
        _mlstm_kernel,
        grid=(B // RB, NC),
        in_specs=[blk, blk, blk, blk, pl.BlockSpec((RB, CHUNK, H2), lambda b, c: (b, c, 0)),
                  pl.BlockSpec((RB, None, H2, CHUNK), lambda b, c: (b, c, 0, 0)), vec, vec],
        out_specs=blk,
        out_shape=jax.ShapeDtypeStruct((B, L, HALF), BF16),
        scratch_shapes=[pltpu.VMEM((RB, MLSTM_HEADS, MLSTM_DH, 2 * MLSTM_DH), F32),
                        pltpu.VMEM((RB, 8, 128), F32)],
        compiler_params=_cparams("parallel", "arbitrary"),
        name="mlstm",
    )(seq3(q), seq3(k), seq3(v), seq3(z), seq3(gates), g_rows,
      ln_g.reshape(1, HALF).astype(F32), skip.reshape(1, HALF).astype(F32))
    return out.reshape(B * L, HALF)


def kernel(x, ev_w_in, ev_fox_fb, ev_s5_a_re, ev_s5_a_im, ev_s5_b_re, ev_s5_b_im, ev_s5_c_re, ev_s5_c_im, ev_s5_d, ev_s5_log_dt, ev_s5_w_glu, ev_s5_b_glu, ev_w_out, od_w_in, od_rwkv_mu, od_rwkv_w0, od_rwkv_w2, od_rwkv_a0, od_rwkv_a2, od_rwkv_g2, od_rwkv_k_k, od_rwkv_k_a, od_rwkv_r_k, od_rwkv_ln_g, od_rwkv_ln_b, od_mlstm_conv_w, od_mlstm_conv_b, od_mlstm_ib, od_mlstm_fb, od_mlstm_ln_g, od_mlstm_skip, od_w_out, ln1_g, ln1_b, ffn_w_up, ffn_conv_w, ffn_conv_b, ffn_w_down, ln2_g, ln2_b):
    B, L, _ = x.shape
    depth = ln1_g.shape[0]
    alpha = float((2 * depth) ** 0.25)
    h = x.reshape(B * L, D_MODEL).astype(F32)
    for layer in range(depth):
        i = layer // 2
        if layer % 2 == 0:
            qt, k, vt, fg, u = _even_pre(h, ev_w_in[i])
            f_row = _fox_gate(fg.reshape(B, L, FOX_HEADS).transpose(0, 2, 1), ev_fox_fb[i].astype(F32))
            f_col = jnp.pad(f_row.transpose(0, 2, 1).reshape(B * L, FOX_HEADS), ((0, 0), (0, 128 - FOX_HEADS)))
            fox = _fox_attention(qt, k, vt, f_col, f_row)
            mats = _s5_matrices(ev_s5_a_re[i], ev_s5_a_im[i], ev_s5_b_re[i], ev_s5_b_im[i],
                                ev_s5_c_re[i], ev_s5_c_im[i], ev_s5_d[i], ev_s5_log_dt[i])
            y = _s5(u, mats, B, L)
            mixed, glu, w_out = (fox, y), (ev_s5_w_glu[i], ev_s5_b_glu[i]), ev_w_out[i]
        else:
            (r, k, v, lw, kk, ka, g, bonus, mq, mk, mv, mz, gates) = _odd_pre(
                h, L, od_w_in[i], od_rwkv_mu[i], od_rwkv_w0[i], od_rwkv_w2[i], od_rwkv_a0[i], od_rwkv_a2[i],
                od_rwkv_g2[i], od_rwkv_k_k[i], od_rwkv_k_a[i], od_rwkv_r_k[i], od_mlstm_conv_w[i],
                od_mlstm_conv_b[i], od_mlstm_ib[i], od_mlstm_fb[i])
            c = _rwkv(r, k, v, lw, kk, ka, g, bonus, od_rwkv_ln_g[i], od_rwkv_ln_b[i], B, L)
            dm = _mlstm(mq, mk, mv, mz, gates, od_mlstm_ln_g[i], od_mlstm_skip[i], B, L)
            mixed, glu, w_out = (c, dm), None, od_w_out[i]
        h = _post(mixed[0], mixed[1], h, L, w_out, ln1_g[layer], ln1_b[layer], ffn_w_up[layer], ffn_conv_w[layer],
                  ffn_conv_b[layer], ffn_w_down[layer], ln2_g[layer], ln2_b[layer], alpha, glu=glu)
    return h.reshape(B, L, D_MODEL).astype(x.dtype)
```

```python
import functools
import math

import jax
import jax.numpy as jnp
from jax import lax
from jax.experimental import pallas as pl
from jax.experimental.pallas import tpu as pltpu

F32 = jnp.float32
BF16 = jnp.bfloat16
HIGHEST = lax.Precision.HIGHEST

D_MODEL = 1024
HALF = D_MODEL // 2
FOX_HEADS = 8
FOX_DH = HALF // FOX_HEADS
FOX_TILE = 256
FOX_AUG = 256
S5_GROUPS = 32
S5_GROUP_CH = 16
S5_STATE = 64
S5_CHUNK = 16
S5_SLAB_GROUPS = 128 // S5_GROUP_CH
S5_SLABS = S5_GROUPS // S5_SLAB_GROUPS
S5_TILE_CHUNKS = 16
RWKV_HEADS = 8
RWKV_N = HALF // RWKV_HEADS
RWKV_DECAY_LORA = 64
RWKV_AAA_LORA = 64
RWKV_GATE_LORA = 160
RWKV_GN_EPS = 64e-5
RWKV_PASSES = 1
RWKV_STATE_PASSES = 3
MLSTM_HEADS = 4
MLSTM_DH = HALF // MLSTM_HEADS
MLSTM_CONV = 4
CHUNK = 64
SEQ_ROWS = 4
MLSTM_SEQ_ROWS = 4
D_FF = 2816
FFN_CONV = 3
FFN_TILE = 256
LN_EPS = 1e-5
HALO = 16
NEG_BIG = -1e30
V7X_VMEM_LIMIT_BYTES = 56 * 1024 * 1024


def _cparams(*sem):
    return pltpu.CompilerParams(dimension_semantics=sem, vmem_limit_bytes=V7X_VMEM_LIMIT_BYTES)


def _resident(shape):
    nd = len(shape)
    return pl.BlockSpec(shape, lambda *_: (0,) * nd, pipeline_mode=pl.Buffered(1))


def _rows(tm, width):
    return pl.BlockSpec((tm, width), lambda i: (i, 0))


def _halo_rows(tm, width):
    return pl.BlockSpec((HALO, width), lambda i: (jnp.maximum(i * (tm // HALO) - 1, 0), 0))


def _mm(a, b):
    return jnp.dot(a.astype(BF16), b.astype(BF16), preferred_element_type=F32)


def _dg(a, b, ca, cb):
    return lax.dot_general(a, b, (((ca,), (cb,)), ((), ())), preferred_element_type=F32)


def _hi_lo(a):
    hi = a.astype(BF16)
    lo = (a - hi.astype(F32)).astype(BF16)
    return hi, lo


def _mm3(a, b, ca=1, cb=0):
    ah, al = _hi_lo(a)
    bh, bl = _hi_lo(b)
    return _dg(ah, bh, ca, cb) + _dg(ah, bl, ca, cb) + _dg(al, bh, ca, cb)


def _mmp(a, b, ca=1, cb=0, passes=1):
    if passes == 3:
        return _mm3(a, b, ca, cb)
    return _dg(a.astype(BF16), b.astype(BF16), ca, cb)


def _split3(a):
    a1 = a.astype(BF16)
    r1 = a - a1.astype(F32)
    a2 = r1.astype(BF16)
    a3 = (r1 - a2.astype(F32)).astype(BF16)
    return a1, a2, a3


def _mm_exact_rhs(a, b01, terms=3):
    parts = _split3(a) if terms == 3 else _hi_lo(a)
    out = jnp.dot(parts[0], b01, preferred_element_type=F32)
    for part in parts[1:]:
        out = out + jnp.dot(part, b01, preferred_element_type=F32)
    return out


def _mm_exact_lhs(a01, b):
    b1, b2, b3 = _split3(b)
    return (jnp.dot(a01, b1, preferred_element_type=F32) + jnp.dot(a01, b2, preferred_element_type=F32)
            + jnp.dot(a01, b3, preferred_element_type=F32))


def _layer_norm(x, g, b):
    mu = jnp.mean(x, -1, keepdims=True)
    d = x - mu
    var = jnp.mean(d * d, -1, keepdims=True)
    return d * lax.rsqrt(var + LN_EPS) * g + b


def _shift_rows(x, n):
    return x if n == 0 else pltpu.roll(x, n, 0)


def _iota2(shape):
    return lax.broadcasted_iota(jnp.int32, shape, 0), lax.broadcasted_iota(jnp.int32, shape, 1)


def _even_pre_kernel(x_ref, wqt_ref, wk_ref, wvt_ref, wf_ref, wu_ref, qt_ref, k_ref, vt_ref, fg_ref, u_ref):
    xb = x_ref[...].astype(BF16)
    qt_ref[...] = (_dg(wqt_ref[...], xb, 1, 1) * (FOX_DH ** -0.5)).astype(BF16)
    vt_ref[...] = _dg(wvt_ref[...], xb, 1, 1).astype(BF16)
    k_ref[...] = jnp.dot(xb, wk_ref[...], preferred_element_type=F32).astype(BF16)
    u = jnp.dot(xb, wu_ref[...], preferred_element_type=F32).astype(BF16)
    for s in range(S5_SLABS):
        u_ref[s] = u[:, 128 * s:128 * (s + 1)]
    fg_ref[...] = jnp.dot(xb, wf_ref[...], preferred_element_type=F32)[:, :FOX_HEADS]


def _even_pre(x, w_in, tm=512):
    T = x.shape[0]
    wb = w_in.astype(BF16)
    wqt, wk, wvt = wb[:, :HALF].T, wb[:, HALF:2 * HALF], wb[:, 2 * HALF:3 * HALF].T
    wf = jnp.pad(wb[:, 3 * HALF:3 * HALF + FOX_HEADS], ((0, 0), (0, 128 - FOX_HEADS)))
    wu = wb[:, 3 * HALF + FOX_HEADS:]
    half_out = jax.ShapeDtypeStruct((T, HALF), BF16)
    half_t = jax.ShapeDtypeStruct((HALF, T), BF16)
    cols = pl.BlockSpec((HALF, tm), lambda i: (0, i))
    return pl.pallas_call(
        _even_pre_kernel,
        grid=(T // tm,),
        in_specs=[_rows(tm, D_MODEL), _resident(wqt.shape), _resident(wk.shape), _resident(wvt.shape),
                  _resident(wf.shape), _resident(wu.shape)],
        out_specs=[cols, _rows(tm, HALF), cols, _rows(tm, FOX_HEADS),
                   pl.BlockSpec((S5_SLABS, tm, 128), lambda i: (0, i, 0))],
        out_shape=[half_t, half_out, half_t, jax.ShapeDtypeStruct((T, FOX_HEADS), F32),
                   jax.ShapeDtypeStruct((S5_SLABS, T, 128), BF16)],
        compiler_params=_cparams("parallel"),
        name="even_pre",
    )(x, wqt, wk, wvt, wf, wu)


def _fox_gate_kernel(fg_ref, fb_ref, o_ref, *, L):
    ls = jax.nn.log_sigmoid(fg_ref[...] + fb_ref[...])
    r, c = _iota2((128, 128))
    tri = (r <= c).astype(BF16)
    carry = jnp.zeros((FOX_HEADS, 1), F32)
    for j in range(L // 128):
        cum = _mm_exact_rhs(ls[:, j * 128:(j + 1) * 128], tri) + carry
        o_ref[:, j * 128:(j + 1) * 128] = cum
        carry = cum[:, 127:128]


def _fox_gate(fg_t, fb):
    B, H, L = fg_t.shape
    return pl.pallas_call(
        functools.partial(_fox_gate_kernel, L=L),
        grid=(B,),
        in_specs=[pl.BlockSpec((None, H, L), lambda b: (b, 0, 0)), _resident((H, 1))],
        out_specs=pl.BlockSpec((None, H, L), lambda b: (b, 0, 0)),
        out_shape=jax.ShapeDtypeStruct((B, H, L), F32),
        compiler_params=_cparams("parallel"),
        name="fox_gate",
    )(fg_t, fb.reshape(H, 1))


def _fox_kernel(qt_ref, k_ref, vt_ref, fc_ref, fr_ref, o_ref, qa_s, ka_s, m_s, l_s, acc_s, st_s, *, L):
    TQ = TK = FOX_TILE
    H, DH, KA = FOX_HEADS, FOX_DH, FOX_AUG
    heads = range(H)

    t1, t2, t3 = (t.astype(F32) for t in _split3(fr_ref[...]))
    r16 = lax.broadcasted_iota(jnp.int32, (16, 1), 0)
    upper = lax.broadcasted_iota(jnp.int32, (2 * DH, 1), 0) < DH
    for h in heads:
        p, hh = divmod(h, 2)
        qpair = qt_ref[2 * DH * p:2 * DH * (p + 1), :]
        keep = upper if hh == 0 else jnp.logical_not(upper)
        qa_s[h, 0:2 * DH, :] = jnp.where(keep, qpair, jnp.zeros_like(qpair))
        ones_rows = (r16 >= 3 + 3 * hh) & (r16 < 6 + 3 * hh)
        blk = jnp.where(r16 == 0, t1[h:h + 1], jnp.where(r16 == 1, t2[h:h + 1], jnp.where(
            r16 == 2, t3[h:h + 1], jnp.where(ones_rows, 1.0, 0.0))))
        qa_s[h, 2 * DH:2 * DH + 16, :] = blk.astype(BF16)
        qa_s[h, 2 * DH + 16:KA, :] = jnp.zeros((KA - 2 * DH - 16, L), BF16)
    c1, c2, c3 = _split3(fc_ref[...])
    rs, cs_ = _iota2((128, 128))
    lane = lax.broadcasted_iota(jnp.int32, (1, 128), 1)
    ones3 = jnp.where(lane < 3, 1.0, 0.0)
    for p in range(H // 2):
        def sel(i, p=p):
            hit = ((rs == 2 * p) & (cs_ == 3 + i)) | ((rs == 2 * p + 1) & (cs_ == 6 + i))
            return jnp.where(hit, -1.0, 0.0).astype(BF16)
        aug = (jnp.dot(c1, sel(0), preferred_element_type=F32) + jnp.dot(c2, sel(1), preferred_element_type=F32)
               + jnp.dot(c3, sel(2), preferred_element_type=F32) + ones3)
        ka_s[:, KA * p:KA * p + 2 * DH] = k_ref[:, 2 * DH * p:2 * DH * (p + 1)]
        ka_s[:, KA * p + 2 * DH:KA * (p + 1)] = aug.astype(BF16)

    ri, ci = _iota2((TK, TQ))
    visible = ri <= ci

    def q_block(qi, _):
        q0 = pl.multiple_of(qi * TQ, TQ)
        m_s[...] = jnp.full(m_s.shape, NEG_BIG, F32)
        l_s[...] = jnp.zeros(l_s.shape, F32)
        acc_s[...] = jnp.zeros(acc_s.shape, F32)

        def scores(slot, k0, masked):
            kt = [ka_s[pl.ds(k0, TK), KA * p:KA * (p + 1)] for p in range(H // 2)]
            for h in heads:
                s = jnp.dot(kt[h // 2], qa_s[h, :, pl.ds(q0, TQ)], preferred_element_type=F32)
                st_s[slot, h] = jnp.where(visible, s, NEG_BIG) if masked else s

        def consume(slot, k0):
            k0 = pl.multiple_of(k0, TK)
            st = [st_s[slot, h] for h in heads]
            m_old = [m_s[h:h + 1, :] for h in heads]
            m_new = [jnp.maximum(m_old[h], jnp.max(st[h], 0, keepdims=True)) for h in heads]
            pt = [jnp.exp(st[h] - m_new[h]) for h in heads]
            pv = [jnp.dot(vt_ref[DH * h:DH * (h + 1), pl.ds(k0, TK)], pt[h].astype(BF16),
                          preferred_element_type=F32) for h in heads]
            for h in heads:
                a = jnp.exp(m_old[h] - m_new[h])
                m_s[h:h + 1, :] = m_new[h]
                l_s[h:h + 1, :] = a * l_s[h:h + 1, :] + jnp.sum(pt[h], 0, keepdims=True)
                acc_s[DH * h:DH * (h + 1), :] = a * acc_s[DH * h:DH * (h + 1), :] + pv[h]

        def pair_step(jj, k_prev):
            ka = pl.multiple_of(2 * jj * TK, TK)
            kb = pl.multiple_of(ka + TK, TK)
            scores(1, ka, False)
            consume(0, k_prev)
            scores(0, kb, False)
            consume(1, ka)
            return kb

        scores(0, q0, True)
        k_prev = lax.fori_loop(0, qi // 2, pair_step, q0)

        @pl.when(qi % 2 == 1)
        def _():
            k_odd = pl.multiple_of((qi - 1) * TK, TK)
            scores(1, k_odd, False)
            consume(0, k_prev)
            consume(1, k_odd)

        @pl.when(qi % 2 == 0)
        def _():
            consume(0, k_prev)
        for p in range(H // 2):
            o_pair = jnp.concatenate([acc_s[DH * h:DH * (h + 1), :] / l_s[h:h + 1, :] for h in (2 * p, 2 * p + 1)], 0)
            o_ref[pl.ds(q0, TQ), 2 * DH * p:2 * DH * (p + 1)] = o_pair.T.astype(BF16)
        return 0

    lax.fori_loop(0, L // TQ, q_block, 0)


def _fox_attention(qt, k, vt, f_col, f_row):
    B, H, L = f_row.shape
    assert L % FOX_TILE == 0, L
    seq = pl.BlockSpec((L, HALF), lambda b: (b, 0))
    seq_t = pl.BlockSpec((HALF, L), lambda b: (0, b))
    return pl.pallas_call(
        functools.partial(_fox_kernel, L=L),
        grid=(B,),
        in_specs=[seq_t, seq, seq_t, pl.BlockSpec((L, 128), lambda b: (b, 0)),
                  pl.BlockSpec((None, H, L), lambda b: (b, 0, 0))],
        out_specs=seq,
        out_shape=jax.ShapeDtypeStruct((B * L, HALF), BF16),
        scratch_shapes=[pltpu.VMEM((H, FOX_AUG, L), BF16), pltpu.VMEM((L, FOX_AUG * H // 2), BF16),
                        pltpu.VMEM((H, FOX_TILE), F32), pltpu.VMEM((H, FOX_TILE), F32),
                        pltpu.VMEM((HALF, FOX_TILE), F32), pltpu.VMEM((2, H, FOX_TILE, FOX_TILE), F32)],
        compiler_params=_cparams("parallel"),
        name="fox_attention",
    )(qt, k, vt, f_col, f_row)


def _s5_matrices(a_re, a_im, b_re, b_im, c_re, c_im, d, log_dt):
    G, P, Cg, LC = S5_GROUPS, S5_STATE, S5_GROUP_CH, S5_CHUNK
    a_re, a_im, b_re, b_im, c_re, c_im = (t.astype(F32) for t in (a_re, a_im, b_re, b_im, c_re, c_im))
    dt = jnp.exp(log_dt.astype(F32))[:, None]
    mag = jnp.exp(a_re * dt)
    lam_re, lam_im = mag * jnp.cos(a_im * dt), mag * jnp.sin(a_im * dt)
    den = a_re ** 2 + a_im ** 2
    zr = ((lam_re - 1.0) * a_re + lam_im * a_im) / den
    zi = (lam_im * a_re - (lam_re - 1.0) * a_im) / den
    bb_re = zr[..., None] * b_re - zi[..., None] * b_im
    bb_im = zr[..., None] * b_im + zi[..., None] * b_re
    n = jnp.arange(LC + 1, dtype=F32)[:, None, None]
    pw_mag = jnp.exp(n * (a_re * dt)[None])
    pr, pi = pw_mag * jnp.cos(n * (a_im * dt)[None]), pw_mag * jnp.sin(n * (a_im * dt)[None])
    ein = functools.partial(jnp.einsum, precision=HIGHEST)
    cr = c_re[None] * pr[:, :, None, :] - c_im[None] * pi[:, :, None, :]
    ci = -(c_re[None] * pi[:, :, None, :] + c_im[None] * pr[:, :, None, :])
    kn = ein('ngcp,gpd->ngcd', cr[:LC], bb_re) + ein('ngcp,gpd->ngcd', ci[:LC], bb_im)
    s_idx = jnp.arange(LC)
    lag = s_idx[None, :] - s_idx[:, None]
    kt = jnp.where((lag >= 0)[:, :, None, None, None],
                   kn[jnp.clip(lag, 0, LC - 1)], 0.0)
    dmat = (lag == 0)[:, :, None, None, None] * (d.astype(F32).reshape(1, 1, G, Cg, 1)
                                                 * jnp.eye(Cg, dtype=F32)[None, None, None])
    NS, GS = S5_SLABS, S5_SLAB_GROUPS
    k6 = (kt + dmat).reshape(LC, LC, NS, GS, Cg, Cg)
    kmat = k6.transpose(2, 0, 3, 5, 1, 4).reshape(NS, LC * 128, LC * Cg)
    rr, ri = pr[LC - 1 - s_idx], pi[LC - 1 - s_idx]
    e_re = rr[..., None] * bb_re[None] - ri[..., None] * bb_im[None]
    e_im = rr[..., None] * bb_im[None] + ri[..., None] * bb_re[None]
    e6 = jnp.stack([e_re, e_im]).reshape(2, LC, NS, GS, P, Cg)
    emat = e6.transpose(2, 1, 3, 5, 0, 4).reshape(NS, LC * 128, 2 * P)
    f6 = jnp.stack([cr[1:], ci[1:]]).reshape(2, LC, NS, GS, Cg, P)
    fmat = f6.transpose(2, 0, 3, 5, 1, 4).reshape(NS, 2 * GS * P, LC * Cg)
    lr, li = pr[LC].reshape(NS, 1, GS * P), pi[LC].reshape(NS, 1, GS * P)
    lam_a = jnp.concatenate([lr, lr], axis=-1)
    lam_b = jnp.concatenate([-li, li], axis=-1)
    return kmat.astype(BF16), emat.astype(BF16), fmat.astype(BF16), lam_a, lam_b


def _s5_expand(src_ref, dst_ref, col_of, row_group, col_group):
    n_src, n_dst = src_ref.shape[1], dst_ref.shape[1]
    sr, sc = _iota2((n_src, n_dst))
    select = jnp.where(sr == col_of(sc), 1.0, 0.0).astype(BF16)
    step = 256
    for i in range(src_ref.shape[0] // step):
        rs = slice(i * step, (i + 1) * step)
        r, c = _iota2((step, n_dst))
        same = row_group(r + i * step) == col_group(c)
        full = jnp.dot(src_ref[rs, :], select, preferred_element_type=F32)
        dst_ref[rs, :] = jnp.where(same, full, 0.0).astype(BF16)


def _s5_kernel(u_ref, kc_ref, ec_ref, fc_ref, la_ref, lb_ref, y_ref, k_ref, e_ref, f_ref, e_s, hs_s, h_s, *, B):
    @pl.when(pl.program_id(1) == 0)
    def _():
        h_s[...] = jnp.zeros_like(h_s)
        gs, cg, p = S5_SLAB_GROUPS, S5_GROUP_CH, S5_STATE
        lg = lambda n: n.bit_length() - 1
        lane_group = lambda i: (i >> lg(cg)) & (gs - 1)
        state_group = lambda i: (i >> lg(p)) & (gs - 1)
        frame_ch = lambda c: ((c >> lg(gs * cg)) << lg(cg)) | (c & (cg - 1))
        part_state = lambda c: ((c >> lg(gs * p)) << lg(p)) | (c & (p - 1))
        _s5_expand(kc_ref, k_ref, frame_ch, lane_group, lane_group)
        _s5_expand(ec_ref, e_ref, part_state, lane_group, state_group)
        _s5_expand(fc_ref, f_ref, frame_ch, state_group, lane_group)

    u = u_ref[...]
    e_s[...] = jnp.dot(u, e_ref[...], preferred_element_type=F32)
    la, lb = la_ref[...], lb_ref[...]
    half = h_s.shape[1] // 2
    h = h_s[...]
    for kc in range(u.shape[0] // B):
        rs = slice(kc * B, (kc + 1) * B)
        hs_s[rs, :] = h
        swapped = jnp.concatenate([h[:, half:], h[:, :half]], axis=1)
        h = la * h + lb * swapped + e_s[rs, :]
    h_s[...] = h
    y_ref[...] = (jnp.dot(u, k_ref[...], preferred_element_type=F32)
                  + jnp.dot(hs_s[...].astype(BF16), f_ref[...], preferred_element_type=F32)).astype(BF16)


def _s5(u4, mats, B, L):
    kmat, emat, fmat, lam_a, lam_b = mats
    NS, LC = S5_SLABS, S5_CHUNK
    NK = L // LC
    assert L % LC == 0 and (NK <= S5_TILE_CHUNKS or NK % S5_TILE_CHUNKS == 0) and B % 8 == 0, (B, L)
    R, W, S2 = NK * B, LC * 128, 2 * S5_SLAB_GROUPS * S5_STATE
    ug = u4.reshape(NS, B, NK, W).transpose(0, 2, 1, 3).reshape(NS, R, W)
    TR = B * min(NK, S5_TILE_CHUNKS)
    rows = pl.BlockSpec((None, TR, W), lambda s, r: (s, r, 0))
    per_s = lambda a: pl.BlockSpec((None,) + a.shape[1:], lambda s, r: (s, 0, 0), pipeline_mode=pl.Buffered(1))
    y = pl.pallas_call(
        functools.partial(_s5_kernel, B=B),
        grid=(NS, R // TR),
        in_specs=[rows] + [per_s(a) for a in (kmat, emat, fmat, lam_a, lam_b)],
        out_specs=rows,
        out_shape=jax.ShapeDtypeStruct((NS, R, W), BF16),
        scratch_shapes=[pltpu.VMEM((W, W), BF16), pltpu.VMEM((W, S2), BF16), pltpu.VMEM((S2, W), BF16),
                        pltpu.VMEM((TR, S2), F32), pltpu.VMEM((TR, S2), F32), pltpu.VMEM((B, S2), F32)],
        compiler_params=_cparams("parallel", "arbitrary"),
        name="s5",
    )(ug, kmat, emat, fmat, lam_a, lam_b)
    return y.reshape(NS, NK, B, W).transpose(0, 2, 1, 3).reshape(NS, B * L, 128)


def _post_kernel(a_ref, ah_ref, b_ref, bh_ref, x_ref, xh_ref, wt_ref, wb_ref, g1_ref, beta1_ref, *rest,
                 even, tiles_per_seq, alpha):
    if even:
        wg_ref, bg_ref, *rest = rest
    wu_ref, cw_ref, cb_ref, wd_ref, g2_ref, beta2_ref, o_ref, act_s = rest
    tm = x_ref.shape[0]
    first = (pl.program_id(0) % tiles_per_seq) == 0
    a = jnp.concatenate([ah_ref[...], a_ref[...]], axis=0)
    x_in = jnp.concatenate([xh_ref[...], x_ref[...]], axis=0)
    if even:
        y = jnp.concatenate([jnp.concatenate([bh_ref[s], b_ref[s]], axis=0) for s in range(S5_SLABS)], axis=1)
        z = jax.nn.gelu(y.astype(F32))
        second = z * jax.nn.sigmoid(_mm(z, wg_ref[...]) + bg_ref[...])
    else:
        second = jnp.concatenate([bh_ref[...], b_ref[...]], axis=0)
    mix = _mm(a, wt_ref[...]) + _mm(second, wb_ref[...])
    x1e = _layer_norm(alpha * x_in + mix, g1_ref[...], beta1_ref[...])
    row = lax.broadcasted_iota(jnp.int32, (HALO + tm, 1), 0)
    x1e = jnp.where((row < HALO) & first, 0.0, x1e)
    x1 = x1e[HALO:]
    xe = x1e.astype(BF16)
    xb = xe[HALO:]

    def up(c):
        ue = jnp.dot(xe, wu_ref[:, c * FFN_TILE:(c + 1) * FFN_TILE], preferred_element_type=F32)
        gate = jnp.dot(xb, wu_ref[:, D_FF + c * FFN_TILE:D_FF + (c + 1) * FFN_TILE], preferred_element_type=F32)
        return ue, gate

    n_tiles = D_FF // FFN_TILE
    nxt = up(0)
    for c in range(n_tiles):
        cs = slice(c * FFN_TILE, (c + 1) * FFN_TILE)
        ue, gate = nxt
        if c + 1 < n_tiles:
            nxt = up(c + 1)
        cw = cw_ref[:, cs]
        u = (cw[2:3] * ue[HALO:] + cw[1:2] * _shift_rows(ue, 1)[HALO:] + cw[0:1] * _shift_rows(ue, 2)[HALO:]
             + cb_ref[:, cs])
        act_s[:, cs] = (jax.nn.gelu(u) * gate).astype(BF16)
    ffn = jnp.dot(act_s[...], wd_ref[...], preferred_element_type=F32)
    o_ref[...] = _layer_norm(alpha * x1 + ffn, g2_ref[...], beta2_ref[...])


def _post(a, b, x, L, w_out, ln1_g, ln1_b, w_up, conv_w, conv_b, w_down, ln2_g, ln2_b, alpha, glu=None, tm=1024):
    T = x.shape[0]
    tm = min(tm, L)
    assert L % tm == 0 and tm % HALO == 0, (L, tm)
    wo = w_out.astype(BF16)
    row1 = lambda v, n: v.reshape(1, n).astype(F32)
    slab = lambda rows, scale: pl.BlockSpec((S5_SLABS, rows, 128), scale)
    b_specs = [_rows(tm, HALF), _halo_rows(tm, HALF)]
    if glu is not None:
        b_specs = [slab(tm, lambda i: (0, i, 0)),
                   slab(HALO, lambda i: (0, jnp.maximum(i * (tm // HALO) - 1, 0), 0))]
    args = [a, a, b, b, x, x, wo[:HALF], wo[HALF:], row1(ln1_g, D_MODEL), row1(ln1_b, D_MODEL)]
    specs = ([_rows(tm, HALF), _halo_rows(tm, HALF)] + b_specs + [_rows(tm, D_MODEL), _halo_rows(tm, D_MODEL)]
             + [_resident(t.shape) for t in args[6:]])
    tail = [w_up.astype(BF16), conv_w.astype(F32), row1(conv_b, D_FF), w_down.astype(BF16),
            row1(ln2_g, D_MODEL), row1(ln2_b, D_MODEL)]
    if glu is not None:
        tail = [glu[0].astype(BF16), row1(glu[1], HALF)] + tail
    return pl.pallas_call(
        functools.partial(_post_kernel, even=glu is not None, tiles_per_seq=L // tm, alpha=alpha),
        grid=(T // tm,),
        in_specs=specs + [_resident(t.shape) for t in tail],
        out_specs=_rows(tm, D_MODEL),
        out_shape=jax.ShapeDtypeStruct((T, D_MODEL), F32),
        scratch_shapes=[pltpu.VMEM((tm, D_FF), BF16)],
        compiler_params=_cparams("parallel"),
        name="post",
    )(*args, *tail)


_LORA_PAD = (128, 128, 256)
_RWKV_PAD = 3 * HALF + sum(_LORA_PAD)


def _odd_pre_kernel(x_ref, xh_ref, wrk_ref, mu_ref, w2_ref, a2_ref, g2_ref, vec_ref, bd_ref,
                    wqk_ref, cw_ref, cb_ref, wv_ref, wz_ref, wif_ref, gb_ref,
                    r_ref, k_ref, v_ref, lw_ref, kk_ref, ka_ref, g_ref, bo_ref,
                    mq_ref, mk_ref, mv_ref, mz_ref, gate_ref, *, tiles_per_seq):
    tm = x_ref.shape[0]
    first = (pl.program_id(0) % tiles_per_seq) == 0
    xh = jnp.where(first, 0.0, xh_ref[...])
    xe = jnp.concatenate([xh, x_ref[...]], axis=0).astype(BF16)
    xb = xe[HALO:]

    pe = jnp.dot(xe, wrk_ref[...], preferred_element_type=F32)
    qke = jnp.dot(xe, wqk_ref[...], preferred_element_type=F32)
    mv_ref[...] = jnp.dot(xb, wv_ref[...], preferred_element_type=F32).astype(BF16)
    mz_ref[...] = jnp.dot(xb, wz_ref[...], preferred_element_type=F32).astype(mz_ref.dtype)
    pre = jnp.dot(xb, wif_ref[...], preferred_element_type=F32)[:, :2 * MLSTM_HEADS] + gb_ref[...]

    cur, prev = pe[HALO:], _shift_rows(pe, 1)[HALO:]
    p = cur + (prev - cur) * mu_ref[...]
    r, k, v = p[:, :HALF], p[:, HALF:2 * HALF], p[:, 2 * HALF:3 * HALF]
    o0 = 3 * HALF
    wd = p[:, o0:o0 + _LORA_PAD[0]]
    ad = p[:, o0 + _LORA_PAD[0]:o0 + _LORA_PAD[0] + _LORA_PAD[1]]
    gd = p[:, o0 + _LORA_PAD[0] + _LORA_PAD[1]:]
    w0, a0, k_k, k_a, r_k = (vec_ref[i:i + 1, :] for i in range(5))
    wlog = -jax.nn.softplus(-(w0 + _mm(jnp.tanh(wd), w2_ref[...]))) - 0.5
    lw_ref[...] = -jnp.exp(wlog)
    a = jax.nn.sigmoid(a0 + _mm(ad, a2_ref[...]))
    g_ref[...] = _mm(jax.nn.sigmoid(gd), g2_ref[...]).astype(g_ref.dtype)
    kk = k * k_k
    ss = _mm_exact_rhs(kk * kk, bd_ref[...], terms=2)
    kk = kk * lax.rsqrt(jnp.maximum(ss, 1e-24))
    kmod = k * (1.0 + (a - 1.0) * k_a)
    bo_ref[...] = (_mm_exact_rhs(r * kmod * r_k, bd_ref[...], terms=2) * v).astype(bo_ref.dtype)
    r_ref[...] = r.astype(r_ref.dtype)
    k_ref[...] = kmod.astype(k_ref.dtype)
    v_ref[...] = v.astype(v_ref.dtype)
    kk_ref[...] = kk.astype(kk_ref.dtype)
    ka_ref[...] = (kk * a).astype(ka_ref.dtype)

    cw = cw_ref[...]
    qk = cb_ref[...]
    for j in range(MLSTM_CONV):
        qk = qk + cw[j:j + 1] * _shift_rows(qke, MLSTM_CONV - 1 - j)[HALO:]
    qk = jax.nn.silu(qk)
    mq_ref[...] = qk[:, :HALF].astype(mq_ref.dtype)
    mk_ref[...] = (qk[:, HALF:] * (MLSTM_DH ** -0.5)).astype(BF16)
    is_i = lax.broadcasted_iota(jnp.int32, pre.shape, 1) < MLSTM_HEADS
    gate_ref[...] = jnp.where(is_i, pre, jax.nn.log_sigmoid(pre))


def _head_block_ones(width, head):
    idx = jnp.arange(width) // head
    return (idx[:, None] == idx[None, :]).astype(BF16)


def _odd_pre(x, L, w_in, mu, w0, w2, a0, a2, g2, k_k, k_a, r_k, conv_w, conv_b, ib, fb, tm=512):
    T = x.shape[0]
    tm = min(tm, L)
    assert L % tm == 0 and tm % HALO == 0, (L, tm)
    wb = w_in.astype(BF16)
    o = 3 * HALF
    sizes = (RWKV_DECAY_LORA, RWKV_AAA_LORA, RWKV_GATE_LORA)

    def pad_lora(m, axis):
        parts, s = [], o
        for sz, pd in zip(sizes, _LORA_PAD):
            piece = lax.slice_in_dim(m, s, s + sz, axis=axis)
            widths = [(0, 0)] * m.ndim
            widths[axis] = (0, pd - sz)
            parts.append(jnp.pad(piece, widths))
            s += sz
        return jnp.concatenate([lax.slice_in_dim(m, 0, o, axis=axis)] + parts, axis=axis)

    rwkv_proj = o + sum(sizes)
    wrk = pad_lora(wb[:, :rwkv_proj], 1)
    mu_p = pad_lora(mu.astype(F32).reshape(1, -1), 1)
    padr = lambda m, rows: jnp.pad(m.astype(BF16), ((0, rows - m.shape[0]), (0, 0)))
    w2p, a2p, g2p = padr(w2, _LORA_PAD[0]), padr(a2, _LORA_PAD[1]), padr(g2, _LORA_PAD[2])
    vecs = jnp.stack([w0, a0, k_k, k_a, r_k.reshape(HALF)]).astype(F32)
    vecs = jnp.pad(vecs, ((0, 8 - vecs.shape[0]), (0, 0)))
    bd = _head_block_ones(HALF, RWKV_N)
    wm = wb[:, rwkv_proj:]
    wqk, wmv, wmz = wm[:, :2 * HALF], wm[:, 2 * HALF:3 * HALF], wm[:, 3 * HALF:4 * HALF]
    wif = jnp.pad(wm[:, 4 * HALF:], ((0, 0), (0, 128 - 2 * MLSTM_HEADS)))
    gbias = jnp.concatenate([ib, fb]).astype(F32).reshape(1, 2 * MLSTM_HEADS)
    args = [x, x, wrk, mu_p, w2p, a2p, g2p, vecs, bd, wqk, conv_w.astype(F32),
            conv_b.astype(F32).reshape(1, 2 * HALF), wmv, wmz, wif, gbias]
    specs = [_rows(tm, D_MODEL), _halo_rows(tm, D_MODEL)] + [_resident(a.shape) for a in args[2:]]
    f_half = jax.ShapeDtypeStruct((T, HALF), F32)
    b_half = jax.ShapeDtypeStruct((T, HALF), BF16)
    out_shape = [b_half] * 3 + [f_half] + [b_half] * 8 + [jax.ShapeDtypeStruct((T, 2 * MLSTM_HEADS), F32)]
    out_specs = [_rows(tm, HALF)] * 12 + [_rows(tm, 2 * MLSTM_HEADS)]
    return pl.pallas_call(
        functools.partial(_odd_pre_kernel, tiles_per_seq=L // tm),
        grid=(T // tm,),
        in_specs=specs,
        out_specs=out_specs,
        out_shape=out_shape,
        compiler_params=_cparams("parallel"),
        name="odd_pre",
    )(*args)


def _rwkv_kernel(r_ref, k_ref, v_ref, lw_ref, kk_ref, ka_ref, g_ref, bo_ref, lng_ref, lnb_ref, o_ref, m_s):
    C, N, H = CHUNK, RWKV_N, RWKV_HEADS
    rows = range(r_ref.shape[0])
    units = [(b, h) for b in rows for h in range(H)]

    @pl.when(pl.program_id(1) == 0)
    def _():
        m_s[...] = jnp.zeros_like(m_s)

    ri, ci = _iota2((C, C))
    incl, strict, eye = ci <= ri, ci < ri, ci == ri
    ri2, ci2 = _iota2((C, 2 * C))
    incl2 = (ci2 & (C - 1)) <= ri2
    zeros = jnp.zeros((C, N), F32)
    mm = functools.partial(_mmp, passes=RWKV_PASSES)
    al, rt, bt, kt, bp, kp, gam, v = ({} for _ in range(8))
    for b in rows:
        lw = lw_ref[b]
        cs = _mm_exact_lhs(incl.astype(BF16), lw)
        cend = cs[C - 1:C, :]
        e_neg = jnp.exp(-cs)
        e_rem = jnp.exp(cend - cs)
        r, k, vv, kk, ka = (t[b].astype(F32) for t in (r_ref, k_ref, v_ref, kk_ref, ka_ref))
        al_b = -kk * jnp.exp(cs - lw)
        rt_b = r * jnp.exp(cs)
        gam_b = jnp.exp(cend)
        bt_b, kt_b, bp_b, kp_b = ka * e_neg, k * e_neg, ka * e_rem, k * e_rem
        for h in range(H):
            s = slice(N * h, N * (h + 1))
            u = (b, h)
            al[u], rt[u], v[u], gam[u] = al_b[:, s], rt_b[:, s], vv[:, s], gam_b[:, s]
            bt[u], kt[u], bp[u], kp[u] = bt_b[:, s], kt_b[:, s], bp_b[:, s], kp_b[:, s]
    pm = {u: mm(jnp.concatenate([al[u], rt[u]], 0), jnp.concatenate([bt[u], kt[u]], 0), 1, 1) for u in units}
    a_ab = {u: jnp.where(strict, pm[u][:C, :C], 0.0) for u in units}
    a_ak = {u: jnp.where(strict, pm[u][:C, C:], 0.0) for u in units}
    a_r = {u: jnp.where(incl2, pm[u][C:, :], 0.0) for u in units}
    w = {u: jnp.concatenate([al[u], mm(a_ak[u], v[u])], 1) for u in units}
    npow = a_ab
    levels = int(math.log2(C))
    for lvl in range(levels):
        if lvl < levels - 1:
            y = {u: mm(npow[u], jnp.concatenate([w[u], npow[u]], 1)) for u in units}
            w = {u: w[u] + y[u][:, :2 * N] for u in units}
            npow = {u: y[u][:, 2 * N:] for u in units}
        else:
            w = {u: w[u] + mm(npow[u], w[u]) for u in units}
    zv = {u: jnp.concatenate([zeros, v[u]], 1) for u in units}
    qt = {u: mm(a_r[u], jnp.concatenate([w[u], zv[u]], 0)) for u in units}
    qb = {u: mm(bp[u], w[u], 0, 0) + mm(kp[u], zv[u], 0, 0) for u in units}
    m0 = {u: m_s[u[0], u[1]] for u in units}
    o_h = {u: mm(rt[u] + qt[u][:, :N], m0[u]) + qt[u][:, N:] for u in units}
    m_new = {u: _mmp(qb[u][:, :N] + jnp.where(eye, gam[u], 0.0), m0[u], passes=RWKV_STATE_PASSES) + qb[u][:, N:]
             for u in units}
    for u in units:
        m_s[u[0], u[1]] = m_new[u]
    om = {u: jnp.mean(o_h[u], -1, keepdims=True) for u in units}
    d = {u: o_h[u] - om[u] for u in units}
    ov = {u: jnp.mean(d[u] * d[u], -1, keepdims=True) for u in units}
    for b in rows:
        outs = [d[(b, h)] * lax.rsqrt(ov[(b, h)] + RWKV_GN_EPS) for h in range(H)]
        on = jnp.concatenate(outs, 1) * lng_ref[...] + lnb_ref[...]
        o_ref[b] = ((on + bo_ref[b].astype(F32)) * g_ref[b].astype(F32)).astype(BF16)


def _rwkv(r, k, v, lw, kk, ka, g, bonus, ln_g, ln_b, B, L):
    NC = L // CHUNK
    assert L % CHUNK == 0, L
    RB = SEQ_ROWS if B % SEQ_ROWS == 0 else 1
    seq3 = lambda t: t.reshape(B, L, HALF)
    blk = pl.BlockSpec((RB, CHUNK, HALF), lambda b, c: (b, c, 0))
    vec = pl.BlockSpec((1, HALF), lambda b, c: (0, 0))
    out = pl.pallas_call(
        _rwkv_kernel,
        grid=(B // RB, NC),
        in_specs=[blk] * 8 + [vec, vec],
        out_specs=blk,
        out_shape=jax.ShapeDtypeStruct((B, L, HALF), BF16),
        scratch_shapes=[pltpu.VMEM((RB, RWKV_HEADS, RWKV_N, RWKV_N), F32)],
        compiler_params=_cparams("parallel", "arbitrary"),
        name="rwkv7",
    )(*(seq3(t) for t in (r, k, v, lw, kk, ka, g, bonus)),
      ln_g.reshape(1, HALF).astype(F32), ln_b.reshape(1, HALF).astype(F32))
    return out.reshape(B * L, HALF)


def _mlstm_kernel(q_ref, k_ref, v_ref, z_ref, gc_ref, gr_ref, lng_ref, skip_ref, o_ref, c_s, m_s):
    C, H, DH = CHUNK, MLSTM_HEADS, MLSTM_DH
    rows = range(q_ref.shape[0])
    units = [(b, h) for b in rows for h in range(H)]

    @pl.when(pl.program_id(1) == 0)
    def _():
        c_s[...] = jnp.zeros_like(c_s)
        m_s[...] = jnp.zeros_like(m_s)

    ri, ci = _iota2((C, C))
    incl = ci <= ri
    lower, upper = incl.astype(BF16), (ri <= ci).astype(BF16)
    lane = lax.broadcasted_iota(jnp.int32, (1, DH), 1)
    ones_blk = jnp.broadcast_to(jnp.where(lane == 0, 1.0, 0.0).astype(BF16), (C, DH))
    q_all, qb, kh, vaug, i_col, i_row, b_col, b_row, m_prev = ({} for _ in range(9))
    for b in rows:
        gc, gr = gc_ref[b], gr_ref[b]
        b_cols = _mm_exact_lhs(lower, gc)
        b_rows = _mm_exact_rhs(gr, upper)
        q_all[b] = q_ref[b].astype(F32)
        for h in range(H):
            u, s = (b, h), slice(DH * h, DH * (h + 1))
            qb[u], kh[u] = q_ref[b, :, s], k_ref[b, :, s]
            vaug[u] = jnp.concatenate([v_ref[b, :, s], ones_blk], 1)
            i_col[u], i_row[u] = gc[:, h:h + 1], gr[h:h + 1, :]
            b_col[u], b_row[u] = b_cols[:, H + h:H + h + 1], b_rows[H + h:H + h + 1, :]
            m_prev[u] = m_s[b, h:h + 1, 0:1]
    c_old = {u: c_s[u[0], u[1]] for u in units}
    qk = {u: _dg(qb[u], kh[u], 1, 1) for u in units}
    qc = {u: jnp.dot(qb[u], c_old[u].astype(BF16), preferred_element_type=F32) for u in units}
    dmat = {u: jnp.where(incl, b_col[u] - b_row[u] + i_row[u], NEG_BIG) for u in units}
    inter = {u: b_col[u] + m_prev[u] for u in units}
    m_t = {u: jnp.maximum(inter[u], jnp.max(dmat[u], -1, keepdims=True)) for u in units}
    s = {u: (qk[u] * jnp.exp(dmat[u] - m_t[u])).astype(BF16) for u in units}
    sv = {u: jnp.dot(s[u], vaug[u], preferred_element_type=F32) for u in units}
    b_end = {u: b_col[u][C - 1:C, :] for u in units}
    m_new = {u: jnp.maximum(b_end[u] + m_prev[u], jnp.max(b_end[u] - b_row[u] + i_row[u], -1, keepdims=True))
             for u in units}
    kw = {u: (jnp.exp(b_end[u] - b_col[u] + i_col[u] - m_new[u]) * kh[u].astype(F32)).astype(BF16) for u in units}
    kv = {u: _dg(kw[u], vaug[u], 0, 0) for u in units}
    for u in units:
        b, h = u
        c_s[b, h] = jnp.exp(b_end[u] + m_prev[u] - m_new[u]) * c_old[u] + kv[u]
        m_s[b, h:h + 1, :] = jnp.broadcast_to(m_new[u], (1, 128))
    nd = {u: jnp.exp(inter[u] - m_t[u]) * qc[u] + sv[u] for u in units}
    den = {u: jnp.maximum(jnp.abs(nd[u][:, DH:DH + 1]), jnp.exp(-m_t[u])) for u in units}
    hid = {u: nd[u][:, :DH] / den[u] for u in units}
    hm = {u: jnp.mean(hid[u], -1, keepdims=True) for u in units}
    d = {u: hid[u] - hm[u] for u in units}
    hv = {u: jnp.mean(d[u] * d[u], -1, keepdims=True) for u in units}
    for b in rows:
        outs = [d[(b, h)] * lax.rsqrt(hv[(b, h)] + LN_EPS) for h in range(H)]
        hn = jnp.concatenate(outs, 1) * lng_ref[...]
        o_ref[b] = ((hn + skip_ref[...] * q_all[b]) * jax.nn.silu(z_ref[b].astype(F32))).astype(BF16)


def _mlstm(q, k, v, z, gates, ln_g, skip, B, L):
    NC, H2 = L // CHUNK, 2 * MLSTM_HEADS
    assert L % CHUNK == 0, L
    RB = MLSTM_SEQ_ROWS if B % MLSTM_SEQ_ROWS == 0 else 1
    seq3 = lambda t: t.reshape(B, L, t.shape[-1])
    blk = pl.BlockSpec((RB, CHUNK, HALF), lambda b, c: (b, c, 0))
    vec = pl.BlockSpec((1, HALF), lambda b, c: (0, 0))
    g_rows = gates.reshape(B, NC, CHUNK, H2).transpose(0, 1, 3, 2)
    out = pl.pallas_call(
        _mlstm_kernel,
        grid=(B // RB, NC),
        in_specs=[blk, blk, blk, blk, pl.BlockSpec((RB, CHUNK, H2), lambda b, c: (b, c, 0)),
                  pl.BlockSpec((RB, None, H2, CHUNK), lambda b, c: (b, c, 0, 0)), vec, vec],
        out_specs=blk,
        out_shape=jax.ShapeDtypeStruct((B, L, HALF), BF16),
        scratch_shapes=[pltpu.VMEM((RB, MLSTM_HEADS, MLSTM_DH, 2 * MLSTM_DH), F32),
                        pltpu.VMEM((RB, 8, 128), F32)],
        compiler_params=_cparams("parallel", "arbitrary"),
        name="mlstm",
    )(seq3(q), seq3(k), seq3(v), seq3(z), seq3(gates), g_rows,
      ln_g.reshape(1, HALF).astype(F32), skip.reshape(1, HALF).astype(F32))
    return out.reshape(B * L, HALF)


def kernel(x, ev_w_in, ev_fox_fb, ev_s5_a_re, ev_s5_a_im, ev_s5_b_re, ev_s5_b_im, ev_s5_c_re, ev_s5_c_im, ev_s5_d, ev_s5_log_dt, ev_s5_w_glu, ev_s5_b_glu, ev_w_out, od_w_in, od_rwkv_mu, od_rwkv_w0, od_rwkv_w2, od_rwkv_a0, od_rwkv_a2, od_rwkv_g2, od_rwkv_k_k, od_rwkv_k_a, od_rwkv_r_k, od_rwkv_ln_g, od_rwkv_ln_b, od_mlstm_conv_w, od_mlstm_conv_b, od_mlstm_ib, od_mlstm_fb, od_mlstm_ln_g, od_mlstm_skip, od_w_out, ln1_g, ln1_b, ffn_w_up, ffn_conv_w, ffn_conv_b, ffn_w_down, ln2_g, ln2_b):
    B, L, _ = x.shape
    depth = ln1_g.shape[0]
    alpha = float((2 * depth) ** 0.25)
    h = x.reshape(B * L, D_MODEL).astype(F32)
    for layer in range(depth):
        i = layer // 2
        if layer % 2 == 0:
            qt, k, vt, fg, u = _even_pre(h, ev_w_in[i])
            f_row = _fox_gate(fg.reshape(B, L, FOX_HEADS).transpose(0, 2, 1), ev_fox_fb[i].astype(F32))
            f_col = jnp.pad(f_row.transpose(0, 2, 1).reshape(B * L, FOX_HEADS), ((0, 0), (0, 128 - FOX_HEADS)))
            fox = _fox_attention(qt, k, vt, f_col, f_row)
            mats = _s5_matrices(ev_s5_a_re[i], ev_s5_a_im[i], ev_s5_b_re[i], ev_s5_b_im[i],
                                ev_s5_c_re[i], ev_s5_c_im[i], ev_s5_d[i], ev_s5_log_dt[i])
            y = _s5(u, mats, B, L)
            mixed, glu, w_out = (fox, y), (ev_s5_w_glu[i], ev_s5_b_glu[i]), ev_w_out[i]
        else:
            (r, k, v, lw, kk, ka, g, bonus, mq, mk, mv, mz, gates) = _odd_pre(
                h, L, od_w_in[i], od_rwkv_mu[i], od_rwkv_w0[i], od_rwkv_w2[i], od_rwkv_a0[i], od_rwkv_a2[i],
                od_rwkv_g2[i], od_rwkv_k_k[i], od_rwkv_k_a[i], od_rwkv_r_k[i], od_mlstm_conv_w[i],
                od_mlstm_conv_b[i], od_mlstm_ib[i], od_mlstm_fb[i])
            c = _rwkv(r, k, v, lw, kk, ka, g, bonus, od_rwkv_ln_g[i], od_rwkv_ln_b[i], B, L)
            dm = _mlstm(mq, mk, mv, mz, gates, od_mlstm_ln_g[i], od_mlstm_skip[i], B, L)
            mixed, glu, w_out = (c, dm), None, od_w_out[i]
        h = _post(mixed[0], mixed[1], h, L, w_out, ln1_g[layer], ln1_b[layer], ffn_w_up[layer], ffn_conv_w[layer],
                  ffn_conv_b[layer], ffn_w_down[layer], ln2_g[layer], ln2_b[layer], alpha, glu=glu)
    return h.reshape(B, L, D_MODEL).astype(x.dtype)
```

```python
import functools
import math

import jax
import jax.numpy as jnp
from jax import lax
from jax.experimental import pallas as pl
from jax.experimental.pallas import tpu as pltpu

F32 = jnp.float32
BF16 = jnp.bfloat16
HIGHEST = lax.Precision.HIGHEST

D_MODEL = 1024
HALF = D_MODEL // 2
FOX_HEADS = 8
FOX_DH = HALF // FOX_HEADS
FOX_TILE = 256
FOX_AUG = 256
S5_GROUPS = 32
S5_GROUP_CH = 16
S5_STATE = 64
S5_CHUNK = 16
S5_SLAB_GROUPS = 128 // S5_GROUP_CH
S5_SLABS = S5_GROUPS // S5_SLAB_GROUPS
S5_TILE_CHUNKS = 16
RWKV_HEADS = 8
RWKV_N = HALF // RWKV_HEADS
RWKV_DECAY_LORA = 64
RWKV_AAA_LORA = 64
RWKV_GATE_LORA = 160
RWKV_GN_EPS = 64e-5
RWKV_PASSES = 1
RWKV_STATE_PASSES = 3
MLSTM_HEADS = 4
MLSTM_DH = HALF // MLSTM_HEADS
MLSTM_CONV = 4
CHUNK = 64
SEQ_ROWS = 4
MLSTM_SEQ_ROWS = 4
D_FF = 2816
FFN_CONV = 3
FFN_TILE = 256
LN_EPS = 1e-5
HALO = 16
NEG_BIG = -1e30
V7X_VMEM_LIMIT_BYTES = 60 * 1024 * 1024


def _cparams(*sem):
    return pltpu.CompilerParams(dimension_semantics=sem, vmem_limit_bytes=V7X_VMEM_LIMIT_BYTES)


def _resident(shape):
    nd = len(shape)
    return pl.BlockSpec(shape, lambda *_: (0,) * nd, pipeline_mode=pl.Buffered(1))


def _rows(tm, width):
    return pl.BlockSpec((tm, width), lambda i: (i, 0))


def _halo_rows(tm, width):
    return pl.BlockSpec((HALO, width), lambda i: (jnp.maximum(i * (tm // HALO) - 1, 0), 0))


def _mm(a, b):
    return jnp.dot(a.astype(BF16), b.astype(BF16), preferred_element_type=F32)


def _dg(a, b, ca, cb):
    return lax.dot_general(a, b, (((ca,), (cb,)), ((), ())), preferred_element_type=F32)


def _hi_lo(a):
    hi = a.astype(BF16)
    lo = (a - hi.astype(F32)).astype(BF16)
    return hi, lo


def _mm3(a, b, ca=1, cb=0):
    ah, al = _hi_lo(a)
    bh, bl = _hi_lo(b)
    return _dg(ah, bh, ca, cb) + _dg(ah, bl, ca, cb) + _dg(al, bh, ca, cb)


def _mmp(a, b, ca=1, cb=0, passes=1):
    if passes == 3:
        return _mm3(a, b, ca, cb)
    return _dg(a.astype(BF16), b.astype(BF16), ca, cb)


def _split3(a):
    a1 = a.astype(BF16)
    r1 = a - a1.astype(F32)
    a2 = r1.astype(BF16)
    a3 = (r1 - a2.astype(F32)).astype(BF16)
    return a1, a2, a3


def _mm_exact_rhs(a, b01, terms=3):
    parts = _split3(a) if terms == 3 else _hi_lo(a)
    out = jnp.dot(parts[0], b01, preferred_element_type=F32)
    for part in parts[1:]:
        out = out + jnp.dot(part, b01, preferred_element_type=F32)
    return out


def _mm_exact_lhs(a01, b):
    b1, b2, b3 = _split3(b)
    return (jnp.dot(a01, b1, preferred_element_type=F32) + jnp.dot(a01, b2, preferred_element_type=F32)
            + jnp.dot(a01, b3, preferred_element_type=F32))


def _layer_norm(x, g, b):
    mu = jnp.mean(x, -1, keepdims=True)
    d = x - mu
    var = jnp.mean(d * d, -1, keepdims=True)
    return d * lax.rsqrt(var + LN_EPS) * g + b


def _shift_rows(x, n):
    return x if n == 0 else pltpu.roll(x, n, 0)


def _iota2(shape):
    return lax.broadcasted_iota(jnp.int32, shape, 0), lax.broadcasted_iota(jnp.int32, shape, 1)


def _even_pre_kernel(x_ref, wqt_ref, wk_ref, wvt_ref, wf_ref, wu_ref, qt_ref, k_ref, vt_ref, fg_ref, u_ref):
    xb = x_ref[...].astype(BF16)
    qt_ref[...] = (_dg(wqt_ref[...], xb, 1, 1) * (FOX_DH ** -0.5)).astype(BF16)
    vt_ref[...] = _dg(wvt_ref[...], xb, 1, 1).astype(BF16)
    k_ref[...] = jnp.dot(xb, wk_ref[...], preferred_element_type=F32).astype(BF16)
    u = jnp.dot(xb, wu_ref[...], preferred_element_type=F32).astype(BF16)
    for s in range(S5_SLABS):
        u_ref[s] = u[:, 128 * s:128 * (s + 1)]
    fg_ref[...] = jnp.dot(xb, wf_ref[...], preferred_element_type=F32)[:, :FOX_HEADS]


def _even_pre(x, w_in, tm=1024):
    T = x.shape[0]
    wb = w_in.astype(BF16)
    wqt, wk, wvt = wb[:, :HALF].T, wb[:, HALF:2 * HALF], wb[:, 2 * HALF:3 * HALF].T
    wf = jnp.pad(wb[:, 3 * HALF:3 * HALF + FOX_HEADS], ((0, 0), (0, 128 - FOX_HEADS)))
    wu = wb[:, 3 * HALF + FOX_HEADS:]
    half_out = jax.ShapeDtypeStruct((T, HALF), BF16)
    half_t = jax.ShapeDtypeStruct((HALF, T), BF16)
    cols = pl.BlockSpec((HALF, tm), lambda i: (0, i))
    return pl.pallas_call(
        _even_pre_kernel,
        grid=(T // tm,),
        in_specs=[_rows(tm, D_MODEL), _resident(wqt.shape), _resident(wk.shape), _resident(wvt.shape),
                  _resident(wf.shape), _resident(wu.shape)],
        out_specs=[cols, _rows(tm, HALF), cols, _rows(tm, FOX_HEADS),
                   pl.BlockSpec((S5_SLABS, tm, 128), lambda i: (0, i, 0))],
        out_shape=[half_t, half_out, half_t, jax.ShapeDtypeStruct((T, FOX_HEADS), F32),
                   jax.ShapeDtypeStruct((S5_SLABS, T, 128), BF16)],
        compiler_params=_cparams("parallel"),
        name="even_pre",
    )(x, wqt, wk, wvt, wf, wu)


def _fox_gate_kernel(fg_ref, fb_ref, o_ref, *, L):
    ls = jax.nn.log_sigmoid(fg_ref[...] + fb_ref[...])
    r, c = _iota2((128, 128))
    tri = (r <= c).astype(BF16)
    carry = jnp.zeros((FOX_HEADS, 1), F32)
    for j in range(L // 128):
        cum = _mm_exact_rhs(ls[:, j * 128:(j + 1) * 128], tri) + carry
        o_ref[:, j * 128:(j + 1) * 128] = cum
        carry = cum[:, 127:128]


def _fox_gate(fg_t, fb):
    B, H, L = fg_t.shape
    return pl.pallas_call(
        functools.partial(_fox_gate_kernel, L=L),
        grid=(B,),
        in_specs=[pl.BlockSpec((None, H, L), lambda b: (b, 0, 0)), _resident((H, 1))],
        out_specs=pl.BlockSpec((None, H, L), lambda b: (b, 0, 0)),
        out_shape=jax.ShapeDtypeStruct((B, H, L), F32),
        compiler_params=_cparams("parallel"),
        name="fox_gate",
    )(fg_t, fb.reshape(H, 1))


def _fox_kernel(qt_ref, k_ref, vt_ref, fc_ref, fr_ref, o_ref, qa_s, ka_s, m_s, l_s, acc_s, st_s, *, L):
    TQ = TK = FOX_TILE
    H, DH, KA = FOX_HEADS, FOX_DH, FOX_AUG
    heads = range(H)

    t1, t2, t3 = (t.astype(F32) for t in _split3(fr_ref[...]))
    r16 = lax.broadcasted_iota(jnp.int32, (16, 1), 0)
    upper = lax.broadcasted_iota(jnp.int32, (2 * DH, 1), 0) < DH
    for h in heads:
        p, hh = divmod(h, 2)
        qpair = qt_ref[2 * DH * p:2 * DH * (p + 1), :]
        keep = upper if hh == 0 else jnp.logical_not(upper)
        qa_s[h, 0:2 * DH, :] = jnp.where(keep, qpair, jnp.zeros_like(qpair))
        ones_rows = (r16 >= 3 + 3 * hh) & (r16 < 6 + 3 * hh)
        blk = jnp.where(r16 == 0, t1[h:h + 1], jnp.where(r16 == 1, t2[h:h + 1], jnp.where(
            r16 == 2, t3[h:h + 1], jnp.where(ones_rows, 1.0, 0.0))))
        qa_s[h, 2 * DH:2 * DH + 16, :] = blk.astype(BF16)
        qa_s[h, 2 * DH + 16:KA, :] = jnp.zeros((KA - 2 * DH - 16, L), BF16)
    c1, c2, c3 = _split3(fc_ref[...])
    rs, cs_ = _iota2((128, 128))
    lane = lax.broadcasted_iota(jnp.int32, (1, 128), 1)
    ones3 = jnp.where(lane < 3, 1.0, 0.0)
    for p in range(H // 2):
        def sel(i, p=p):
            hit = ((rs == 2 * p) & (cs_ == 3 + i)) | ((rs == 2 * p + 1) & (cs_ == 6 + i))
            return jnp.where(hit, -1.0, 0.0).astype(BF16)
        aug = (jnp.dot(c1, sel(0), preferred_element_type=F32) + jnp.dot(c2, sel(1), preferred_element_type=F32)
               + jnp.dot(c3, sel(2), preferred_element_type=F32) + ones3)
        ka_s[:, KA * p:KA * p + 2 * DH] = k_ref[:, 2 * DH * p:2 * DH * (p + 1)]
        ka_s[:, KA * p + 2 * DH:KA * (p + 1)] = aug.astype(BF16)

    ri, ci = _iota2((TK, TQ))
    visible = ri <= ci

    def q_block(qi, _):
        q0 = pl.multiple_of(qi * TQ, TQ)
        m_s[...] = jnp.full(m_s.shape, NEG_BIG, F32)
        l_s[...] = jnp.zeros(l_s.shape, F32)
        acc_s[...] = jnp.zeros(acc_s.shape, F32)

        def scores(slot, k0, masked):
            kt = [ka_s[pl.ds(k0, TK), KA * p:KA * (p + 1)] for p in range(H // 2)]
            for h in heads:
                s = jnp.dot(kt[h // 2], qa_s[h, :, pl.ds(q0, TQ)], preferred_element_type=F32)
                st_s[slot, h] = jnp.where(visible, s, NEG_BIG) if masked else s

        def consume(slot, k0):
            k0 = pl.multiple_of(k0, TK)
            st = [st_s[slot, h] for h in heads]
            m_old = [m_s[h:h + 1, :] for h in heads]
            m_new = [jnp.maximum(m_old[h], jnp.max(st[h], 0, keepdims=True)) for h in heads]
            pt = [jnp.exp(st[h] - m_new[h]) for h in heads]
            pv = [jnp.dot(vt_ref[DH * h:DH * (h + 1), pl.ds(k0, TK)], pt[h].astype(BF16),
                          preferred_element_type=F32) for h in heads]
            for h in heads:
                a = jnp.exp(m_old[h] - m_new[h])
                m_s[h:h + 1, :] = m_new[h]
                l_s[h:h + 1, :] = a * l_s[h:h + 1, :] + jnp.sum(pt[h], 0, keepdims=True)
                acc_s[DH * h:DH * (h + 1), :] = a * acc_s[DH * h:DH * (h + 1), :] + pv[h]

        def pair_step(jj, k_prev):
            ka = pl.multiple_of(2 * jj * TK, TK)
            kb = pl.multiple_of(ka + TK, TK)
            scores(1, ka, False)
            consume(0, k_prev)
            scores(0, kb, False)
            consume(1, ka)
            return kb

        scores(0, q0, True)
        k_prev = lax.fori_loop(0, qi // 2, pair_step, q0)

        @pl.when(qi % 2 == 1)
        def _():
            k_odd = pl.multiple_of((qi - 1) * TK, TK)
            scores(1, k_odd, False)
            consume(0, k_prev)
            consume(1, k_odd)

        @pl.when(qi % 2 == 0)
        def _():
            consume(0, k_prev)
        for p in range(H // 2):
            o_pair = jnp.concatenate([acc_s[DH * h:DH * (h + 1), :] / l_s[h:h + 1, :] for h in (2 * p, 2 * p + 1)], 0)
            o_ref[pl.ds(q0, TQ), 2 * DH * p:2 * DH * (p + 1)] = o_pair.T.astype(BF16)
        return 0

    lax.fori_loop(0, L // TQ, q_block, 0)


def _fox_attention(qt, k, vt, f_col, f_row):
    B, H, L = f_row.shape
    assert L % FOX_TILE == 0, L
    seq = pl.BlockSpec((L, HALF), lambda b: (b, 0))
    seq_t = pl.BlockSpec((HALF, L), lambda b: (0, b))
    return pl.pallas_call(
        functools.partial(_fox_kernel, L=L),
        grid=(B,),
        in_specs=[seq_t, seq, seq_t, pl.BlockSpec((L, 128), lambda b: (b, 0)),
                  pl.BlockSpec((None, H, L), lambda b: (b, 0, 0))],
        out_specs=seq,
        out_shape=jax.ShapeDtypeStruct((B * L, HALF), BF16),
        scratch_shapes=[pltpu.VMEM((H, FOX_AUG, L), BF16), pltpu.VMEM((L, FOX_AUG * H // 2), BF16),
                        pltpu.VMEM((H, FOX_TILE), F32), pltpu.VMEM((H, FOX_TILE), F32),
                        pltpu.VMEM((HALF, FOX_TILE), F32), pltpu.VMEM((2, H, FOX_TILE, FOX_TILE), F32)],
        compiler_params=_cparams("parallel"),
        name="fox_attention",
    )(qt, k, vt, f_col, f_row)


def _s5_matrices(a_re, a_im, b_re, b_im, c_re, c_im, d, log_dt):
    G, P, Cg, LC = S5_GROUPS, S5_STATE, S5_GROUP_CH, S5_CHUNK
    a_re, a_im, b_re, b_im, c_re, c_im = (t.astype(F32) for t in (a_re, a_im, b_re, b_im, c_re, c_im))
    dt = jnp.exp(log_dt.astype(F32))[:, None]
    mag = jnp.exp(a_re * dt)
    lam_re, lam_im = mag * jnp.cos(a_im * dt), mag * jnp.sin(a_im * dt)
    den = a_re ** 2 + a_im ** 2
    zr = ((lam_re - 1.0) * a_re + lam_im * a_im) / den
    zi = (lam_im * a_re - (lam_re - 1.0) * a_im) / den
    bb_re = zr[..., None] * b_re - zi[..., None] * b_im
    bb_im = zr[..., None] * b_im + zi[..., None] * b_re
    n = jnp.arange(LC + 1, dtype=F32)[:, None, None]
    pw_mag = jnp.exp(n * (a_re * dt)[None])
    pr, pi = pw_mag * jnp.cos(n * (a_im * dt)[None]), pw_mag * jnp.sin(n * (a_im * dt)[None])
    ein = functools.partial(jnp.einsum, precision=HIGHEST)
    cr = c_re[None] * pr[:, :, None, :] - c_im[None] * pi[:, :, None, :]
    ci = -(c_re[None] * pi[:, :, None, :] + c_im[None] * pr[:, :, None, :])
    kn = ein('ngcp,gpd->ngcd', cr[:LC], bb_re) + ein('ngcp,gpd->ngcd', ci[:LC], bb_im)
    s_idx = jnp.arange(LC)
    lag = s_idx[None, :] - s_idx[:, None]
    kt = jnp.where((lag >= 0)[:, :, None, None, None],
                   kn[jnp.clip(lag, 0, LC - 1)], 0.0)
    dmat = (lag == 0)[:, :, None, None, None] * (d.astype(F32).reshape(1, 1, G, Cg, 1)
                                                 * jnp.eye(Cg, dtype=F32)[None, None, None])
    NS, GS = S5_SLABS, S5_SLAB_GROUPS
    k6 = (kt + dmat).reshape(LC, LC, NS, GS, Cg, Cg)
    kmat = k6.transpose(2, 0, 3, 5, 1, 4).reshape(NS, LC * 128, LC * Cg)
    rr, ri = pr[LC - 1 - s_idx], pi[LC - 1 - s_idx]
    e_re = rr[..., None] * bb_re[None] - ri[..., None] * bb_im[None]
    e_im = rr[..., None] * bb_im[None] + ri[..., None] * bb_re[None]
    e6 = jnp.stack([e_re, e_im]).reshape(2, LC, NS, GS, P, Cg)
    emat = e6.transpose(2, 1, 3, 5, 0, 4).reshape(NS, LC * 128, 2 * P)
    f6 = jnp.stack([cr[1:], ci[1:]]).reshape(2, LC, NS, GS, Cg, P)
    fmat = f6.transpose(2, 0, 3, 5, 1, 4).reshape(NS, 2 * GS * P, LC * Cg)
    lr, li = pr[LC].reshape(NS, 1, GS * P), pi[LC].reshape(NS, 1, GS * P)
    lam_a = jnp.concatenate([lr, lr], axis=-1)
    lam_b = jnp.concatenate([-li, li], axis=-1)
    return kmat.astype(BF16), emat.astype(BF16), fmat.astype(BF16), lam_a, lam_b


def _s5_expand(src_ref, dst_ref, col_of, row_group, col_group):
    n_src, n_dst = src_ref.shape[1], dst_ref.shape[1]
    sr, sc = _iota2((n_src, n_dst))
    select = jnp.where(sr == col_of(sc), 1.0, 0.0).astype(BF16)
    step = 256
    for i in range(src_ref.shape[0] // step):
        rs = slice(i * step, (i + 1) * step)
        r, c = _iota2((step, n_dst))
        same = row_group(r + i * step) == col_group(c)
        full = jnp.dot(src_ref[rs, :], select, preferred_element_type=F32)
        dst_ref[rs, :] = jnp.where(same, full, 0.0).astype(BF16)


def _s5_kernel(u_ref, kc_ref, ec_ref, fc_ref, la_ref, lb_ref, y_ref, k_ref, e_ref, f_ref, e_s, hs_s, h_s, *, B):
    @pl.when(pl.program_id(1) == 0)
    def _():
        h_s[...] = jnp.zeros_like(h_s)
        gs, cg, p = S5_SLAB_GROUPS, S5_GROUP_CH, S5_STATE
        lg = lambda n: n.bit_length() - 1
        lane_group = lambda i: (i >> lg(cg)) & (gs - 1)
        state_group = lambda i: (i >> lg(p)) & (gs - 1)
        frame_ch = lambda c: ((c >> lg(gs * cg)) << lg(cg)) | (c & (cg - 1))
        part_state = lambda c: ((c >> lg(gs * p)) << lg(p)) | (c & (p - 1))
        _s5_expand(kc_ref, k_ref, frame_ch, lane_group, lane_group)
        _s5_expand(ec_ref, e_ref, part_state, lane_group, state_group)
        _s5_expand(fc_ref, f_ref, frame_ch, state_group, lane_group)

    u = u_ref[...]
    e_s[...] = jnp.dot(u, e_ref[...], preferred_element_type=F32)
    la, lb = la_ref[...], lb_ref[...]
    half = h_s.shape[1] // 2
    h = h_s[...]
    for kc in range(u.shape[0] // B):
        rs = slice(kc * B, (kc + 1) * B)
        hs_s[rs, :] = h
        swapped = jnp.concatenate([h[:, half:], h[:, :half]], axis=1)
        h = la * h + lb * swapped + e_s[rs, :]
    h_s[...] = h
    y_ref[...] = (jnp.dot(u, k_ref[...], preferred_element_type=F32)
                  + jnp.dot(hs_s[...].astype(BF16), f_ref[...], preferred_element_type=F32)).astype(BF16)


def _s5(u4, mats, B, L):
    kmat, emat, fmat, lam_a, lam_b = mats
    NS, LC = S5_SLABS, S5_CHUNK
    NK = L // LC
    assert L % LC == 0 and (NK <= S5_TILE_CHUNKS or NK % S5_TILE_CHUNKS == 0) and B % 8 == 0, (B, L)
    R, W, S2 = NK * B, LC * 128, 2 * S5_SLAB_GROUPS * S5_STATE
    ug = u4.reshape(NS, B, NK, W).transpose(0, 2, 1, 3).reshape(NS, R, W)
    TR = B * min(NK, S5_TILE_CHUNKS)
    rows = pl.BlockSpec((None, TR, W), lambda s, r: (s, r, 0))
    per_s = lambda a: pl.BlockSpec((None,) + a.shape[1:], lambda s, r: (s, 0, 0), pipeline_mode=pl.Buffered(1))
    y = pl.pallas_call(
        functools.partial(_s5_kernel, B=B),
        grid=(NS, R // TR),
        in_specs=[rows] + [per_s(a) for a in (kmat, emat, fmat, lam_a, lam_b)],
        out_specs=rows,
        out_shape=jax.ShapeDtypeStruct((NS, R, W), BF16),
        scratch_shapes=[pltpu.VMEM((W, W), BF16), pltpu.VMEM((W, S2), BF16), pltpu.VMEM((S2, W), BF16),
                        pltpu.VMEM((TR, S2), F32), pltpu.VMEM((TR, S2), F32), pltpu.VMEM((B, S2), F32)],
        compiler_params=_cparams("parallel", "arbitrary"),
        name="s5",
    )(ug, kmat, emat, fmat, lam_a, lam_b)
    return y.reshape(NS, NK, B, W).transpose(0, 2, 1, 3).reshape(NS, B * L, 128)


def _post_kernel(a_ref, ah_ref, b_ref, bh_ref, x_ref, xh_ref, wo_ref, g1_ref, beta1_ref, *rest,
                 even, tiles_per_seq, alpha):
    if even:
        wg_ref, bg_ref, *rest = rest
    wu_ref, cw_ref, cb_ref, wd_ref, g2_ref, beta2_ref, o_ref, act_s = rest
    tm = x_ref.shape[0]
    first = (pl.program_id(0) % tiles_per_seq) == 0
    a = jnp.concatenate([ah_ref[...], a_ref[...]], axis=0)
    x_in = jnp.concatenate([xh_ref[...], x_ref[...]], axis=0)
    if even:
        y = jnp.concatenate([jnp.concatenate([bh_ref[s], b_ref[s]], axis=0) for s in range(S5_SLABS)], axis=1)
        z = jax.nn.gelu(y.astype(F32))
        second = z * jax.nn.sigmoid(_mm(z, wg_ref[...]) + bg_ref[...])
    else:
        second = jnp.concatenate([bh_ref[...], b_ref[...]], axis=0)
    mix = jnp.dot(jnp.concatenate([a, second.astype(BF16)], axis=1), wo_ref[...], preferred_element_type=F32)
    x1e = _layer_norm(alpha * x_in + mix, g1_ref[...], beta1_ref[...])
    row = lax.broadcasted_iota(jnp.int32, (HALO + tm, 1), 0)
    x1e = jnp.where((row < HALO) & first, 0.0, x1e)
    x1 = x1e[HALO:]
    xe = x1e.astype(BF16)
    xb = xe[HALO:]

    def up(c):
        ue = jnp.dot(xe, wu_ref[:, c * FFN_TILE:(c + 1) * FFN_TILE], preferred_element_type=F32)
        gate = jnp.dot(xb, wu_ref[:, D_FF + c * FFN_TILE:D_FF + (c + 1) * FFN_TILE], preferred_element_type=F32)
        return ue, gate

    n_tiles = D_FF // FFN_TILE
    nxt = up(0)
    for c in range(n_tiles):
        cs = slice(c * FFN_TILE, (c + 1) * FFN_TILE)
        ue, gate = nxt
        if c + 1 < n_tiles:
            nxt = up(c + 1)
        cw = cw_ref[:, cs]
        u = (cw[2:3] * ue[HALO:] + cw[1:2] * _shift_rows(ue, 1)[HALO:] + cw[0:1] * _shift_rows(ue, 2)[HALO:]
             + cb_ref[:, cs])
        act_s[:, cs] = (jax.nn.gelu(u) * gate).astype(BF16)
    ffn = jnp.dot(act_s[...], wd_ref[...], preferred_element_type=F32)
    o_ref[...] = _layer_norm(alpha * x1 + ffn, g2_ref[...], beta2_ref[...])


def _post(a, b, x, L, w_out, ln1_g, ln1_b, w_up, conv_w, conv_b, w_down, ln2_g, ln2_b, alpha, glu=None, tm=1024):
    T = x.shape[0]
    tm = min(tm, L)
    assert L % tm == 0 and tm % HALO == 0, (L, tm)
    wo = w_out.astype(BF16)
    row1 = lambda v, n: v.reshape(1, n).astype(F32)
    slab = lambda rows, scale: pl.BlockSpec((S5_SLABS, rows, 128), scale)
    b_specs = [_rows(tm, HALF), _halo_rows(tm, HALF)]
    if glu is not None:
        b_specs = [slab(tm, lambda i: (0, i, 0)),
                   slab(HALO, lambda i: (0, jnp.maximum(i * (tm // HALO) - 1, 0), 0))]
    args = [a, a, b, b, x, x, wo, row1(ln1_g, D_MODEL), row1(ln1_b, D_MODEL)]
    specs = ([_rows(tm, HALF), _halo_rows(tm, HALF)] + b_specs + [_rows(tm, D_MODEL), _halo_rows(tm, D_MODEL)]
             + [_resident(t.shape) for t in args[6:]])
    tail = [w_up.astype(BF16), conv_w.astype(F32), row1(conv_b, D_FF), w_down.astype(BF16),
            row1(ln2_g, D_MODEL), row1(ln2_b, D_MODEL)]
    if glu is not None:
        tail = [glu[0].astype(BF16), row1(glu[1], HALF)] + tail
    return pl.pallas_call(
        functools.partial(_post_kernel, even=glu is not None, tiles_per_seq=L // tm, alpha=alpha),
        grid=(T // tm,),
        in_specs=specs + [_resident(t.shape) for t in tail],
        out_specs=_rows(tm, D_MODEL),
        out_shape=jax.ShapeDtypeStruct((T, D_MODEL), F32),
        scratch_shapes=[pltpu.VMEM((tm, D_FF), BF16)],
        compiler_params=_cparams("parallel"),
        name="post",
    )(*args, *tail)


_LORA_PAD = (128, 128, 256)
_RWKV_PAD = 3 * HALF + sum(_LORA_PAD)


def _odd_pre_kernel(x_ref, xh_ref, wrk_ref, mu_ref, w2_ref, a2_ref, g2_ref, vec_ref, bd_ref,
                    wqk_ref, cw_ref, cb_ref, wv_ref, wz_ref, wif_ref, gb_ref,
                    r_ref, k_ref, v_ref, lw_ref, kk_ref, ka_ref, g_ref, bo_ref,
                    mq_ref, mk_ref, mv_ref, mz_ref, gate_ref, *, tiles_per_seq):
    tm = x_ref.shape[0]
    first = (pl.program_id(0) % tiles_per_seq) == 0
    xh = jnp.where(first, 0.0, xh_ref[...])
    xe = jnp.concatenate([xh, x_ref[...]], axis=0).astype(BF16)
    xb = xe[HALO:]

    pe = [jnp.dot(xe, wrk_ref[:, HALF * i:HALF * (i + 1)], preferred_element_type=F32)
          for i in range(_RWKV_PAD // HALF)]
    qke = [jnp.dot(xe, wqk_ref[:, HALF * i:HALF * (i + 1)], preferred_element_type=F32) for i in range(2)]
    mv_ref[...] = jnp.dot(xb, wv_ref[...], preferred_element_type=F32).astype(BF16)
    mz_ref[...] = jnp.dot(xb, wz_ref[...], preferred_element_type=F32).astype(mz_ref.dtype)
    pre = jnp.dot(xb, wif_ref[...], preferred_element_type=F32)[:, :2 * MLSTM_HEADS] + gb_ref[...]

    def token_shift(i):
        cur, prev = pe[i][HALO:], _shift_rows(pe[i], 1)[HALO:]
        return cur + (prev - cur) * mu_ref[:, HALF * i:HALF * (i + 1)]

    r, k, v, lora = (token_shift(i) for i in range(4))
    wd = lora[:, :_LORA_PAD[0]]
    ad = lora[:, _LORA_PAD[0]:_LORA_PAD[0] + _LORA_PAD[1]]
    gd = lora[:, _LORA_PAD[0] + _LORA_PAD[1]:]
    w0, a0, k_k, k_a, r_k = (vec_ref[i:i + 1, :] for i in range(5))
    wlog = -jax.nn.softplus(-(w0 + _mm(jnp.tanh(wd), w2_ref[...]))) - 0.5
    lw_ref[...] = -jnp.exp(wlog)
    a = jax.nn.sigmoid(a0 + _mm(ad, a2_ref[...]))
    g_ref[...] = _mm(jax.nn.sigmoid(gd), g2_ref[...]).astype(g_ref.dtype)
    kk = k * k_k
    ss = _mm_exact_rhs(kk * kk, bd_ref[...], terms=2)
    kk = kk * lax.rsqrt(jnp.maximum(ss, 1e-24))
    kmod = k * (1.0 + (a - 1.0) * k_a)
    bo_ref[...] = (_mm_exact_rhs(r * kmod * r_k, bd_ref[...], terms=2) * v).astype(bo_ref.dtype)
    r_ref[...] = r.astype(r_ref.dtype)
    k_ref[...] = kmod.astype(k_ref.dtype)
    v_ref[...] = v.astype(v_ref.dtype)
    kk_ref[...] = kk.astype(kk_ref.dtype)
    ka_ref[...] = (kk * a).astype(ka_ref.dtype)

    def conv_silu(i):
        cols = slice(HALF * i, HALF * (i + 1))
        acc = cb_ref[:, cols]
        for j in range(MLSTM_CONV):
            acc = acc + cw_ref[j:j + 1, cols] * _shift_rows(qke[i], MLSTM_CONV - 1 - j)[HALO:]
        return jax.nn.silu(acc)

    mq_ref[...] = conv_silu(0).astype(mq_ref.dtype)
    mk_ref[...] = (conv_silu(1) * (MLSTM_DH ** -0.5)).astype(BF16)
    is_i = lax.broadcasted_iota(jnp.int32, pre.shape, 1) < MLSTM_HEADS
    gate_ref[...] = jnp.where(is_i, pre, jax.nn.log_sigmoid(pre))


def _head_block_ones(width, head):
    idx = jnp.arange(width) // head
    return (idx[:, None] == idx[None, :]).astype(BF16)


def _odd_pre(x, L, w_in, mu, w0, w2, a0, a2, g2, k_k, k_a, r_k, conv_w, conv_b, ib, fb, tm=1024):
    T = x.shape[0]
    tm = min(tm, L)
    assert L % tm == 0 and tm % HALO == 0, (L, tm)
    wb = w_in.astype(BF16)
    o = 3 * HALF
    sizes = (RWKV_DECAY_LORA, RWKV_AAA_LORA, RWKV_GATE_LORA)

    def pad_lora(m, axis):
        parts, s = [], o
        for sz, pd in zip(sizes, _LORA_PAD):
            piece = lax.slice_in_dim(m, s, s + sz, axis=axis)
            widths = [(0, 0)] * m.ndim
            widths[axis] = (0, pd - sz)
            parts.append(jnp.pad(piece, widths))
            s += sz
        return jnp.concatenate([lax.slice_in_dim(m, 0, o, axis=axis)] + parts, axis=axis)

    rwkv_proj = o + sum(sizes)
    wrk = pad_lora(wb[:, :rwkv_proj], 1)
    mu_p = pad_lora(mu.astype(F32).reshape(1, -1), 1)
    padr = lambda m, rows: jnp.pad(m.astype(BF16), ((0, rows - m.shape[0]), (0, 0)))
    w2p, a2p, g2p = padr(w2, _LORA_PAD[0]), padr(a2, _LORA_PAD[1]), padr(g2, _LORA_PAD[2])
    vecs = jnp.stack([w0, a0, k_k, k_a, r_k.reshape(HALF)]).astype(F32)
    vecs = jnp.pad(vecs, ((0, 8 - vecs.shape[0]), (0, 0)))
    bd = _head_block_ones(HALF, RWKV_N)
    wm = wb[:, rwkv_proj:]
    wqk, wmv, wmz = wm[:, :2 * HALF], wm[:, 2 * HALF:3 * HALF], wm[:, 3 * HALF:4 * HALF]
    wif = jnp.pad(wm[:, 4 * HALF:], ((0, 0), (0, 128 - 2 * MLSTM_HEADS)))
    gbias = jnp.concatenate([ib, fb]).astype(F32).reshape(1, 2 * MLSTM_HEADS)
    args = [x, x, wrk, mu_p, w2p, a2p, g2p, vecs, bd, wqk, conv_w.astype(F32),
            conv_b.astype(F32).reshape(1, 2 * HALF), wmv, wmz, wif, gbias]
    specs = [_rows(tm, D_MODEL), _halo_rows(tm, D_MODEL)] + [_resident(a.shape) for a in args[2:]]
    f_half = jax.ShapeDtypeStruct((T, HALF), F32)
    b_half = jax.ShapeDtypeStruct((T, HALF), BF16)
    out_shape = [b_half] * 3 + [f_half] + [b_half] * 8 + [jax.ShapeDtypeStruct((T, 2 * MLSTM_HEADS), F32)]
    out_specs = [_rows(tm, HALF)] * 12 + [_rows(tm, 2 * MLSTM_HEADS)]
    return pl.pallas_call(
        functools.partial(_odd_pre_kernel, tiles_per_seq=L // tm),
        grid=(T // tm,),
        in_specs=specs,
        out_specs=out_specs,
        out_shape=out_shape,
        compiler_params=_cparams("parallel"),
        name="odd_pre",
    )(*args)


def _rwkv_kernel(r_ref, k_ref, v_ref, lw_ref, kk_ref, ka_ref, g_ref, bo_ref, lng_ref, lnb_ref, o_ref, m_s):
    C, N, H = CHUNK, RWKV_N, RWKV_HEADS
    rows = range(r_ref.shape[0])
    units = [(b, h) for b in rows for h in range(H)]

    @pl.when(pl.program_id(1) == 0)
    def _():
        m_s[...] = jnp.zeros_like(m_s)

    ri, ci = _iota2((C, C))
    incl, strict, eye = ci <= ri, ci < ri, ci == ri
    ri2, ci2 = _iota2((C, 2 * C))
    incl2 = (ci2 & (C - 1)) <= ri2
    zeros = jnp.zeros((C, N), F32)
    mm = functools.partial(_mmp, passes=RWKV_PASSES)
    al, rt, bt, kt, bp, kp, gam, v = ({} for _ in range(8))
    for b in rows:
        lw = lw_ref[b]
        cs = _mm_exact_lhs(incl.astype(BF16), lw)
        cend = cs[C - 1:C, :]
        e_neg = jnp.exp(-cs)
        e_rem = jnp.exp(cend - cs)
        r, k, vv, kk, ka = (t[b].astype(F32) for t in (r_ref, k_ref, v_ref, kk_ref, ka_ref))
        al_b = -kk * jnp.exp(cs - lw)
        rt_b = r * jnp.exp(cs)
        gam_b = jnp.exp(cend)
        bt_b, kt_b, bp_b, kp_b = ka * e_neg, k * e_neg, ka * e_rem, k * e_rem
        for h in range(H):
            s = slice(N * h, N * (h + 1))
            u = (b, h)
            al[u], rt[u], v[u], gam[u] = al_b[:, s], rt_b[:, s], vv[:, s], gam_b[:, s]
            bt[u], kt[u], bp[u], kp[u] = bt_b[:, s], kt_b[:, s], bp_b[:, s], kp_b[:, s]
    pm = {u: mm(jnp.concatenate([al[u], rt[u]], 0), jnp.concatenate([bt[u], kt[u]], 0), 1, 1) for u in units}
    a_ab = {u: jnp.where(strict, pm[u][:C, :C], 0.0) for u in units}
    a_ak = {u: jnp.where(strict, pm[u][:C, C:], 0.0) for u in units}
    a_r = {u: jnp.where(incl2, pm[u][C:, :], 0.0) for u in units}
    w = {u: jnp.concatenate([al[u], mm(a_ak[u], v[u])], 1) for u in units}
    npow = a_ab
    levels = int(math.log2(C))
    for lvl in range(levels):
        if lvl < levels - 1:
            y = {u: mm(npow[u], jnp.concatenate([w[u], npow[u]], 1)) for u in units}
            w = {u: w[u] + y[u][:, :2 * N] for u in units}
            npow = {u: y[u][:, 2 * N:] for u in units}
        else:
            w = {u: w[u] + mm(npow[u], w[u]) for u in units}
    zv = {u: jnp.concatenate([zeros, v[u]], 1) for u in units}
    wz = {u: jnp.concatenate([w[u], zv[u]], 0) for u in units}
    qt = {u: mm(a_r[u], wz[u]) for u in units}
    qb = {u: mm(jnp.concatenate([bp[u], kp[u]], 0), wz[u], 0, 0) for u in units}
    m0 = {u: m_s[u[0], u[1]] for u in units}
    o_h = {u: mm(rt[u] + qt[u][:, :N], m0[u]) + qt[u][:, N:] for u in units}
    m_new = {u: _mmp(qb[u][:, :N] + jnp.where(eye, gam[u], 0.0), m0[u], passes=RWKV_STATE_PASSES) + qb[u][:, N:]
             for u in units}
    for u in units:
        m_s[u[0], u[1]] = m_new[u]
    om = {u: jnp.mean(o_h[u], -1, keepdims=True) for u in units}
    d = {u: o_h[u] - om[u] for u in units}
    ov = {u: jnp.mean(d[u] * d[u], -1, keepdims=True) for u in units}
    for b in rows:
        outs = [d[(b, h)] * lax.rsqrt(ov[(b, h)] + RWKV_GN_EPS) for h in range(H)]
        on = jnp.concatenate(outs, 1) * lng_ref[...] + lnb_ref[...]
        o_ref[b] = ((on + bo_ref[b].astype(F32)) * g_ref[b].astype(F32)).astype(BF16)


def _rwkv(r, k, v, lw, kk, ka, g, bonus, ln_g, ln_b, B, L):
    NC = L // CHUNK
    assert L % CHUNK == 0, L
    RB = SEQ_ROWS if B % SEQ_ROWS == 0 else 1
    seq3 = lambda t: t.reshape(B, L, HALF)
    blk = pl.BlockSpec((RB, CHUNK, HALF), lambda b, c: (b, c, 0))
    vec = pl.BlockSpec((1, HALF), lambda b, c: (0, 0))
    out = pl.pallas_call(
        _rwkv_kernel,
        grid=(B // RB, NC),
        in_specs=[blk] * 8 + [vec, vec],
        out_specs=blk,
        out_shape=jax.ShapeDtypeStruct((B, L, HALF), BF16),
        scratch_shapes=[pltpu.VMEM((RB, RWKV_HEADS, RWKV_N, RWKV_N), F32)],
        compiler_params=_cparams("parallel", "arbitrary"),
        name="rwkv7",
    )(*(seq3(t) for t in (r, k, v, lw, kk, ka, g, bonus)),
      ln_g.reshape(1, HALF).astype(F32), ln_b.reshape(1, HALF).astype(F32))
    return out.reshape(B * L, HALF)


def _mlstm_kernel(q_ref, k_ref, v_ref, z_ref, gc_ref, gr_ref, lng_ref, skip_ref, o_ref, c_s, m_s):
    C, H, DH = CHUNK, MLSTM_HEADS, MLSTM_DH
    rows = range(q_ref.shape[0])
    units = [(b, h) for b in rows for h in range(H)]

    @pl.when(pl.program_id(1) == 0)
    def _():
        c_s[...] = jnp.zeros_like(c_s)
        m_s[...] = jnp.zeros_like(m_s)

    ri, ci = _iota2((C, C))
    incl = ci <= ri
    lower, upper = incl.astype(BF16), (ri <= ci).astype(BF16)
    lane = lax.broadcasted_iota(jnp.int32, (1, DH), 1)
    ones_blk = jnp.broadcast_to(jnp.where(lane == 0, 1.0, 0.0).astype(BF16), (C, DH))
    q_all, qb, kh, vaug, i_col, i_row, b_col, b_row, m_prev = ({} for _ in range(9))
    for b in rows:
        gc, gr = gc_ref[b], gr_ref[b]
        b_cols = _mm_exact_lhs(lower, gc)
        b_rows = _mm_exact_rhs(gr, upper)
        q_all[b] = q_ref[b].astype(F32)
        for h in range(H):
            u, s = (b, h), slice(DH * h, DH * (h + 1))
            qb[u], kh[u] = q_ref[b, :, s], k_ref[b, :, s]
            vaug[u] = jnp.concatenate([v_ref[b, :, s], ones_blk], 1)
            i_col[u], i_row[u] = gc[:, h:h + 1], gr[h:h + 1, :]
            b_col[u], b_row[u] = b_cols[:, H + h:H + h + 1], b_rows[H + h:H + h + 1, :]
            m_prev[u] = m_s[b, h:h + 1, 0:1]
    c_old = {u: c_s[u[0], u[1]] for u in units}
    qk = {u: _dg(qb[u], kh[u], 1, 1) for u in units}
    qc = {u: jnp.dot(qb[u], c_old[u].astype(BF16), preferred_element_type=F32) for u in units}
    dmat = {u: jnp.where(incl, b_col[u] - b_row[u] + i_row[u], NEG_BIG) for u in units}
    inter = {u: b_col[u] + m_prev[u] for u in units}
    m_t = {u: jnp.maximum(inter[u], jnp.max(dmat[u], -1, keepdims=True)) for u in units}
    s = {u: (qk[u] * jnp.exp(dmat[u] - m_t[u])).astype(BF16) for u in units}
    sv = {u: jnp.dot(s[u], vaug[u], preferred_element_type=F32) for u in units}
    b_end = {u: b_col[u][C - 1:C, :] for u in units}
    m_new = {u: jnp.maximum(b_end[u] + m_prev[u], jnp.max(b_end[u] - b_row[u] + i_row[u], -1, keepdims=True))
             for u in units}
    kw = {u: (jnp.exp(b_end[u] - b_col[u] + i_col[u] - m_new[u]) * kh[u].astype(F32)).astype(BF16) for u in units}
    kv = {u: _dg(kw[u], vaug[u], 0, 0) for u in units}
    for u in units:
        b, h = u
        c_s[b, h] = jnp.exp(b_end[u] + m_prev[u] - m_new[u]) * c_old[u] + kv[u]
        m_s[b, h:h + 1, :] = jnp.broadcast_to(m_new[u], (1, 128))
    nd = {u: jnp.exp(inter[u] - m_t[u]) * qc[u] + sv[u] for u in units}
    den = {u: jnp.maximum(jnp.abs(nd[u][:, DH:DH + 1]), jnp.exp(-m_t[u])) for u in units}
    hid = {u: nd[u][:, :DH] / den[u] for u in units}
    hm = {u: jnp.mean(hid[u], -1, keepdims=True) for u in units}
    d = {u: hid[u] - hm[u] for u in units}
    hv = {u: jnp.mean(d[u] * d[u], -1, keepdims=True) for u in units}
    for b in rows:
        outs = [d[(b, h)] * lax.rsqrt(hv[(b, h)] + LN_EPS) for h in range(H)]
        hn = jnp.concatenate(outs, 1) * lng_ref[...]
        o_ref[b] = ((hn + skip_ref[...] * q_all[b]) * jax.nn.silu(z_ref[b].astype(F32))).astype(BF16)


def _mlstm(q, k, v, z, gates, ln_g, skip, B, L):
    NC, H2 = L // CHUNK, 2 * MLSTM_HEADS
    assert L % CHUNK == 0, L
    RB = MLSTM_SEQ_ROWS if B % MLSTM_SEQ_ROWS == 0 else 1
    seq3 = lambda t: t.reshape(B, L, t.shape[-1])
    blk = pl.BlockSpec((RB, CHUNK, HALF), lambda b, c: (b, c, 0))
    vec = pl.BlockSpec((1, HALF), lambda b, c: (0, 0))
    g_rows = gates.reshape(B, NC, CHUNK, H2).transpose(0, 1, 3, 2)
    out = pl.pallas_call(
        _mlstm_kernel,
        grid=(B // RB, NC),
        in_specs=[blk, blk, blk, blk, pl.BlockSpec((RB, CHUNK, H2), lambda b, c: (b, c, 0)),
                  pl.BlockSpec((RB, None, H2, CHUNK), lambda b, c: (b, c, 0, 0)), vec, vec],
        out_specs=blk,
        out_shape=jax.ShapeDtypeStruct((B, L, HALF), BF16),
        scratch_shapes=[pltpu.VMEM((RB, MLSTM_HEADS, MLSTM_DH, 2 * MLSTM_DH), F32),
                        pltpu.VMEM((RB, 8, 128), F32)],
        compiler_params=_cparams("parallel", "arbitrary"),
        name="mlstm",
    )(seq3(q), seq3(k), seq3(v), seq3(z), seq3(gates), g_rows,
      ln_g.reshape(1, HALF).astype(F32), skip.reshape(1, HALF).astype(F32))
    return out.reshape(B * L, HALF)


def kernel(x, ev_w_in, ev_fox_fb, ev_s5_a_re, ev_s5_a_im, ev_s5_b_re, ev_s5_b_im, ev_s5_c_re, ev_s5_c_im, ev_s5_d, ev_s5_log_dt, ev_s5_w_glu, ev_s5_b_glu, ev_w_out, od_w_in, od_rwkv_mu, od_rwkv_w0, od_rwkv_w2, od_rwkv_a0, od_rwkv_a2, od_rwkv_g2, od_rwkv_k_k, od_rwkv_k_a, od_rwkv_r_k, od_rwkv_ln_g, od_rwkv_ln_b, od_mlstm_conv_w, od_mlstm_conv_b, od_mlstm_ib, od_mlstm_fb, od_mlstm_ln_g, od_mlstm_skip, od_w_out, ln1_g, ln1_b, ffn_w_up, ffn_conv_w, ffn_conv_b, ffn_w_down, ln2_g, ln2_b):
    B, L, _ = x.shape
    depth = ln1_g.shape[0]
    alpha = float((2 * depth) ** 0.25)
    h = x.reshape(B * L, D_MODEL).astype(F32)
    for layer in range(depth):
        i = layer // 2
        if layer % 2 == 0:
            qt, k, vt, fg, u = _even_pre(h, ev_w_in[i])
            f_row = _fox_gate(fg.reshape(B, L, FOX_HEADS).transpose(0, 2, 1), ev_fox_fb[i].astype(F32))
            f_col = jnp.pad(f_row.transpose(0, 2, 1).reshape(B * L, FOX_HEADS), ((0, 0), (0, 128 - FOX_HEADS)))
            fox = _fox_attention(qt, k, vt, f_col, f_row)
            mats = _s5_matrices(ev_s5_a_re[i], ev_s5_a_im[i], ev_s5_b_re[i], ev_s5_b_im[i],
                                ev_s5_c_re[i], ev_s5_c_im[i], ev_s5_d[i], ev_s5_log_dt[i])
            y = _s5(u, mats, B, L)
            mixed, glu, w_out = (fox, y), (ev_s5_w_glu[i], ev_s5_b_glu[i]), ev_w_out[i]
        else:
            (r, k, v, lw, kk, ka, g, bonus, mq, mk, mv, mz, gates) = _odd_pre(
                h, L, od_w_in[i], od_rwkv_mu[i], od_rwkv_w0[i], od_rwkv_w2[i], od_rwkv_a0[i], od_rwkv_a2[i],
                od_rwkv_g2[i], od_rwkv_k_k[i], od_rwkv_k_a[i], od_rwkv_r_k[i], od_mlstm_conv_w[i],
                od_mlstm_conv_b[i], od_mlstm_ib[i], od_mlstm_fb[i])
            c = _rwkv(r, k, v, lw, kk, ka, g, bonus, od_rwkv_ln_g[i], od_rwkv_ln_b[i], B, L)
            dm = _mlstm(mq, mk, mv, mz, gates, od_mlstm_ln_g[i], od_mlstm_skip[i], B, L)
            mixed, glu, w_out = (c, dm), None, od_w_out[i]
        h = _post(mixed[0], mixed[1], h, L, w_out, ln1_g[layer], ln1_b[layer], ffn_w_up[layer], ffn_conv_w[layer],
                  ffn_conv_b[layer], ffn_w_down[layer], ln2_g[layer], ln2_b[layer], alpha, glu=glu)
    return h.reshape(B, L, D_MODEL).astype(x.dtype)
```

```python
import functools
import math

import jax
import jax.numpy as jnp
from jax import lax
from jax.experimental import pallas as pl
from jax.experimental.pallas import tpu as pltpu

F32 = jnp.float32
BF16 = jnp.bfloat16
HIGHEST = lax.Precision.HIGHEST

D_MODEL = 1024
HALF = D_MODEL // 2
FOX_HEADS = 8
FOX_DH = HALF // FOX_HEADS
FOX_TILE = 256
FOX_AUG = 256
S5_GROUPS = 32
S5_GROUP_CH = 16
S5_STATE = 64
S5_CHUNK = 16
S5_SLAB_GROUPS = 128 // S5_GROUP_CH
S5_SLABS = S5_GROUPS // S5_SLAB_GROUPS
S5_TILE_CHUNKS = 16
RWKV_HEADS = 8
RWKV_N = HALF // RWKV_HEADS
RWKV_DECAY_LORA = 64
RWKV_AAA_LORA = 64
RWKV_GATE_LORA = 160
RWKV_GN_EPS = 64e-5
RWKV_PASSES = 1
RWKV_STATE_PASSES = 3
MLSTM_HEADS = 4
MLSTM_DH = HALF // MLSTM_HEADS
MLSTM_CONV = 4
CHUNK = 64
SEQ_ROWS = 4
MLSTM_SEQ_ROWS = 8
D_FF = 2816
FFN_CONV = 3
FFN_TILE = 256
LN_EPS = 1e-5
HALO = 16
LOG2_E = math.log2(math.e)
NEG_BIG = -1e30
V7X_VMEM_LIMIT_BYTES = 60 * 1024 * 1024


def _cparams(*sem):
    return pltpu.CompilerParams(dimension_semantics=sem, vmem_limit_bytes=V7X_VMEM_LIMIT_BYTES)


def _resident(shape):
    nd = len(shape)
    return pl.BlockSpec(shape, lambda *_: (0,) * nd, pipeline_mode=pl.Buffered(1))


def _rows(tm, width):
    return pl.BlockSpec((tm, width), lambda i: (i, 0))


def _halo_rows(tm, width):
    return pl.BlockSpec((HALO, width), lambda i: (jnp.maximum(i * (tm // HALO) - 1, 0), 0))


def _mm(a, b):
    return jnp.dot(a.astype(BF16), b.astype(BF16), preferred_element_type=F32)


def _dg(a, b, ca, cb):
    return lax.dot_general(a, b, (((ca,), (cb,)), ((), ())), preferred_element_type=F32)


def _hi_lo(a):
    hi = a.astype(BF16)
    lo = (a - hi.astype(F32)).astype(BF16)
    return hi, lo


def _mm3(a, b, ca=1, cb=0):
    ah, al = _hi_lo(a)
    bh, bl = _hi_lo(b)
    return _dg(ah, bh, ca, cb) + _dg(ah, bl, ca, cb) + _dg(al, bh, ca, cb)


def _mmp(a, b, ca=1, cb=0, passes=1):
    if passes == 3:
        return _mm3(a, b, ca, cb)
    return _dg(a.astype(BF16), b.astype(BF16), ca, cb)


def _split3(a):
    a1 = a.astype(BF16)
    r1 = a - a1.astype(F32)
    a2 = r1.astype(BF16)
    a3 = (r1 - a2.astype(F32)).astype(BF16)
    return a1, a2, a3


def _mm_exact_rhs(a, b01, terms=3):
    parts = _split3(a) if terms == 3 else _hi_lo(a)
    out = jnp.dot(parts[0], b01, preferred_element_type=F32)
    for part in parts[1:]:
        out = out + jnp.dot(part, b01, preferred_element_type=F32)
    return out


def _mm_exact_lhs(a01, b):
    b1, b2, b3 = _split3(b)
    return (jnp.dot(a01, b1, preferred_element_type=F32) + jnp.dot(a01, b2, preferred_element_type=F32)
            + jnp.dot(a01, b3, preferred_element_type=F32))


def _layer_norm(x, g, b):
    mu = jnp.mean(x, -1, keepdims=True)
    d = x - mu
    var = jnp.mean(d * d, -1, keepdims=True)
    return d * lax.rsqrt(var + LN_EPS) * g + b


def _shift_rows(x, n):
    return x if n == 0 else pltpu.roll(x, n, 0)


def _iota2(shape):
    return lax.broadcasted_iota(jnp.int32, shape, 0), lax.broadcasted_iota(jnp.int32, shape, 1)


def _even_pre_kernel(x_ref, wqt_ref, wk_ref, wvt_ref, wf_ref, wu_ref, qt_ref, k_ref, vt_ref, fg_ref, u_ref):
    xb = x_ref[...].astype(BF16)
    qt_ref[...] = (_dg(wqt_ref[...], xb, 1, 1) * (FOX_DH ** -0.5 * LOG2_E)).astype(BF16)
    vt_ref[...] = _dg(wvt_ref[...], xb, 1, 1).astype(BF16)
    k_ref[...] = jnp.dot(xb, wk_ref[...], preferred_element_type=F32).astype(BF16)
    u = jnp.dot(xb, wu_ref[...], preferred_element_type=F32).astype(BF16)
    for s in range(S5_SLABS):
        u_ref[s] = u[:, 128 * s:128 * (s + 1)]
    fg_ref[...] = jnp.dot(xb, wf_ref[...], preferred_element_type=F32)[:, :FOX_HEADS]


def _even_pre(x, w_in, tm=1024):
    T = x.shape[0]
    wb = w_in.astype(BF16)
    wqt, wk, wvt = wb[:, :HALF].T, wb[:, HALF:2 * HALF], wb[:, 2 * HALF:3 * HALF].T
    wf = jnp.pad(wb[:, 3 * HALF:3 * HALF + FOX_HEADS], ((0, 0), (0, 128 - FOX_HEADS)))
    wu = wb[:, 3 * HALF + FOX_HEADS:]
    half_out = jax.ShapeDtypeStruct((T, HALF), BF16)
    half_t = jax.ShapeDtypeStruct((HALF, T), BF16)
    cols = pl.BlockSpec((HALF, tm), lambda i: (0, i))
    return pl.pallas_call(
        _even_pre_kernel,
        grid=(T // tm,),
        in_specs=[_rows(tm, D_MODEL), _resident(wqt.shape), _resident(wk.shape), _resident(wvt.shape),
                  _resident(wf.shape), _resident(wu.shape)],
        out_specs=[cols, _rows(tm, HALF), cols, _rows(tm, FOX_HEADS),
                   pl.BlockSpec((S5_SLABS, tm, 128), lambda i: (0, i, 0))],
        out_shape=[half_t, half_out, half_t, jax.ShapeDtypeStruct((T, FOX_HEADS), F32),
                   jax.ShapeDtypeStruct((S5_SLABS, T, 128), BF16)],
        compiler_params=_cparams("parallel"),
        name="even_pre",
    )(x, wqt, wk, wvt, wf, wu)


def _fox_gate_kernel(fg_ref, fb_ref, o_ref, *, L):
    ls = jax.nn.log_sigmoid(fg_ref[...] + fb_ref[...])
    r, c = _iota2((128, 128))
    tri = (r <= c).astype(BF16)
    carry = jnp.zeros((FOX_HEADS, 1), F32)
    for j in range(L // 128):
        cum = _mm_exact_rhs(ls[:, j * 128:(j + 1) * 128], tri) + carry
        o_ref[:, j * 128:(j + 1) * 128] = cum
        carry = cum[:, 127:128]


def _fox_gate(fg_t, fb):
    B, H, L = fg_t.shape
    return pl.pallas_call(
        functools.partial(_fox_gate_kernel, L=L),
        grid=(B,),
        in_specs=[pl.BlockSpec((None, H, L), lambda b: (b, 0, 0)), _resident((H, 1))],
        out_specs=pl.BlockSpec((None, H, L), lambda b: (b, 0, 0)),
        out_shape=jax.ShapeDtypeStruct((B, H, L), F32),
        compiler_params=_cparams("parallel"),
        name="fox_gate",
    )(fg_t, fb.reshape(H, 1))


def _fox_kernel(qt_ref, k_ref, vt_ref, fc_ref, fr_ref, o_ref, qa_s, ka_s, m_s, l_s, acc_s, st_s, *, L):
    TQ = TK = FOX_TILE
    H, DH, KA = FOX_HEADS, FOX_DH, FOX_AUG
    heads = range(H)

    t1, t2, t3 = (t.astype(F32) for t in _split3(fr_ref[...] * LOG2_E))
    r16 = lax.broadcasted_iota(jnp.int32, (16, 1), 0)
    upper = lax.broadcasted_iota(jnp.int32, (2 * DH, 1), 0) < DH
    for h in heads:
        p, hh = divmod(h, 2)
        qpair = qt_ref[2 * DH * p:2 * DH * (p + 1), :]
        keep = upper if hh == 0 else jnp.logical_not(upper)
        qa_s[h, 0:2 * DH, :] = jnp.where(keep, qpair, jnp.zeros_like(qpair))
        ones_rows = (r16 >= 3 + 3 * hh) & (r16 < 6 + 3 * hh)
        blk = jnp.where(r16 == 0, t1[h:h + 1], jnp.where(r16 == 1, t2[h:h + 1], jnp.where(
            r16 == 2, t3[h:h + 1], jnp.where(ones_rows, 1.0, 0.0))))
        qa_s[h, 2 * DH:2 * DH + 16, :] = blk.astype(BF16)
        qa_s[h, 2 * DH + 16:KA, :] = jnp.zeros((KA - 2 * DH - 16, L), BF16)
    c1, c2, c3 = _split3(fc_ref[...] * LOG2_E)
    rs, cs_ = _iota2((128, 128))
    lane = lax.broadcasted_iota(jnp.int32, (1, 128), 1)
    ones3 = jnp.where(lane < 3, 1.0, 0.0)
    for p in range(H // 2):
        def sel(i, p=p):
            hit = ((rs == 2 * p) & (cs_ == 3 + i)) | ((rs == 2 * p + 1) & (cs_ == 6 + i))
            return jnp.where(hit, -1.0, 0.0).astype(BF16)
        aug = (jnp.dot(c1, sel(0), preferred_element_type=F32) + jnp.dot(c2, sel(1), preferred_element_type=F32)
               + jnp.dot(c3, sel(2), preferred_element_type=F32) + ones3)
        ka_s[:, KA * p:KA * p + 2 * DH] = k_ref[:, 2 * DH * p:2 * DH * (p + 1)]
        ka_s[:, KA * p + 2 * DH:KA * (p + 1)] = aug.astype(BF16)

    ri, ci = _iota2((TK, TQ))
    visible = ri <= ci

    def q_block(qi, _):
        q0 = pl.multiple_of(qi * TQ, TQ)
        m_s[...] = jnp.full(m_s.shape, NEG_BIG, F32)
        l_s[...] = jnp.zeros(l_s.shape, F32)
        acc_s[...] = jnp.zeros(acc_s.shape, F32)

        def scores(slot, k0, masked):
            kt = [ka_s[pl.ds(k0, TK), KA * p:KA * (p + 1)] for p in range(H // 2)]
            for h in heads:
                s = jnp.dot(kt[h // 2], qa_s[h, :, pl.ds(q0, TQ)], preferred_element_type=F32)
                st_s[slot, h] = jnp.where(visible, s, NEG_BIG) if masked else s

        def consume(slot, k0):
            k0 = pl.multiple_of(k0, TK)
            st = [st_s[slot, h] for h in heads]
            m_old = [m_s[h:h + 1, :] for h in heads]
            m_new = [jnp.maximum(m_old[h], jnp.max(st[h], 0, keepdims=True)) for h in heads]
            pt = [jnp.exp2(st[h] - m_new[h]) for h in heads]
            pv = [jnp.dot(vt_ref[DH * h:DH * (h + 1), pl.ds(k0, TK)], pt[h].astype(BF16),
                          preferred_element_type=F32) for h in heads]
            for h in heads:
                a = jnp.exp2(m_old[h] - m_new[h])
                m_s[h:h + 1, :] = m_new[h]
                l_s[h:h + 1, :] = a * l_s[h:h + 1, :] + jnp.sum(pt[h], 0, keepdims=True)
                acc_s[DH * h:DH * (h + 1), :] = a * acc_s[DH * h:DH * (h + 1), :] + pv[h]

        def pair_step(jj, k_prev):
            ka = pl.multiple_of(2 * jj * TK, TK)
            kb = pl.multiple_of(ka + TK, TK)
            scores(1, ka, False)
            consume(0, k_prev)
            scores(0, kb, False)
            consume(1, ka)
            return kb

        scores(0, q0, True)
        k_prev = lax.fori_loop(0, qi // 2, pair_step, q0)

        @pl.when(qi % 2 == 1)
        def _():
            k_odd = pl.multiple_of((qi - 1) * TK, TK)
            scores(1, k_odd, False)
            consume(0, k_prev)
            consume(1, k_odd)

        @pl.when(qi % 2 == 0)
        def _():
            consume(0, k_prev)
        for p in range(H // 2):
            o_pair = jnp.concatenate([acc_s[DH * h:DH * (h + 1), :] / l_s[h:h + 1, :] for h in (2 * p, 2 * p + 1)], 0)
            o_ref[pl.ds(q0, TQ), 2 * DH * p:2 * DH * (p + 1)] = o_pair.T.astype(BF16)
        return 0

    lax.fori_loop(0, L // TQ, q_block, 0)


def _fox_attention(qt, k, vt, f_col, f_row):
    B, H, L = f_row.shape
    assert L % FOX_TILE == 0, L
    seq = pl.BlockSpec((L, HALF), lambda b: (b, 0))
    seq_t = pl.BlockSpec((HALF, L), lambda b: (0, b))
    return pl.pallas_call(
        functools.partial(_fox_kernel, L=L),
        grid=(B,),
        in_specs=[seq_t, seq, seq_t, pl.BlockSpec((L, 128), lambda b: (b, 0)),
                  pl.BlockSpec((None, H, L), lambda b: (b, 0, 0))],
        out_specs=seq,
        out_shape=jax.ShapeDtypeStruct((B * L, HALF), BF16),
        scratch_shapes=[pltpu.VMEM((H, FOX_AUG, L), BF16), pltpu.VMEM((L, FOX_AUG * H // 2), BF16),
                        pltpu.VMEM((H, FOX_TILE), F32), pltpu.VMEM((H, FOX_TILE), F32),
                        pltpu.VMEM((HALF, FOX_TILE), F32), pltpu.VMEM((2, H, FOX_TILE, FOX_TILE), F32)],
        compiler_params=_cparams("parallel"),
        name="fox_attention",
    )(qt, k, vt, f_col, f_row)


def _s5_matrices(a_re, a_im, b_re, b_im, c_re, c_im, d, log_dt):
    G, P, Cg, LC = S5_GROUPS, S5_STATE, S5_GROUP_CH, S5_CHUNK
    a_re, a_im, b_re, b_im, c_re, c_im = (t.astype(F32) for t in (a_re, a_im, b_re, b_im, c_re, c_im))
    dt = jnp.exp(log_dt.astype(F32))[:, None]
    mag = jnp.exp(a_re * dt)
    lam_re, lam_im = mag * jnp.cos(a_im * dt), mag * jnp.sin(a_im * dt)
    den = a_re ** 2 + a_im ** 2
    zr = ((lam_re - 1.0) * a_re + lam_im * a_im) / den
    zi = (lam_im * a_re - (lam_re - 1.0) * a_im) / den
    bb_re = zr[..., None] * b_re - zi[..., None] * b_im
    bb_im = zr[..., None] * b_im + zi[..., None] * b_re
    n = jnp.arange(LC + 1, dtype=F32)[:, None, None]
    pw_mag = jnp.exp(n * (a_re * dt)[None])
    pr, pi = pw_mag * jnp.cos(n * (a_im * dt)[None]), pw_mag * jnp.sin(n * (a_im * dt)[None])
    ein = functools.partial(jnp.einsum, precision=HIGHEST)
    cr = c_re[None] * pr[:, :, None, :] - c_im[None] * pi[:, :, None, :]
    ci = -(c_re[None] * pi[:, :, None, :] + c_im[None] * pr[:, :, None, :])
    kn = ein('ngcp,gpd->ngcd', cr[:LC], bb_re) + ein('ngcp,gpd->ngcd', ci[:LC], bb_im)
    s_idx = jnp.arange(LC)
    lag = s_idx[None, :] - s_idx[:, None]
    kt = jnp.where((lag >= 0)[:, :, None, None, None],
                   kn[jnp.clip(lag, 0, LC - 1)], 0.0)
    dmat = (lag == 0)[:, :, None, None, None] * (d.astype(F32).reshape(1, 1, G, Cg, 1)
                                                 * jnp.eye(Cg, dtype=F32)[None, None, None])
    NS, GS = S5_SLABS, S5_SLAB_GROUPS
    k6 = (kt + dmat).reshape(LC, LC, NS, GS, Cg, Cg)
    kmat = k6.transpose(2, 0, 3, 5, 1, 4).reshape(NS, LC * 128, LC * Cg)
    rr, ri = pr[LC - 1 - s_idx], pi[LC - 1 - s_idx]
    e_re = rr[..., None] * bb_re[None] - ri[..., None] * bb_im[None]
    e_im = rr[..., None] * bb_im[None] + ri[..., None] * bb_re[None]
    e6 = jnp.stack([e_re, e_im]).reshape(2, LC, NS, GS, P, Cg)
    emat = e6.transpose(2, 1, 3, 5, 0, 4).reshape(NS, LC * 128, 2 * P)
    f6 = jnp.stack([cr[1:], ci[1:]]).reshape(2, LC, NS, GS, Cg, P)
    fmat = f6.transpose(2, 0, 3, 5, 1, 4).reshape(NS, 2 * GS * P, LC * Cg)
    lr, li = pr[LC].reshape(NS, 1, GS * P), pi[LC].reshape(NS, 1, GS * P)
    lam_a = jnp.concatenate([lr, lr], axis=-1)
    lam_b = jnp.concatenate([-li, li], axis=-1)
    return kmat.astype(BF16), emat.astype(BF16), fmat.astype(BF16), lam_a, lam_b


def _s5_expand(src_ref, dst_ref, col_of, row_group, col_group):
    n_src, n_dst = src_ref.shape[1], dst_ref.shape[1]
    sr, sc = _iota2((n_src, n_dst))
    select = jnp.where(sr == col_of(sc), 1.0, 0.0).astype(BF16)
    step = 256
    for i in range(src_ref.shape[0] // step):
        rs = slice(i * step, (i + 1) * step)
        r, c = _iota2((step, n_dst))
        same = row_group(r + i * step) == col_group(c)
        full = jnp.dot(src_ref[rs, :], select, preferred_element_type=F32)
        dst_ref[rs, :] = jnp.where(same, full, 0.0).astype(BF16)


def _s5_kernel(u_ref, kc_ref, ec_ref, fc_ref, la_ref, lb_ref, y_ref, k_ref, e_ref, f_ref, e_s, hs_s, h_s, *, B):
    @pl.when(pl.program_id(1) == 0)
    def _():
        h_s[...] = jnp.zeros_like(h_s)
        gs, cg, p = S5_SLAB_GROUPS, S5_GROUP_CH, S5_STATE
        lg = lambda n: n.bit_length() - 1
        lane_group = lambda i: (i >> lg(cg)) & (gs - 1)
        state_group = lambda i: (i >> lg(p)) & (gs - 1)
        frame_ch = lambda c: ((c >> lg(gs * cg)) << lg(cg)) | (c & (cg - 1))
        part_state = lambda c: ((c >> lg(gs * p)) << lg(p)) | (c & (p - 1))
        _s5_expand(kc_ref, k_ref, frame_ch, lane_group, lane_group)
        _s5_expand(ec_ref, e_ref, part_state, lane_group, state_group)
        _s5_expand(fc_ref, f_ref, frame_ch, state_group, lane_group)

    u = u_ref[...]
    e_s[...] = jnp.dot(u, e_ref[...], preferred_element_type=F32)
    la, lb = la_ref[...], lb_ref[...]
    half = h_s.shape[1] // 2
    h = h_s[...]
    for kc in range(u.shape[0] // B):
        rs = slice(kc * B, (kc + 1) * B)
        hs_s[rs, :] = h
        swapped = jnp.concatenate([h[:, half:], h[:, :half]], axis=1)
        h = la * h + lb * swapped + e_s[rs, :]
    h_s[...] = h
    y_ref[...] = (jnp.dot(u, k_ref[...], preferred_element_type=F32)
                  + jnp.dot(hs_s[...].astype(BF16), f_ref[...], preferred_element_type=F32)).astype(BF16)


def _s5(u4, mats, B, L):
    kmat, emat, fmat, lam_a, lam_b = mats
    NS, LC = S5_SLABS, S5_CHUNK
    NK = L // LC
    assert L % LC == 0 and (NK <= S5_TILE_CHUNKS or NK % S5_TILE_CHUNKS == 0) and B % 8 == 0, (B, L)
    R, W, S2 = NK * B, LC * 128, 2 * S5_SLAB_GROUPS * S5_STATE
    ug = u4.reshape(NS, B, NK, W).transpose(0, 2, 1, 3).reshape(NS, R, W)
    TR = B * min(NK, S5_TILE_CHUNKS)
    rows = pl.BlockSpec((None, TR, W), lambda s, r: (s, r, 0))
    per_s = lambda a: pl.BlockSpec((None,) + a.shape[1:], lambda s, r: (s, 0, 0), pipeline_mode=pl.Buffered(1))
    y = pl.pallas_call(
        functools.partial(_s5_kernel, B=B),
        grid=(NS, R // TR),
        in_specs=[rows] + [per_s(a) for a in (kmat, emat, fmat, lam_a, lam_b)],
        out_specs=rows,
        out_shape=jax.ShapeDtypeStruct((NS, R, W), BF16),
        scratch_shapes=[pltpu.VMEM((W, W), BF16), pltpu.VMEM((W, S2), BF16), pltpu.VMEM((S2, W), BF16),
                        pltpu.VMEM((TR, S2), F32), pltpu.VMEM((TR, S2), F32), pltpu.VMEM((B, S2), F32)],
        compiler_params=_cparams("parallel", "arbitrary"),
        name="s5",
    )(ug, kmat, emat, fmat, lam_a, lam_b)
    return y.reshape(NS, NK, B, W).transpose(0, 2, 1, 3).reshape(NS, B * L, 128)


def _post_kernel(a_ref, ah_ref, b_ref, bh_ref, x_ref, xh_ref, wo_ref, g1_ref, beta1_ref, *rest,
                 even, tiles_per_seq, alpha):
    if even:
        wg_ref, bg_ref, *rest = rest
    wu_ref, cw_ref, cb_ref, wd_ref, g2_ref, beta2_ref, o_ref, act_s = rest
    tm = x_ref.shape[0]
    first = (pl.program_id(0) % tiles_per_seq) == 0
    a = jnp.concatenate([ah_ref[...], a_ref[...]], axis=0)
    x_in = jnp.concatenate([xh_ref[...], x_ref[...]], axis=0)
    if even:
        y = jnp.concatenate([jnp.concatenate([bh_ref[s], b_ref[s]], axis=0) for s in range(S5_SLABS)], axis=1)
        z = jax.nn.gelu(y.astype(F32))
        second = z * jax.nn.sigmoid(_mm(z, wg_ref[...]) + bg_ref[...])
    else:
        second = jnp.concatenate([bh_ref[...], b_ref[...]], axis=0)
    mix = jnp.dot(jnp.concatenate([a, second.astype(BF16)], axis=1), wo_ref[...], preferred_element_type=F32)
    x1e = _layer_norm(alpha * x_in + mix, g1_ref[...], beta1_ref[...])
    row = lax.broadcasted_iota(jnp.int32, (HALO + tm, 1), 0)
    x1e = jnp.where((row < HALO) & first, 0.0, x1e)
    x1 = x1e[HALO:]
    xe = x1e.astype(BF16)
    xb = xe[HALO:]

    def up(c):
        ue = jnp.dot(xe, wu_ref[:, c * FFN_TILE:(c + 1) * FFN_TILE], preferred_element_type=F32)
        gate = jnp.dot(xb, wu_ref[:, D_FF + c * FFN_TILE:D_FF + (c + 1) * FFN_TILE], preferred_element_type=F32)
        return ue, gate

    n_tiles = D_FF // FFN_TILE
    nxt = up(0)
    for c in range(n_tiles):
        cs = slice(c * FFN_TILE, (c + 1) * FFN_TILE)
        ue, gate = nxt
        if c + 1 < n_tiles:
            nxt = up(c + 1)
        cw = cw_ref[:, cs]
        u = (cw[2:3] * ue[HALO:] + cw[1:2] * _shift_rows(ue, 1)[HALO:] + cw[0:1] * _shift_rows(ue, 2)[HALO:]
             + cb_ref[:, cs])
        act_s[:, cs] = (jax.nn.gelu(u) * gate).astype(BF16)
    ffn = jnp.dot(act_s[...], wd_ref[...], preferred_element_type=F32)
    o_ref[...] = _layer_norm(alpha * x1 + ffn, g2_ref[...], beta2_ref[...])


def _post(a, b, x, L, w_out, ln1_g, ln1_b, w_up, conv_w, conv_b, w_down, ln2_g, ln2_b, alpha, glu=None, tm=1024):
    T = x.shape[0]
    tm = min(tm, L)
    assert L % tm == 0 and tm % HALO == 0, (L, tm)
    wo = w_out.astype(BF16)
    row1 = lambda v, n: v.reshape(1, n).astype(F32)
    slab = lambda rows, scale: pl.BlockSpec((S5_SLABS, rows, 128), scale)
    b_specs = [_rows(tm, HALF), _halo_rows(tm, HALF)]
    if glu is not None:
        b_specs = [slab(tm, lambda i: (0, i, 0)),
                   slab(HALO, lambda i: (0, jnp.maximum(i * (tm // HALO) - 1, 0), 0))]
    args = [a, a, b, b, x, x, wo, row1(ln1_g, D_MODEL), row1(ln1_b, D_MODEL)]
    specs = ([_rows(tm, HALF), _halo_rows(tm, HALF)] + b_specs + [_rows(tm, D_MODEL), _halo_rows(tm, D_MODEL)]
             + [_resident(t.shape) for t in args[6:]])
    tail = [w_up.astype(BF16), conv_w.astype(F32), row1(conv_b, D_FF), w_down.astype(BF16),
            row1(ln2_g, D_MODEL), row1(ln2_b, D_MODEL)]
    if glu is not None:
        tail = [glu[0].astype(BF16), row1(glu[1], HALF)] + tail
    return pl.pallas_call(
        functools.partial(_post_kernel, even=glu is not None, tiles_per_seq=L // tm, alpha=alpha),
        grid=(T // tm,),
        in_specs=specs + [_resident(t.shape) for t in tail],
        out_specs=_rows(tm, D_MODEL),
        out_shape=jax.ShapeDtypeStruct((T, D_MODEL), F32),
        scratch_shapes=[pltpu.VMEM((tm, D_FF), BF16)],
        compiler_params=_cparams("parallel"),
        name="post",
    )(*args, *tail)


_LORA_PAD = (128, 128, 256)
_RWKV_PAD = 3 * HALF + sum(_LORA_PAD)


def _odd_pre_kernel(x_ref, xh_ref, wrk_ref, mu_ref, w2_ref, a2_ref, g2_ref, vec_ref, bd_ref,
                    wqk_ref, cw_ref, cb_ref, wv_ref, wz_ref, wif_ref, gb_ref,
                    r_ref, k_ref, v_ref, lw_ref, kk_ref, ka_ref, g_ref, bo_ref,
                    mq_ref, mk_ref, mv_ref, mz_ref, gate_ref, *, tiles_per_seq):
    tm = x_ref.shape[0]
    first = (pl.program_id(0) % tiles_per_seq) == 0
    xh = jnp.where(first, 0.0, xh_ref[...])
    xe = jnp.concatenate([xh, x_ref[...]], axis=0).astype(BF16)
    xb = xe[HALO:]

    pe = [jnp.dot(xe, wrk_ref[:, HALF * i:HALF * (i + 1)], preferred_element_type=F32)
          for i in range(_RWKV_PAD // HALF)]
    qke = [jnp.dot(xe, wqk_ref[:, HALF * i:HALF * (i + 1)], preferred_element_type=F32) for i in range(2)]
    mv_ref[...] = jnp.dot(xb, wv_ref[...], preferred_element_type=F32).astype(BF16)
    mz_ref[...] = jnp.dot(xb, wz_ref[...], preferred_element_type=F32).astype(mz_ref.dtype)
    pre = jnp.dot(xb, wif_ref[...], preferred_element_type=F32)[:, :2 * MLSTM_HEADS] + gb_ref[...]

    def token_shift(i):
        cur, prev = pe[i][HALO:], _shift_rows(pe[i], 1)[HALO:]
        return cur + (prev - cur) * mu_ref[:, HALF * i:HALF * (i + 1)]

    r, k, v, lora = (token_shift(i) for i in range(4))
    wd = lora[:, :_LORA_PAD[0]]
    ad = lora[:, _LORA_PAD[0]:_LORA_PAD[0] + _LORA_PAD[1]]
    gd = lora[:, _LORA_PAD[0] + _LORA_PAD[1]:]
    w0, a0, k_k, k_a, r_k = (vec_ref[i:i + 1, :] for i in range(5))
    wlog = -jax.nn.softplus(-(w0 + _mm(jnp.tanh(wd), w2_ref[...]))) - 0.5
    lw_ref[...] = -jnp.exp(wlog)
    a = jax.nn.sigmoid(a0 + _mm(ad, a2_ref[...]))
    g_ref[...] = _mm(jax.nn.sigmoid(gd), g2_ref[...]).astype(g_ref.dtype)
    kk = k * k_k
    ss = _mm_exact_rhs(kk * kk, bd_ref[...], terms=2)
    kk = kk * lax.rsqrt(jnp.maximum(ss, 1e-24))
    kmod = k * (1.0 + (a - 1.0) * k_a)
    bo_ref[...] = (_mm_exact_rhs(r * kmod * r_k, bd_ref[...], terms=2) * v).astype(bo_ref.dtype)
    r_ref[...] = r.astype(r_ref.dtype)
    k_ref[...] = kmod.astype(k_ref.dtype)
    v_ref[...] = v.astype(v_ref.dtype)
    kk_ref[...] = kk.astype(kk_ref.dtype)
    ka_ref[...] = (kk * a).astype(ka_ref.dtype)

    def conv_silu(i):
        cols = slice(HALF * i, HALF * (i + 1))
        acc = cb_ref[:, cols]
        for j in range(MLSTM_CONV):
            acc = acc + cw_ref[j:j + 1, cols] * _shift_rows(qke[i], MLSTM_CONV - 1 - j)[HALO:]
        return jax.nn.silu(acc)

    mq_ref[...] = conv_silu(0).astype(mq_ref.dtype)
    mk_ref[...] = (conv_silu(1) * (MLSTM_DH ** -0.5)).astype(BF16)
    is_i = lax.broadcasted_iota(jnp.int32, pre.shape, 1) < MLSTM_HEADS
    gate_ref[...] = jnp.where(is_i, pre, jax.nn.log_sigmoid(pre))


def _head_block_ones(width, head):
    idx = jnp.arange(width) // head
    return (idx[:, None] == idx[None, :]).astype(BF16)


def _odd_pre(x, L, w_in, mu, w0, w2, a0, a2, g2, k_k, k_a, r_k, conv_w, conv_b, ib, fb, tm=1024):
    T = x.shape[0]
    tm = min(tm, L)
    assert L % tm == 0 and tm % HALO == 0, (L, tm)
    wb = w_in.astype(BF16)
    o = 3 * HALF
    sizes = (RWKV_DECAY_LORA, RWKV_AAA_LORA, RWKV_GATE_LORA)

    def pad_lora(m, axis):
        parts, s = [], o
        for sz, pd in zip(sizes, _LORA_PAD):
            piece = lax.slice_in_dim(m, s, s + sz, axis=axis)
            widths = [(0, 0)] * m.ndim
            widths[axis] = (0, pd - sz)
            parts.append(jnp.pad(piece, widths))
            s += sz
        return jnp.concatenate([lax.slice_in_dim(m, 0, o, axis=axis)] + parts, axis=axis)

    rwkv_proj = o + sum(sizes)
    wrk = pad_lora(wb[:, :rwkv_proj], 1)
    mu_p = pad_lora(mu.astype(F32).reshape(1, -1), 1)
    padr = lambda m, rows: jnp.pad(m.astype(BF16), ((0, rows - m.shape[0]), (0, 0)))
    w2p, a2p, g2p = padr(w2, _LORA_PAD[0]), padr(a2, _LORA_PAD[1]), padr(g2, _LORA_PAD[2])
    vecs = jnp.stack([w0, a0, k_k, k_a, r_k.reshape(HALF)]).astype(F32)
    vecs = jnp.pad(vecs, ((0, 8 - vecs.shape[0]), (0, 0)))
    bd = _head_block_ones(HALF, RWKV_N)
    wm = wb[:, rwkv_proj:]
    wqk, wmv, wmz = wm[:, :2 * HALF], wm[:, 2 * HALF:3 * HALF], wm[:, 3 * HALF:4 * HALF]
    wif = jnp.pad(wm[:, 4 * HALF:], ((0, 0), (0, 128 - 2 * MLSTM_HEADS)))
    gbias = jnp.concatenate([ib, fb]).astype(F32).reshape(1, 2 * MLSTM_HEADS)
    args = [x, x, wrk, mu_p, w2p, a2p, g2p, vecs, bd, wqk, conv_w.astype(F32),
            conv_b.astype(F32).reshape(1, 2 * HALF), wmv, wmz, wif, gbias]
    specs = [_rows(tm, D_MODEL), _halo_rows(tm, D_MODEL)] + [_resident(a.shape) for a in args[2:]]
    f_half = jax.ShapeDtypeStruct((T, HALF), F32)
    b_half = jax.ShapeDtypeStruct((T, HALF), BF16)
    out_shape = [b_half] * 3 + [f_half] + [b_half] * 8 + [jax.ShapeDtypeStruct((T, 2 * MLSTM_HEADS), F32)]
    out_specs = [_rows(tm, HALF)] * 12 + [_rows(tm, 2 * MLSTM_HEADS)]
    return pl.pallas_call(
        functools.partial(_odd_pre_kernel, tiles_per_seq=L // tm),
        grid=(T // tm,),
        in_specs=specs,
        out_specs=out_specs,
        out_shape=out_shape,
        compiler_params=_cparams("parallel"),
        name="odd_pre",
    )(*args)


def _rwkv_kernel(r_ref, k_ref, v_ref, lw_ref, kk_ref, ka_ref, g_ref, bo_ref, lng_ref, lnb_ref, o_ref, m_s):
    C, N, H = CHUNK, RWKV_N, RWKV_HEADS
    rows = range(r_ref.shape[0])
    units = [(b, h) for b in rows for h in range(H)]

    @pl.when(pl.program_id(1) == 0)
    def _():
        m_s[...] = jnp.zeros_like(m_s)

    ri, ci = _iota2((C, C))
    incl, strict, eye = ci <= ri, ci < ri, ci == ri
    ri2, ci2 = _iota2((C, 2 * C))
    incl2 = (ci2 & (C - 1)) <= ri2
    zeros = jnp.zeros((C, N), F32)
    mm = functools.partial(_mmp, passes=RWKV_PASSES)
    al, rt, bt, kt, bp, kp, gam, v = ({} for _ in range(8))
    for b in rows:
        lw = lw_ref[b]
        cs = _mm_exact_lhs(incl.astype(BF16), lw)
        cend = cs[C - 1:C, :]
        e_neg = jnp.exp(-cs)
        e_rem = jnp.exp(cend - cs)
        r, k, vv, kk, ka = (t[b].astype(F32) for t in (r_ref, k_ref, v_ref, kk_ref, ka_ref))
        al_b = -kk * jnp.exp(cs - lw)
        rt_b = r * jnp.exp(cs)
        gam_b = jnp.exp(cend)
        bt_b, kt_b, bp_b, kp_b = ka * e_neg, k * e_neg, ka * e_rem, k * e_rem
        for h in range(H):
            s = slice(N * h, N * (h + 1))
            u = (b, h)
            al[u], rt[u], v[u], gam[u] = al_b[:, s], rt_b[:, s], vv[:, s], gam_b[:, s]
            bt[u], kt[u], bp[u], kp[u] = bt_b[:, s], kt_b[:, s], bp_b[:, s], kp_b[:, s]
    pm = {u: mm(jnp.concatenate([al[u], rt[u]], 0), jnp.concatenate([bt[u], kt[u]], 0), 1, 1) for u in units}
    a_ab = {u: jnp.where(strict, pm[u][:C, :C], 0.0) for u in units}
    a_ak = {u: jnp.where(strict, pm[u][:C, C:], 0.0) for u in units}
    a_r = {u: jnp.where(incl2, pm[u][C:, :], 0.0) for u in units}
    w = {u: jnp.concatenate([al[u], mm(a_ak[u], v[u])], 1) for u in units}
    npow = a_ab
    levels = int(math.log2(C))
    for lvl in range(levels):
        if lvl < levels - 1:
            y = {u: mm(npow[u], jnp.concatenate([w[u], npow[u]], 1)) for u in units}
            w = {u: w[u] + y[u][:, :2 * N] for u in units}
            npow = {u: y[u][:, 2 * N:] for u in units}
        else:
            w = {u: w[u] + mm(npow[u], w[u]) for u in units}
    zv = {u: jnp.concatenate([zeros, v[u]], 1) for u in units}
    wz = {u: jnp.concatenate([w[u], zv[u]], 0) for u in units}
    qt = {u: mm(a_r[u], wz[u]) for u in units}
    qb = {u: mm(jnp.concatenate([bp[u], kp[u]], 0), wz[u], 0, 0) for u in units}
    m0 = {u: m_s[u[0], u[1]] for u in units}
    o_h = {u: mm(rt[u] + qt[u][:, :N], m0[u]) + qt[u][:, N:] for u in units}
    m_new = {u: _mmp(qb[u][:, :N] + jnp.where(eye, gam[u], 0.0), m0[u], passes=RWKV_STATE_PASSES) + qb[u][:, N:]
             for u in units}
    for u in units:
        m_s[u[0], u[1]] = m_new[u]
    om = {u: jnp.mean(o_h[u], -1, keepdims=True) for u in units}
    d = {u: o_h[u] - om[u] for u in units}
    ov = {u: jnp.mean(d[u] * d[u], -1, keepdims=True) for u in units}
    for b in rows:
        outs = [d[(b, h)] * lax.rsqrt(ov[(b, h)] + RWKV_GN_EPS) for h in range(H)]
        on = jnp.concatenate(outs, 1) * lng_ref[...] + lnb_ref[...]
        o_ref[b] = ((on + bo_ref[b].astype(F32)) * g_ref[b].astype(F32)).astype(BF16)


def _rwkv(r, k, v, lw, kk, ka, g, bonus, ln_g, ln_b, B, L):
    NC = L // CHUNK
    assert L % CHUNK == 0, L
    RB = SEQ_ROWS if B % SEQ_ROWS == 0 else 1
    seq3 = lambda t: t.reshape(B, L, HALF)
    blk = pl.BlockSpec((RB, CHUNK, HALF), lambda b, c: (b, c, 0))
    vec = pl.BlockSpec((1, HALF), lambda b, c: (0, 0))
    out = pl.pallas_call(
        _rwkv_kernel,
        grid=(B // RB, NC),
        in_specs=[blk] * 8 + [vec, vec],
        out_specs=blk,
        out_shape=jax.ShapeDtypeStruct((B, L, HALF), BF16),
        scratch_shapes=[pltpu.VMEM((RB, RWKV_HEADS, RWKV_N, RWKV_N), F32)],
        compiler_params=_cparams("parallel", "arbitrary"),
        name="rwkv7",
    )(*(seq3(t) for t in (r, k, v, lw, kk, ka, g, bonus)),
      ln_g.reshape(1, HALF).astype(F32), ln_b.reshape(1, HALF).astype(F32))
    return out.reshape(B * L, HALF)


def _mlstm_kernel(q_ref, k_ref, v_ref, z_ref, gc_ref, gr_ref, lng_ref, skip_ref, o_ref, c_s, m_s):
    C, H, DH = CHUNK, MLSTM_HEADS, MLSTM_DH
    rows = range(q_ref.shape[0])
    units = [(b, h) for b in rows for h in range(H)]

    @pl.when(pl.program_id(1) == 0)
    def _():
        c_s[...] = jnp.zeros_like(c_s)
        m_s[...] = jnp.zeros_like(m_s)

    ri, ci = _iota2((C, C))
    incl = ci <= ri
    lower, upper = incl.astype(BF16), (ri <= ci).astype(BF16)
    lane = lax.broadcasted_iota(jnp.int32, (1, DH), 1)
    ones_blk = jnp.broadcast_to(jnp.where(lane == 0, 1.0, 0.0).astype(BF16), (C, DH))
    q_all, qb, kh, vaug, i_col, i_row, b_col, b_row, m_prev = ({} for _ in range(9))
    for b in rows:
        gc, gr = gc_ref[b], gr_ref[b]
        b_cols = _mm_exact_lhs(lower, gc)
        b_rows = _mm_exact_rhs(gr, upper)
        q_all[b] = q_ref[b].astype(F32)
        for h in range(H):
            u, s = (b, h), slice(DH * h, DH * (h + 1))
            qb[u], kh[u] = q_ref[b, :, s], k_ref[b, :, s]
            vaug[u] = jnp.concatenate([v_ref[b, :, s], ones_blk], 1)
            i_col[u], i_row[u] = gc[:, h:h + 1], gr[h:h + 1, :]
            b_col[u], b_row[u] = b_cols[:, H + h:H + h + 1], b_rows[H + h:H + h + 1, :]
            m_prev[u] = m_s[b, h:h + 1, 0:1]
    c_old = {u: c_s[u[0], u[1]] for u in units}
    qk = {u: _dg(qb[u], kh[u], 1, 1) for u in units}
    qc = {u: jnp.dot(qb[u], c_old[u].astype(BF16), preferred_element_type=F32) for u in units}
    dmat = {u: jnp.where(incl, b_col[u] - b_row[u] + i_row[u], NEG_BIG) for u in units}
    inter = {u: b_col[u] + m_prev[u] for u in units}
    m_t = {u: jnp.maximum(inter[u], jnp.max(dmat[u], -1, keepdims=True)) for u in units}
    s = {u: (qk[u] * jnp.exp(dmat[u] - m_t[u])).astype(BF16) for u in units}
    sv = {u: jnp.dot(s[u], vaug[u], preferred_element_type=F32) for u in units}
    b_end = {u: b_col[u][C - 1:C, :] for u in units}
    m_new = {u: jnp.maximum(b_end[u] + m_prev[u], jnp.max(b_end[u] - b_row[u] + i_row[u], -1, keepdims=True))
             for u in units}
    kw = {u: (jnp.exp(b_end[u] - b_col[u] + i_col[u] - m_new[u]) * kh[u].astype(F32)).astype(BF16) for u in units}
    kv = {u: _dg(kw[u], vaug[u], 0, 0) for u in units}
    for u in units:
        b, h = u
        c_s[b, h] = jnp.exp(b_end[u] + m_prev[u] - m_new[u]) * c_old[u] + kv[u]
        m_s[b, h:h + 1, :] = jnp.broadcast_to(m_new[u], (1, 128))
    nd = {u: jnp.exp(inter[u] - m_t[u]) * qc[u] + sv[u] for u in units}
    den = {u: jnp.maximum(jnp.abs(nd[u][:, DH:DH + 1]), jnp.exp(-m_t[u])) for u in units}
    hid = {u: nd[u][:, :DH] / den[u] for u in units}
    hm = {u: jnp.mean(hid[u], -1, keepdims=True) for u in units}
    d = {u: hid[u] - hm[u] for u in units}
    hv = {u: jnp.mean(d[u] * d[u], -1, keepdims=True) for u in units}
    for b in rows:
        outs = [d[(b, h)] * lax.rsqrt(hv[(b, h)] + LN_EPS) for h in range(H)]
        hn = jnp.concatenate(outs, 1) * lng_ref[...]
        o_ref[b] = ((hn + skip_ref[...] * q_all[b]) * jax.nn.silu(z_ref[b].astype(F32))).astype(BF16)


def _mlstm(q, k, v, z, gates, ln_g, skip, B, L):
    NC, H2 = L // CHUNK, 2 * MLSTM_HEADS
    assert L % CHUNK == 0, L
    RB = MLSTM_SEQ_ROWS if B % MLSTM_SEQ_ROWS == 0 else 1
    seq3 = lambda t: t.reshape(B, L, t.shape[-1])
    blk = pl.BlockSpec((RB, CHUNK, HALF), lambda b, c: (b, c, 0))
    vec = pl.BlockSpec((1, HALF), lambda b, c: (0, 0))
    g_rows = gates.reshape(B, NC, CHUNK, H2).transpose(0, 1, 3, 2)
    out = pl.pallas_call(
        _mlstm_kernel,
        grid=(B // RB, NC),
        in_specs=[blk, blk, blk, blk, pl.BlockSpec((RB, CHUNK, H2), lambda b, c: (b, c, 0)),
                  pl.BlockSpec((RB, None, H2, CHUNK), lambda b, c: (b, c, 0, 0)), vec, vec],
        out_specs=blk,
        out_shape=jax.ShapeDtypeStruct((B, L, HALF), BF16),
        scratch_shapes=[pltpu.VMEM((RB, MLSTM_HEADS, MLSTM_DH, 2 * MLSTM_DH), F32),
                        pltpu.VMEM((RB, 8, 128), F32)],
        compiler_params=_cparams("parallel", "arbitrary"),
        name="mlstm",
    )(seq3(q), seq3(k), seq3(v), seq3(z), seq3(gates), g_rows,
      ln_g.reshape(1, HALF).astype(F32), skip.reshape(1, HALF).astype(F32))
    return out.reshape(B * L, HALF)


def kernel(x, ev_w_in, ev_fox_fb, ev_s5_a_re, ev_s5_a_im, ev_s5_b_re, ev_s5_b_im, ev_s5_c_re, ev_s5_c_im, ev_s5_d, ev_s5_log_dt, ev_s5_w_glu, ev_s5_b_glu, ev_w_out, od_w_in, od_rwkv_mu, od_rwkv_w0, od_rwkv_w2, od_rwkv_a0, od_rwkv_a2, od_rwkv_g2, od_rwkv_k_k, od_rwkv_k_a, od_rwkv_r_k, od_rwkv_ln_g, od_rwkv_ln_b, od_mlstm_conv_w, od_mlstm_conv_b, od_mlstm_ib, od_mlstm_fb, od_mlstm_ln_g, od_mlstm_skip, od_w_out, ln1_g, ln1_b, ffn_w_up, ffn_conv_w, ffn_conv_b, ffn_w_down, ln2_g, ln2_b):
    B, L, _ = x.shape
    depth = ln1_g.shape[0]
    alpha = float((2 * depth) ** 0.25)
    h = x.reshape(B * L, D_MODEL).astype(F32)
    for layer in range(depth):
        i = layer // 2
        if layer % 2 == 0:
            qt, k, vt, fg, u = _even_pre(h, ev_w_in[i])
            f_row = _fox_gate(fg.reshape(B, L, FOX_HEADS).transpose(0, 2, 1), ev_fox_fb[i].astype(F32))
            f_col = jnp.pad(f_row.transpose(0, 2, 1).reshape(B * L, FOX_HEADS), ((0, 0), (0, 128 - FOX_HEADS)))
            fox = _fox_attention(qt, k, vt, f_col, f_row)
            mats = _s5_matrices(ev_s5_a_re[i], ev_s5_a_im[i], ev_s5_b_re[i], ev_s5_b_im[i],
                                ev_s5_c_re[i], ev_s5_c_im[i], ev_s5_d[i], ev_s5_log_dt[i])
            y = _s5(u, mats, B, L)
            mixed, glu, w_out = (fox, y), (ev_s5_w_glu[i], ev_s5_b_glu[i]), ev_w_out[i]
        else:
            (r, k, v, lw, kk, ka, g, bonus, mq, mk, mv, mz, gates) = _odd_pre(
                h, L, od_w_in[i], od_rwkv_mu[i], od_rwkv_w0[i], od_rwkv_w2[i], od_rwkv_a0[i], od_rwkv_a2[i],
                od_rwkv_g2[i], od_rwkv_k_k[i], od_rwkv_k_a[i], od_rwkv_r_k[i], od_mlstm_conv_w[i],
                od_mlstm_conv_b[i], od_mlstm_ib[i], od_mlstm_fb[i])
            c = _rwkv(r, k, v, lw, kk, ka, g, bonus, od_rwkv_ln_g[i], od_rwkv_ln_b[i], B, L)
            dm = _mlstm(mq, mk, mv, mz, gates, od_mlstm_ln_g[i], od_mlstm_skip[i], B, L)
            mixed, glu, w_out = (c, dm), None, od_w_out[i]
        h = _post(mixed[0], mixed[1], h, L, w_out, ln1_g[layer], ln1_b[layer], ffn_w_up[layer], ffn_conv_w[layer],
                  ffn_conv_b[layer], ffn_w_down[layer], ln2_g[layer], ln2_b[layer], alpha, glu=glu)
    return h.reshape(B, L, D_MODEL).astype(x.dtype)
```

```python
import functools
import math

import jax
import jax.numpy as jnp
from jax import lax
from jax.experimental import pallas as pl
from jax.experimental.pallas import tpu as pltpu

F32 = jnp.float32
BF16 = jnp.bfloat16
HIGHEST = lax.Precision.HIGHEST

D_MODEL = 1024
HALF = D_MODEL // 2
FOX_HEADS = 8
FOX_DH = HALF // FOX_HEADS
FOX_TILE = 256
FOX_AUG = 256
S5_GROUPS = 32
S5_GROUP_CH = 16
S5_STATE = 64
S5_CHUNK = 16
S5_SLAB_GROUPS = 128 // S5_GROUP_CH
S5_SLABS = S5_GROUPS // S5_SLAB_GROUPS
S5_TILE_CHUNKS = 16
RWKV_HEADS = 8
RWKV_N = HALF // RWKV_HEADS
RWKV_DECAY_LORA = 64
RWKV_AAA_LORA = 64
RWKV_GATE_LORA = 160
RWKV_GN_EPS = 64e-5
RWKV_PASSES = 1
RWKV_STATE_PASSES = 3
MLSTM_HEADS = 4
MLSTM_DH = HALF // MLSTM_HEADS
MLSTM_CONV = 4
CHUNK = 64
SEQ_ROWS = 4
MLSTM_SEQ_ROWS = 8
D_FF = 2816
FFN_CONV = 3
FFN_TILE = 256
LN_EPS = 1e-5
HALO = 16
LOG2_E = math.log2(math.e)
NEG_BIG = -1e30
V7X_VMEM_LIMIT_BYTES = 60 * 1024 * 1024


def _cparams(*sem):
    return pltpu.CompilerParams(dimension_semantics=sem, vmem_limit_bytes=V7X_VMEM_LIMIT_BYTES)


def _resident(shape):
    nd = len(shape)
    return pl.BlockSpec(shape, lambda *_: (0,) * nd, pipeline_mode=pl.Buffered(1))


def _rows(tm, width):
    return pl.BlockSpec((tm, width), lambda i: (i, 0))


def _halo_rows(tm, width):
    return pl.BlockSpec((HALO, width), lambda i: (jnp.maximum(i * (tm // HALO) - 1, 0), 0))


def _mm(a, b):
    return jnp.dot(a.astype(BF16), b.astype(BF16), preferred_element_type=F32)


def _dg(a, b, ca, cb):
    return lax.dot_general(a, b, (((ca,), (cb,)), ((), ())), preferred_element_type=F32)


def _hi_lo(a):
    hi = a.astype(BF16)
    lo = (a - hi.astype(F32)).astype(BF16)
    return hi, lo


def _mm3(a, b, ca=1, cb=0):
    ah, al = _hi_lo(a)
    bh, bl = _hi_lo(b)
    return _dg(ah, bh, ca, cb) + _dg(ah, bl, ca, cb) + _dg(al, bh, ca, cb)


def _mmp(a, b, ca=1, cb=0, passes=1):
    if passes == 3:
        return _mm3(a, b, ca, cb)
    return _dg(a.astype(BF16), b.astype(BF16), ca, cb)


def _split3(a):
    a1 = a.astype(BF16)
    r1 = a - a1.astype(F32)
    a2 = r1.astype(BF16)
    a3 = (r1 - a2.astype(F32)).astype(BF16)
    return a1, a2, a3


def _mm_exact_rhs(a, b01, terms=3):
    parts = _split3(a) if terms == 3 else _hi_lo(a)
    out = jnp.dot(parts[0], b01, preferred_element_type=F32)
    for part in parts[1:]:
        out = out + jnp.dot(part, b01, preferred_element_type=F32)
    return out


def _mm_exact_lhs(a01, b):
    b1, b2, b3 = _split3(b)
    return (jnp.dot(a01, b1, preferred_element_type=F32) + jnp.dot(a01, b2, preferred_element_type=F32)
            + jnp.dot(a01, b3, preferred_element_type=F32))


def _layer_norm(x, g, b):
    mu = jnp.mean(x, -1, keepdims=True)
    d = x - mu
    var = jnp.mean(d * d, -1, keepdims=True)
    return d * lax.rsqrt(var + LN_EPS) * g + b


def _shift_rows(x, n):
    return x if n == 0 else pltpu.roll(x, n, 0)


def _iota2(shape):
    return lax.broadcasted_iota(jnp.int32, shape, 0), lax.broadcasted_iota(jnp.int32, shape, 1)


def _even_pre_kernel(x_ref, wqt_ref, wk_ref, wvt_ref, wf_ref, wu_ref, qt_ref, k_ref, vt_ref, fg_ref, u_ref):
    xb = x_ref[...].astype(BF16)
    qt_ref[...] = (_dg(wqt_ref[...], xb, 1, 1) * (FOX_DH ** -0.5 * LOG2_E)).astype(BF16)
    vt_ref[...] = _dg(wvt_ref[...], xb, 1, 1).astype(BF16)
    k_ref[...] = jnp.dot(xb, wk_ref[...], preferred_element_type=F32).astype(BF16)
    u = jnp.dot(xb, wu_ref[...], preferred_element_type=F32).astype(BF16)
    for s in range(S5_SLABS):
        u_ref[s] = u[:, 128 * s:128 * (s + 1)]
    fg_ref[...] = jnp.dot(xb, wf_ref[...], preferred_element_type=F32)[:, :FOX_HEADS]


def _even_pre(x, w_in, tm=1024):
    T = x.shape[0]
    wb = w_in.astype(BF16)
    wqt, wk, wvt = wb[:, :HALF].T, wb[:, HALF:2 * HALF], wb[:, 2 * HALF:3 * HALF].T
    wf = jnp.pad(wb[:, 3 * HALF:3 * HALF + FOX_HEADS], ((0, 0), (0, 128 - FOX_HEADS)))
    wu = wb[:, 3 * HALF + FOX_HEADS:]
    half_out = jax.ShapeDtypeStruct((T, HALF), BF16)
    half_t = jax.ShapeDtypeStruct((HALF, T), BF16)
    cols = pl.BlockSpec((HALF, tm), lambda i: (0, i))
    return pl.pallas_call(
        _even_pre_kernel,
        grid=(T // tm,),
        in_specs=[_rows(tm, D_MODEL), _resident(wqt.shape), _resident(wk.shape), _resident(wvt.shape),
                  _resident(wf.shape), _resident(wu.shape)],
        out_specs=[cols, _rows(tm, HALF), cols, _rows(tm, FOX_HEADS),
                   pl.BlockSpec((S5_SLABS, tm, 128), lambda i: (0, i, 0))],
        out_shape=[half_t, half_out, half_t, jax.ShapeDtypeStruct((T, FOX_HEADS), F32),
                   jax.ShapeDtypeStruct((S5_SLABS, T, 128), BF16)],
        compiler_params=_cparams("parallel"),
        name="even_pre",
    )(x, wqt, wk, wvt, wf, wu)


def _fox_gate_kernel(fg_ref, fb_ref, o_ref, *, L):
    ls = jax.nn.log_sigmoid(fg_ref[...] + fb_ref[...])
    r, c = _iota2((128, 128))
    tri = (r <= c).astype(BF16)
    carry = jnp.zeros((FOX_HEADS, 1), F32)
    for j in range(L // 128):
        cum = _mm_exact_rhs(ls[:, j * 128:(j + 1) * 128], tri) + carry
        o_ref[:, j * 128:(j + 1) * 128] = cum
        carry = cum[:, 127:128]


def _fox_gate(fg_t, fb):
    B, H, L = fg_t.shape
    return pl.pallas_call(
        functools.partial(_fox_gate_kernel, L=L),
        grid=(B,),
        in_specs=[pl.BlockSpec((None, H, L), lambda b: (b, 0, 0)), _resident((H, 1))],
        out_specs=pl.BlockSpec((None, H, L), lambda b: (b, 0, 0)),
        out_shape=jax.ShapeDtypeStruct((B, H, L), F32),
        compiler_params=_cparams("parallel"),
        name="fox_gate",
    )(fg_t, fb.reshape(H, 1))


def _fox_kernel(qt_ref, k_ref, vt_ref, fc_ref, fr_ref, o_ref, qa_s, ka_s, m_s, l_s, acc_s, st_s, *, L):
    TQ = TK = FOX_TILE
    H, DH, KA = FOX_HEADS, FOX_DH, FOX_AUG
    heads = range(H)

    t1, t2, t3 = (t.astype(F32) for t in _split3(fr_ref[...] * LOG2_E))
    r16 = lax.broadcasted_iota(jnp.int32, (16, 1), 0)
    upper = lax.broadcasted_iota(jnp.int32, (2 * DH, 1), 0) < DH
    for h in heads:
        p, hh = divmod(h, 2)
        qpair = qt_ref[2 * DH * p:2 * DH * (p + 1), :]
        keep = upper if hh == 0 else jnp.logical_not(upper)
        qa_s[h, 0:2 * DH, :] = jnp.where(keep, qpair, jnp.zeros_like(qpair))
        ones_rows = (r16 >= 3 + 3 * hh) & (r16 < 6 + 3 * hh)
        blk = jnp.where(r16 == 0, t1[h:h + 1], jnp.where(r16 == 1, t2[h:h + 1], jnp.where(
            r16 == 2, t3[h:h + 1], jnp.where(ones_rows, 1.0, 0.0))))
        qa_s[h, 2 * DH:2 * DH + 16, :] = blk.astype(BF16)
        qa_s[h, 2 * DH + 16:KA, :] = jnp.zeros((KA - 2 * DH - 16, L), BF16)
    c1, c2, c3 = _split3(fc_ref[...] * LOG2_E)
    rs, cs_ = _iota2((128, 128))
    lane = lax.broadcasted_iota(jnp.int32, (1, 128), 1)
    ones3 = jnp.where(lane < 3, 1.0, 0.0)
    for p in range(H // 2):
        def sel(i, p=p):
            hit = ((rs == 2 * p) & (cs_ == 3 + i)) | ((rs == 2 * p + 1) & (cs_ == 6 + i))
            return jnp.where(hit, -1.0, 0.0).astype(BF16)
        aug = (jnp.dot(c1, sel(0), preferred_element_type=F32) + jnp.dot(c2, sel(1), preferred_element_type=F32)
               + jnp.dot(c3, sel(2), preferred_element_type=F32) + ones3)
        ka_s[:, KA * p:KA * p + 2 * DH] = k_ref[:, 2 * DH * p:2 * DH * (p + 1)]
        ka_s[:, KA * p + 2 * DH:KA * (p + 1)] = aug.astype(BF16)

    ri, ci = _iota2((TK, TQ))
    visible = ri <= ci

    def q_block(qi, _):
        q0 = pl.multiple_of(qi * TQ, TQ)
        m_s[...] = jnp.full(m_s.shape, NEG_BIG, F32)
        l_s[...] = jnp.zeros(l_s.shape, F32)
        acc_s[...] = jnp.zeros(acc_s.shape, F32)

        def scores(slot, k0, masked):
            kt = [ka_s[pl.ds(k0, TK), KA * p:KA * (p + 1)] for p in range(H // 2)]
            for h in heads:
                s = jnp.dot(kt[h // 2], qa_s[h, :, pl.ds(q0, TQ)], preferred_element_type=F32)
                st_s[slot, h] = jnp.where(visible, s, NEG_BIG) if masked else s

        def consume(slot, k0):
            k0 = pl.multiple_of(k0, TK)
            st = [st_s[slot, h] for h in heads]
            m_old = [m_s[h:h + 1, :] for h in heads]
            m_new = [jnp.maximum(m_old[h], jnp.max(st[h], 0, keepdims=True)) for h in heads]
            pt = [jnp.exp2(st[h] - m_new[h]) for h in heads]
            pv = [jnp.dot(vt_ref[DH * h:DH * (h + 1), pl.ds(k0, TK)], pt[h].astype(BF16),
                          preferred_element_type=F32) for h in heads]
            for h in heads:
                a = jnp.exp2(m_old[h] - m_new[h])
                m_s[h:h + 1, :] = m_new[h]
                l_s[h:h + 1, :] = a * l_s[h:h + 1, :] + jnp.sum(pt[h], 0, keepdims=True)
                acc_s[DH * h:DH * (h + 1), :] = a * acc_s[DH * h:DH * (h + 1), :] + pv[h]

        def pair_step(jj, k_prev):
            ka = pl.multiple_of(2 * jj * TK, TK)
            kb = pl.multiple_of(ka + TK, TK)
            scores(1, ka, False)
            consume(0, k_prev)
            scores(0, kb, False)
            consume(1, ka)
            return kb

        scores(0, q0, True)
        k_prev = lax.fori_loop(0, qi // 2, pair_step, q0)

        @pl.when(qi % 2 == 1)
        def _():
            k_odd = pl.multiple_of((qi - 1) * TK, TK)
            scores(1, k_odd, False)
            consume(0, k_prev)
            consume(1, k_odd)

        @pl.when(qi % 2 == 0)
        def _():
            consume(0, k_prev)
        for p in range(H // 2):
            o_pair = jnp.concatenate([acc_s[DH * h:DH * (h + 1), :] / l_s[h:h + 1, :] for h in (2 * p, 2 * p + 1)], 0)
            o_ref[pl.ds(q0, TQ), 2 * DH * p:2 * DH * (p + 1)] = o_pair.T.astype(BF16)
        return 0

    lax.fori_loop(0, L // TQ, q_block, 0)


def _fox_attention(qt, k, vt, f_col, f_row):
    B, H, L = f_row.shape
    assert L % FOX_TILE == 0, L
    seq = pl.BlockSpec((L, HALF), lambda b: (b, 0))
    seq_t = pl.BlockSpec((HALF, L), lambda b: (0, b))
    return pl.pallas_call(
        functools.partial(_fox_kernel, L=L),
        grid=(B,),
        in_specs=[seq_t, seq, seq_t, pl.BlockSpec((L, 128), lambda b: (b, 0)),
                  pl.BlockSpec((None, H, L), lambda b: (b, 0, 0))],
        out_specs=seq,
        out_shape=jax.ShapeDtypeStruct((B * L, HALF), BF16),
        scratch_shapes=[pltpu.VMEM((H, FOX_AUG, L), BF16), pltpu.VMEM((L, FOX_AUG * H // 2), BF16),
                        pltpu.VMEM((H, FOX_TILE), F32), pltpu.VMEM((H, FOX_TILE), F32),
                        pltpu.VMEM((HALF, FOX_TILE), F32), pltpu.VMEM((2, H, FOX_TILE, FOX_TILE), F32)],
        compiler_params=_cparams("parallel"),
        name="fox_attention",
    )(qt, k, vt, f_col, f_row)


def _s5_matrices(a_re, a_im, b_re, b_im, c_re, c_im, d, log_dt):
    G, P, Cg, LC = S5_GROUPS, S5_STATE, S5_GROUP_CH, S5_CHUNK
    a_re, a_im, b_re, b_im, c_re, c_im = (t.astype(F32) for t in (a_re, a_im, b_re, b_im, c_re, c_im))
    dt = jnp.exp(log_dt.astype(F32))[:, None]
    mag = jnp.exp(a_re * dt)
    lam_re, lam_im = mag * jnp.cos(a_im * dt), mag * jnp.sin(a_im * dt)
    den = a_re ** 2 + a_im ** 2
    zr = ((lam_re - 1.0) * a_re + lam_im * a_im) / den
    zi = (lam_im * a_re - (lam_re - 1.0) * a_im) / den
    bb_re = zr[..., None] * b_re - zi[..., None] * b_im
    bb_im = zr[..., None] * b_im + zi[..., None] * b_re
    n = jnp.arange(LC + 1, dtype=F32)[:, None, None]
    pw_mag = jnp.exp(n * (a_re * dt)[None])
    pr, pi = pw_mag * jnp.cos(n * (a_im * dt)[None]), pw_mag * jnp.sin(n * (a_im * dt)[None])
    ein = functools.partial(jnp.einsum, precision=HIGHEST)
    cr = c_re[None] * pr[:, :, None, :] - c_im[None] * pi[:, :, None, :]
    ci = -(c_re[None] * pi[:, :, None, :] + c_im[None] * pr[:, :, None, :])
    kn = ein('ngcp,gpd->ngcd', cr[:LC], bb_re) + ein('ngcp,gpd->ngcd', ci[:LC], bb_im)
    s_idx = jnp.arange(LC)
    lag = s_idx[None, :] - s_idx[:, None]
    kt = jnp.where((lag >= 0)[:, :, None, None, None],
                   kn[jnp.clip(lag, 0, LC - 1)], 0.0)
    dmat = (lag == 0)[:, :, None, None, None] * (d.astype(F32).reshape(1, 1, G, Cg, 1)
                                                 * jnp.eye(Cg, dtype=F32)[None, None, None])
    NS, GS = S5_SLABS, S5_SLAB_GROUPS
    k6 = (kt + dmat).reshape(LC, LC, NS, GS, Cg, Cg)
    kmat = k6.transpose(2, 0, 3, 5, 1, 4).reshape(NS, LC * 128, LC * Cg)
    rr, ri = pr[LC - 1 - s_idx], pi[LC - 1 - s_idx]
    e_re = rr[..., None] * bb_re[None] - ri[..., None] * bb_im[None]
    e_im = rr[..., None] * bb_im[None] + ri[..., None] * bb_re[None]
    e6 = jnp.stack([e_re, e_im]).reshape(2, LC, NS, GS, P, Cg)
    emat = e6.transpose(2, 1, 3, 5, 0, 4).reshape(NS, LC * 128, 2 * P)
    f6 = jnp.stack([cr[1:], ci[1:]]).reshape(2, LC, NS, GS, Cg, P)
    fmat = f6.transpose(2, 0, 3, 5, 1, 4).reshape(NS, 2 * GS * P, LC * Cg)
    lr, li = pr[LC].reshape(NS, 1, GS * P), pi[LC].reshape(NS, 1, GS * P)
    lam_a = jnp.concatenate([lr, lr], axis=-1)
    lam_b = jnp.concatenate([-li, li], axis=-1)
    return kmat.astype(BF16), emat.astype(BF16), fmat.astype(BF16), lam_a, lam_b


def _s5_expand(src_ref, dst_ref, col_of, row_group, col_group):
    n_src, n_dst = src_ref.shape[1], dst_ref.shape[1]
    sr, sc = _iota2((n_src, n_dst))
    select = jnp.where(sr == col_of(sc), 1.0, 0.0).astype(BF16)
    step = 256
    for i in range(src_ref.shape[0] // step):
        rs = slice(i * step, (i + 1) * step)
        r, c = _iota2((step, n_dst))
        same = row_group(r + i * step) == col_group(c)
        full = jnp.dot(src_ref[rs, :], select, preferred_element_type=F32)
        dst_ref[rs, :] = jnp.where(same, full, 0.0).astype(BF16)


def _s5_kernel(u_ref, kc_ref, ec_ref, fc_ref, la_ref, lb_ref, y_ref, k_ref, e_ref, f_ref, e_s, hs_s, h_s, *, B):
    @pl.when(pl.program_id(1) == 0)
    def _():
        h_s[...] = jnp.zeros_like(h_s)
        gs, cg, p = S5_SLAB_GROUPS, S5_GROUP_CH, S5_STATE
        lg = lambda n: n.bit_length() - 1
        lane_group = lambda i: (i >> lg(cg)) & (gs - 1)
        state_group = lambda i: (i >> lg(p)) & (gs - 1)
        frame_ch = lambda c: ((c >> lg(gs * cg)) << lg(cg)) | (c & (cg - 1))
        part_state = lambda c: ((c >> lg(gs * p)) << lg(p)) | (c & (p - 1))
        _s5_expand(kc_ref, k_ref, frame_ch, lane_group, lane_group)
        _s5_expand(ec_ref, e_ref, part_state, lane_group, state_group)
        _s5_expand(fc_ref, f_ref, frame_ch, state_group, lane_group)

    u = u_ref[...]
    e_s[...] = jnp.dot(u, e_ref[...], preferred_element_type=F32)
    la, lb = la_ref[...], lb_ref[...]
    half = h_s.shape[1] // 2
    h = h_s[...]
    for kc in range(u.shape[0] // B):
        rs = slice(kc * B, (kc + 1) * B)
        hs_s[rs, :] = h
        swapped = jnp.concatenate([h[:, half:], h[:, :half]], axis=1)
        h = la * h + lb * swapped + e_s[rs, :]
    h_s[...] = h
    y_ref[...] = (jnp.dot(u, k_ref[...], preferred_element_type=F32)
                  + jnp.dot(hs_s[...].astype(BF16), f_ref[...], preferred_element_type=F32)).astype(BF16)


def _s5(u4, mats, B, L):
    kmat, emat, fmat, lam_a, lam_b = mats
    NS, LC = S5_SLABS, S5_CHUNK
    NK = L // LC
    assert L % LC == 0 and (NK <= S5_TILE_CHUNKS or NK % S5_TILE_CHUNKS == 0) and B % 8 == 0, (B, L)
    R, W, S2 = NK * B, LC * 128, 2 * S5_SLAB_GROUPS * S5_STATE
    ug = u4.reshape(NS, B, NK, W).transpose(0, 2, 1, 3).reshape(NS, R, W)
    TR = B * min(NK, S5_TILE_CHUNKS)
    rows = pl.BlockSpec((None, TR, W), lambda s, r: (s, r, 0))
    per_s = lambda a: pl.BlockSpec((None,) + a.shape[1:], lambda s, r: (s, 0, 0), pipeline_mode=pl.Buffered(1))
    y = pl.pallas_call(
        functools.partial(_s5_kernel, B=B),
        grid=(NS, R // TR),
        in_specs=[rows] + [per_s(a) for a in (kmat, emat, fmat, lam_a, lam_b)],
        out_specs=rows,
        out_shape=jax.ShapeDtypeStruct((NS, R, W), BF16),
        scratch_shapes=[pltpu.VMEM((W, W), BF16), pltpu.VMEM((W, S2), BF16), pltpu.VMEM((S2, W), BF16),
                        pltpu.VMEM((TR, S2), F32), pltpu.VMEM((TR, S2), F32), pltpu.VMEM((B, S2), F32)],
        compiler_params=_cparams("parallel", "arbitrary"),
        name="s5",
    )(ug, kmat, emat, fmat, lam_a, lam_b)
    return y.reshape(NS, NK, B, W).transpose(0, 2, 1, 3).reshape(NS, B * L, 128)


def _post_kernel(a_ref, ah_ref, b_ref, bh_ref, x_ref, xh_ref, wo_ref, g1_ref, beta1_ref, *rest,
                 even, tiles_per_seq, alpha):
    if even:
        wg_ref, bg_ref, *rest = rest
    wu_ref, cw_ref, cb_ref, wd_ref, g2_ref, beta2_ref, o_ref, act_s = rest
    tm = x_ref.shape[0]
    first = (pl.program_id(0) % tiles_per_seq) == 0
    a = jnp.concatenate([ah_ref[...], a_ref[...]], axis=0)
    x_in = jnp.concatenate([xh_ref[...], x_ref[...]], axis=0)
    if even:
        y = jnp.concatenate([jnp.concatenate([bh_ref[s], b_ref[s]], axis=0) for s in range(S5_SLABS)], axis=1)
        z = jax.nn.gelu(y.astype(F32))
        second = z * jax.nn.sigmoid(_mm(z, wg_ref[...]) + bg_ref[...])
    else:
        second = jnp.concatenate([bh_ref[...], b_ref[...]], axis=0)
    mix = jnp.dot(jnp.concatenate([a, second.astype(BF16)], axis=1), wo_ref[...], preferred_element_type=F32)
    x1e = _layer_norm(alpha * x_in + mix, g1_ref[...], beta1_ref[...])
    row = lax.broadcasted_iota(jnp.int32, (HALO + tm, 1), 0)
    x1e = jnp.where((row < HALO) & first, 0.0, x1e)
    x1 = x1e[HALO:]
    xe = x1e.astype(BF16)
    xb = xe[HALO:]

    def up(c):
        ue = jnp.dot(xe, wu_ref[:, c * FFN_TILE:(c + 1) * FFN_TILE], preferred_element_type=F32)
        gate = jnp.dot(xb, wu_ref[:, D_FF + c * FFN_TILE:D_FF + (c + 1) * FFN_TILE], preferred_element_type=F32)
        return ue, gate

    n_tiles = D_FF // FFN_TILE
    nxt = up(0)
    for c in range(n_tiles):
        cs = slice(c * FFN_TILE, (c + 1) * FFN_TILE)
        ue, gate = nxt
        if c + 1 < n_tiles:
            nxt = up(c + 1)
        u = cb_ref[:, cs]
        for j in range(FFN_CONV):
            u = u + cw_ref[j:j + 1, cs] * _shift_rows(ue, FFN_CONV - 1 - j)[HALO:]
        act_s[:, cs] = (jax.nn.gelu(u) * gate).astype(BF16)
    ffn = jnp.dot(act_s[...], wd_ref[...], preferred_element_type=F32)
    o_ref[...] = _layer_norm(alpha * x1 + ffn, g2_ref[...], beta2_ref[...])


def _post(a, b, x, L, w_out, ln1_g, ln1_b, w_up, conv_w, conv_b, w_down, ln2_g, ln2_b, alpha, glu=None, tm=1024):
    T = x.shape[0]
    tm = min(tm, L)
    assert L % tm == 0 and tm % HALO == 0, (L, tm)
    wo = w_out.astype(BF16)
    row1 = lambda v, n: v.reshape(1, n).astype(F32)
    slab = lambda rows, scale: pl.BlockSpec((S5_SLABS, rows, 128), scale)
    b_specs = [_rows(tm, HALF), _halo_rows(tm, HALF)]
    if glu is not None:
        b_specs = [slab(tm, lambda i: (0, i, 0)),
                   slab(HALO, lambda i: (0, jnp.maximum(i * (tm // HALO) - 1, 0), 0))]
    args = [a, a, b, b, x, x, wo, row1(ln1_g, D_MODEL), row1(ln1_b, D_MODEL)]
    specs = ([_rows(tm, HALF), _halo_rows(tm, HALF)] + b_specs + [_rows(tm, D_MODEL), _halo_rows(tm, D_MODEL)]
             + [_resident(t.shape) for t in args[6:]])
    tail = [w_up.astype(BF16), conv_w.astype(F32), row1(conv_b, D_FF), w_down.astype(BF16),
            row1(ln2_g, D_MODEL), row1(ln2_b, D_MODEL)]
    if glu is not None:
        tail = [glu[0].astype(BF16), row1(glu[1], HALF)] + tail
    return pl.pallas_call(
        functools.partial(_post_kernel, even=glu is not None, tiles_per_seq=L // tm, alpha=alpha),
        grid=(T // tm,),
        in_specs=specs + [_resident(t.shape) for t in tail],
        out_specs=_rows(tm, D_MODEL),
        out_shape=jax.ShapeDtypeStruct((T, D_MODEL), F32),
        scratch_shapes=[pltpu.VMEM((tm, D_FF), BF16)],
        compiler_params=_cparams("parallel"),
        name="post",
    )(*args, *tail)


_LORA_PAD = (128, 128, 256)
_RWKV_PAD = 3 * HALF + sum(_LORA_PAD)


def _odd_pre_kernel(x_ref, xh_ref, wrk_ref, mu_ref, w2_ref, a2_ref, g2_ref, vec_ref, bd_ref,
                    wqk_ref, cw_ref, cb_ref, wv_ref, wz_ref, wif_ref, gb_ref,
                    r_ref, k_ref, v_ref, lw_ref, kk_ref, ka_ref, g_ref, bo_ref,
                    mq_ref, mk_ref, mv_ref, mz_ref, gate_ref, *, tiles_per_seq):
    tm = x_ref.shape[0]
    first = (pl.program_id(0) % tiles_per_seq) == 0
    xh = jnp.where(first, 0.0, xh_ref[...])
    xe = jnp.concatenate([xh, x_ref[...]], axis=0).astype(BF16)
    xb = xe[HALO:]

    pe = [jnp.dot(xe, wrk_ref[:, HALF * i:HALF * (i + 1)], preferred_element_type=F32)
          for i in range(_RWKV_PAD // HALF)]
    qke = [jnp.dot(xe, wqk_ref[:, HALF * i:HALF * (i + 1)], preferred_element_type=F32) for i in range(2)]
    mv_ref[...] = jnp.dot(xb, wv_ref[...], preferred_element_type=F32).astype(BF16)
    mz_ref[...] = jnp.dot(xb, wz_ref[...], preferred_element_type=F32).astype(mz_ref.dtype)
    pre = jnp.dot(xb, wif_ref[...], preferred_element_type=F32)[:, :2 * MLSTM_HEADS] + gb_ref[...]

    def token_shift(i):
        cur, prev = pe[i][HALO:], _shift_rows(pe[i], 1)[HALO:]
        return cur + (prev - cur) * mu_ref[:, HALF * i:HALF * (i + 1)]

    r, k, v, lora = (token_shift(i) for i in range(4))
    wd = lora[:, :_LORA_PAD[0]]
    ad = lora[:, _LORA_PAD[0]:_LORA_PAD[0] + _LORA_PAD[1]]
    gd = lora[:, _LORA_PAD[0] + _LORA_PAD[1]:]
    w0, a0, k_k, k_a, r_k = (vec_ref[i:i + 1, :] for i in range(5))
    wlog = -jax.nn.softplus(-(w0 + _mm(jnp.tanh(wd), w2_ref[...]))) - 0.5
    lw_ref[...] = -jnp.exp(wlog)
    a = jax.nn.sigmoid(a0 + _mm(ad, a2_ref[...]))
    g_ref[...] = _mm(jax.nn.sigmoid(gd), g2_ref[...]).astype(g_ref.dtype)
    kk = k * k_k
    ss = _mm_exact_rhs(kk * kk, bd_ref[...], terms=2)
    kk = kk * lax.rsqrt(jnp.maximum(ss, 1e-24))
    kmod = k * (1.0 + (a - 1.0) * k_a)
    bo_ref[...] = (_mm_exact_rhs(r * kmod * r_k, bd_ref[...], terms=2) * v).astype(bo_ref.dtype)
    r_ref[...] = r.astype(r_ref.dtype)
    k_ref[...] = kmod.astype(k_ref.dtype)
    v_ref[...] = v.astype(v_ref.dtype)
    kk_ref[...] = kk.astype(kk_ref.dtype)
    ka_ref[...] = (kk * a).astype(ka_ref.dtype)

    def conv_silu(i):
        cols = slice(HALF * i, HALF * (i + 1))
        acc = cb_ref[:, cols]
        for j in range(MLSTM_CONV):
            acc = acc + cw_ref[j:j + 1, cols] * _shift_rows(qke[i], MLSTM_CONV - 1 - j)[HALO:]
        return jax.nn.silu(acc)

    mq_ref[...] = conv_silu(0).astype(mq_ref.dtype)
    mk_ref[...] = (conv_silu(1) * (MLSTM_DH ** -0.5)).astype(BF16)
    is_i = lax.broadcasted_iota(jnp.int32, pre.shape, 1) < MLSTM_HEADS
    gate_ref[...] = jnp.where(is_i, pre, jax.nn.log_sigmoid(pre))


def _head_block_ones(width, head):
    idx = jnp.arange(width) // head
    return (idx[:, None] == idx[None, :]).astype(BF16)


def _odd_pre(x, L, w_in, mu, w0, w2, a0, a2, g2, k_k, k_a, r_k, conv_w, conv_b, ib, fb, tm=1024):
    T = x.shape[0]
    tm = min(tm, L)
    assert L % tm == 0 and tm % HALO == 0, (L, tm)
    wb = w_in.astype(BF16)
    o = 3 * HALF
    sizes = (RWKV_DECAY_LORA, RWKV_AAA_LORA, RWKV_GATE_LORA)

    def pad_lora(m, axis):
        parts, s = [], o
        for sz, pd in zip(sizes, _LORA_PAD):
            piece = lax.slice_in_dim(m, s, s + sz, axis=axis)
            widths = [(0, 0)] * m.ndim
            widths[axis] = (0, pd - sz)
            parts.append(jnp.pad(piece, widths))
            s += sz
        return jnp.concatenate([lax.slice_in_dim(m, 0, o, axis=axis)] + parts, axis=axis)

    rwkv_proj = o + sum(sizes)
    wrk = pad_lora(wb[:, :rwkv_proj], 1)
    mu_p = pad_lora(mu.astype(F32).reshape(1, -1), 1)
    padr = lambda m, rows: jnp.pad(m.astype(BF16), ((0, rows - m.shape[0]), (0, 0)))
    w2p, a2p, g2p = padr(w2, _LORA_PAD[0]), padr(a2, _LORA_PAD[1]), padr(g2, _LORA_PAD[2])
    vecs = jnp.stack([w0, a0, k_k, k_a, r_k.reshape(HALF)]).astype(F32)
    vecs = jnp.pad(vecs, ((0, 8 - vecs.shape[0]), (0, 0)))
    bd = _head_block_ones(HALF, RWKV_N)
    wm = wb[:, rwkv_proj:]
    wqk, wmv, wmz = wm[:, :2 * HALF], wm[:, 2 * HALF:3 * HALF], wm[:, 3 * HALF:4 * HALF]
    wif = jnp.pad(wm[:, 4 * HALF:], ((0, 0), (0, 128 - 2 * MLSTM_HEADS)))
    gbias = jnp.concatenate([ib, fb]).astype(F32).reshape(1, 2 * MLSTM_HEADS)
    args = [x, x, wrk, mu_p, w2p, a2p, g2p, vecs, bd, wqk, conv_w.astype(F32),
            conv_b.astype(F32).reshape(1, 2 * HALF), wmv, wmz, wif, gbias]
    specs = [_rows(tm, D_MODEL), _halo_rows(tm, D_MODEL)] + [_resident(a.shape) for a in args[2:]]
    f_half = jax.ShapeDtypeStruct((T, HALF), F32)
    b_half = jax.ShapeDtypeStruct((T, HALF), BF16)
    out_shape = [b_half] * 3 + [f_half] + [b_half] * 8 + [jax.ShapeDtypeStruct((T, 2 * MLSTM_HEADS), F32)]
    out_specs = [_rows(tm, HALF)] * 12 + [_rows(tm, 2 * MLSTM_HEADS)]
    return pl.pallas_call(
        functools.partial(_odd_pre_kernel, tiles_per_seq=L // tm),
        grid=(T // tm,),
        in_specs=specs,
        out_specs=out_specs,
        out_shape=out_shape,
        compiler_params=_cparams("parallel"),
        name="odd_pre",
    )(*args)


def _rwkv_kernel(r_ref, k_ref, v_ref, lw_ref, kk_ref, ka_ref, g_ref, bo_ref, lng_ref, lnb_ref, o_ref, m_s):
    C, N, H = CHUNK, RWKV_N, RWKV_HEADS
    rows = range(r_ref.shape[0])
    units = [(b, h) for b in rows for h in range(H)]

    @pl.when(pl.program_id(1) == 0)
    def _():
        m_s[...] = jnp.zeros_like(m_s)

    ri, ci = _iota2((C, C))
    incl, strict, eye = ci <= ri, ci < ri, ci == ri
    ri2, ci2 = _iota2((C, 2 * C))
    incl2 = (ci2 & (C - 1)) <= ri2
    zeros = jnp.zeros((C, N), F32)
    mm = functools.partial(_mmp, passes=RWKV_PASSES)
    al, rt, bt, kt, bp, kp, gam, v = ({} for _ in range(8))
    for b in rows:
        lw = lw_ref[b]
        cs = _mm_exact_lhs(incl.astype(BF16), lw)
        cend = cs[C - 1:C, :]
        e_neg = jnp.exp(-cs)
        e_rem = jnp.exp(cend - cs)
        r, k, vv, kk, ka = (t[b].astype(F32) for t in (r_ref, k_ref, v_ref, kk_ref, ka_ref))
        al_b = -kk * jnp.exp(cs - lw)
        rt_b = r * jnp.exp(cs)
        gam_b = jnp.exp(cend)
        bt_b, kt_b, bp_b, kp_b = ka * e_neg, k * e_neg, ka * e_rem, k * e_rem
        for h in range(H):
            s = slice(N * h, N * (h + 1))
            u = (b, h)
            al[u], rt[u], v[u], gam[u] = al_b[:, s], rt_b[:, s], vv[:, s], gam_b[:, s]
            bt[u], kt[u], bp[u], kp[u] = bt_b[:, s], kt_b[:, s], bp_b[:, s], kp_b[:, s]
    pm = {u: mm(jnp.concatenate([al[u], rt[u]], 0), jnp.concatenate([bt[u], kt[u]], 0), 1, 1) for u in units}
    a_ab = {u: jnp.where(strict, pm[u][:C, :C], 0.0) for u in units}
    a_ak = {u: jnp.where(strict, pm[u][:C, C:], 0.0) for u in units}
    a_r = {u: jnp.where(incl2, pm[u][C:, :], 0.0) for u in units}
    w = {u: jnp.concatenate([al[u], mm(a_ak[u], v[u])], 1) for u in units}
    npow = a_ab
    levels = int(math.log2(C))
    for lvl in range(levels):
        if lvl < levels - 1:
            y = {u: mm(npow[u], jnp.concatenate([w[u], npow[u]], 1)) for u in units}
            w = {u: w[u] + y[u][:, :2 * N] for u in units}
            npow = {u: y[u][:, 2 * N:] for u in units}
        else:
            w = {u: w[u] + mm(npow[u], w[u]) for u in units}
    zv = {u: jnp.concatenate([zeros, v[u]], 1) for u in units}
    wz = {u: jnp.concatenate([w[u], zv[u]], 0) for u in units}
    qt = {u: mm(a_r[u], wz[u]) for u in units}
    qb = {u: mm(jnp.concatenate([bp[u], kp[u]], 0), wz[u], 0, 0) for u in units}
    m0 = {u: m_s[u[0], u[1]] for u in units}
    o_h = {u: mm(rt[u] + qt[u][:, :N], m0[u]) + qt[u][:, N:] for u in units}
    m_new = {u: _mmp(qb[u][:, :N] + jnp.where(eye, gam[u], 0.0), m0[u], passes=RWKV_STATE_PASSES) + qb[u][:, N:]
             for u in units}
    for u in units:
        m_s[u[0], u[1]] = m_new[u]
    om = {u: jnp.mean(o_h[u], -1, keepdims=True) for u in units}
    d = {u: o_h[u] - om[u] for u in units}
    ov = {u: jnp.mean(d[u] * d[u], -1, keepdims=True) for u in units}
    for b in rows:
        outs = [d[(b, h)] * lax.rsqrt(ov[(b, h)] + RWKV_GN_EPS) for h in range(H)]
        on = jnp.concatenate(outs, 1) * lng_ref[...] + lnb_ref[...]
        o_ref[b] = ((on + bo_ref[b].astype(F32)) * g_ref[b].astype(F32)).astype(BF16)


def _rwkv(r, k, v, lw, kk, ka, g, bonus, ln_g, ln_b, B, L):
    NC = L // CHUNK
    assert L % CHUNK == 0, L
    RB = SEQ_ROWS if B % SEQ_ROWS == 0 else 1
    seq3 = lambda t: t.reshape(B, L, HALF)
    blk = pl.BlockSpec((RB, CHUNK, HALF), lambda b, c: (b, c, 0))
    vec = pl.BlockSpec((1, HALF), lambda b, c: (0, 0))
    out = pl.pallas_call(
        _rwkv_kernel,
        grid=(B // RB, NC),
        in_specs=[blk] * 8 + [vec, vec],
        out_specs=blk,
        out_shape=jax.ShapeDtypeStruct((B, L, HALF), BF16),
        scratch_shapes=[pltpu.VMEM((RB, RWKV_HEADS, RWKV_N, RWKV_N), F32)],
        compiler_params=_cparams("parallel", "arbitrary"),
        name="rwkv7",
    )(*(seq3(t) for t in (r, k, v, lw, kk, ka, g, bonus)),
      ln_g.reshape(1, HALF).astype(F32), ln_b.reshape(1, HALF).astype(F32))
    return out.reshape(B * L, HALF)


def _mlstm_kernel(q_ref, k_ref, v_ref, z_ref, gc_ref, gr_ref, lng_ref, skip_ref, o_ref, c_s, m_s):
    C, H, DH = CHUNK, MLSTM_HEADS, MLSTM_DH
    rows = range(q_ref.shape[0])
    units = [(b, h) for b in rows for h in range(H)]

    @pl.when(pl.program_id(1) == 0)
    def _():
        c_s[...] = jnp.zeros_like(c_s)
        m_s[...] = jnp.zeros_like(m_s)

    ri, ci = _iota2((C, C))
    incl = ci <= ri
    lower, upper = incl.astype(BF16), (ri <= ci).astype(BF16)
    lane = lax.broadcasted_iota(jnp.int32, (1, DH), 1)
    ones_blk = jnp.broadcast_to(jnp.where(lane == 0, 1.0, 0.0).astype(BF16), (C, DH))
    q_all, qb, kh, vaug, i_col, i_row, b_col, b_row, m_prev = ({} for _ in range(9))
    for b in rows:
        gc, gr = gc_ref[b], gr_ref[b]
        b_cols = _mm_exact_lhs(lower, gc)
        b_rows = _mm_exact_rhs(gr, upper)
        q_all[b] = q_ref[b].astype(F32)
        for h in range(H):
            u, s = (b, h), slice(DH * h, DH * (h + 1))
            qb[u], kh[u] = q_ref[b, :, s], k_ref[b, :, s]
            vaug[u] = jnp.concatenate([v_ref[b, :, s], ones_blk], 1)
            i_col[u], i_row[u] = gc[:, h:h + 1], gr[h:h + 1, :]
            b_col[u], b_row[u] = b_cols[:, H + h:H + h + 1], b_rows[H + h:H + h + 1, :]
            m_prev[u] = m_s[b, h:h + 1, 0:1]
    c_old = {u: c_s[u[0], u[1]] for u in units}
    qk = {u: _dg(qb[u], kh[u], 1, 1) for u in units}
    qc = {u: jnp.dot(qb[u], c_old[u].astype(BF16), preferred_element_type=F32) for u in units}
    dmat = {u: jnp.where(incl, b_col[u] - b_row[u] + i_row[u], NEG_BIG) for u in units}
    inter = {u: b_col[u] + m_prev[u] for u in units}
    m_t = {u: jnp.maximum(inter[u], jnp.max(dmat[u], -1, keepdims=True)) for u in units}
    s = {u: (qk[u] * jnp.exp(dmat[u] - m_t[u])).astype(BF16) for u in units}
    sv = {u: jnp.dot(s[u], vaug[u], preferred_element_type=F32) for u in units}
    b_end = {u: b_col[u][C - 1:C, :] for u in units}
    m_new = {u: jnp.maximum(b_end[u] + m_prev[u], jnp.max(b_end[u] - b_row[u] + i_row[u], -1, keepdims=True))
             for u in units}
    kw = {u: (jnp.exp(b_end[u] - b_col[u] + i_col[u] - m_new[u]) * kh[u].astype(F32)).astype(BF16) for u in units}
    kv = {u: _dg(kw[u], vaug[u], 0, 0) for u in units}
    for u in units:
        b, h = u
        c_s[b, h] = jnp.exp(b_end[u] + m_prev[u] - m_new[u]) * c_old[u] + kv[u]
        m_s[b, h:h + 1, :] = jnp.broadcast_to(m_new[u], (1, 128))
    nd = {u: jnp.exp(inter[u] - m_t[u]) * qc[u] + sv[u] for u in units}
    den = {u: jnp.maximum(jnp.abs(nd[u][:, DH:DH + 1]), jnp.exp(-m_t[u])) for u in units}
    hid = {u: nd[u][:, :DH] / den[u] for u in units}
    hm = {u: jnp.mean(hid[u], -1, keepdims=True) for u in units}
    d = {u: hid[u] - hm[u] for u in units}
    hv = {u: jnp.mean(d[u] * d[u], -1, keepdims=True) for u in units}
    for b in rows:
        outs = [d[(b, h)] * lax.rsqrt(hv[(b, h)] + LN_EPS) for h in range(H)]
        hn = jnp.concatenate(outs, 1) * lng_ref[...]
        o_ref[b] = ((hn + skip_ref[...] * q_all[b]) * jax.nn.silu(z_ref[b].astype(F32))).astype(BF16)


def _mlstm(q, k, v, z, gates, ln_g, skip, B, L):
    NC, H2 = L // CHUNK, 2 * MLSTM_HEADS
    assert L % CHUNK == 0, L
    RB = MLSTM_SEQ_ROWS if B % MLSTM_SEQ_ROWS == 0 else 1
    seq3 = lambda t: t.reshape(B, L, t.shape[-1])
    blk = pl.BlockSpec((RB, CHUNK, HALF), lambda b, c: (b, c, 0))
    vec = pl.BlockSpec((1, HALF), lambda b, c: (0, 0))
    g_rows = gates.reshape(B, NC, CHUNK, H2).transpose(0, 1, 3, 2)
    out = pl.pallas_call(
        _mlstm_kernel,
        grid=(B // RB, NC),
        in_specs=[blk, blk, blk, blk, pl.BlockSpec((RB, CHUNK, H2), lambda b, c: (b, c, 0)),
                  pl.BlockSpec((RB, None, H2, CHUNK), lambda b, c: (b, c, 0, 0)), vec, vec],
        out_specs=blk,
        out_shape=jax.ShapeDtypeStruct((B, L, HALF), BF16),
        scratch_shapes=[pltpu.VMEM((RB, MLSTM_HEADS, MLSTM_DH, 2 * MLSTM_DH), F32),
                        pltpu.VMEM((RB, 8, 128), F32)],
        compiler_params=_cparams("parallel", "arbitrary"),
        name="mlstm",
    )(seq3(q), seq3(k), seq3(v), seq3(z), seq3(gates), g_rows,
      ln_g.reshape(1, HALF).astype(F32), skip.reshape(1, HALF).astype(F32))
    return out.reshape(B * L, HALF)


def kernel(x, ev_w_in, ev_fox_fb, ev_s5_a_re, ev_s5_a_im, ev_s5_b_re, ev_s5_b_im, ev_s5_c_re, ev_s5_c_im, ev_s5_d, ev_s5_log_dt, ev_s5_w_glu, ev_s5_b_glu, ev_w_out, od_w_in, od_rwkv_mu, od_rwkv_w0, od_rwkv_w2, od_rwkv_a0, od_rwkv_a2, od_rwkv_g2, od_rwkv_k_k, od_rwkv_k_a, od_rwkv_r_k, od_rwkv_ln_g, od_rwkv_ln_b, od_mlstm_conv_w, od_mlstm_conv_b, od_mlstm_ib, od_mlstm_fb, od_mlstm_ln_g, od_mlstm_skip, od_w_out, ln1_g, ln1_b, ffn_w_up, ffn_conv_w, ffn_conv_b, ffn_w_down, ln2_g, ln2_b):
    B, L, _ = x.shape
    depth = ln1_g.shape[0]
    alpha = float((2 * depth) ** 0.25)
    h = x.reshape(B * L, D_MODEL).astype(F32)
    for layer in range(depth):
        i = layer // 2
        if layer % 2 == 0:
            qt, k, vt, fg, u = _even_pre(h, ev_w_in[i])
            f_row = _fox_gate(fg.reshape(B, L, FOX_HEADS).transpose(0, 2, 1), ev_fox_fb[i].astype(F32))
            f_col = jnp.pad(f_row.transpose(0, 2, 1).reshape(B * L, FOX_HEADS), ((0, 0), (0, 128 - FOX_HEADS)))
            fox = _fox_attention(qt, k, vt, f_col, f_row)
            mats = _s5_matrices(ev_s5_a_re[i], ev_s5_a_im[i], ev_s5_b_re[i], ev_s5_b_im[i],
                                ev_s5_c_re[i], ev_s5_c_im[i], ev_s5_d[i], ev_s5_log_dt[i])
            y = _s5(u, mats, B, L)
            mixed, glu, w_out = (fox, y), (ev_s5_w_glu[i], ev_s5_b_glu[i]), ev_w_out[i]
        else:
            (r, k, v, lw, kk, ka, g, bonus, mq, mk, mv, mz, gates) = _odd_pre(
                h, L, od_w_in[i], od_rwkv_mu[i], od_rwkv_w0[i], od_rwkv_w2[i], od_rwkv_a0[i], od_rwkv_a2[i],
                od_rwkv_g2[i], od_rwkv_k_k[i], od_rwkv_k_a[i], od_rwkv_r_k[i], od_mlstm_conv_w[i],
                od_mlstm_conv_b[i], od_mlstm_ib[i], od_mlstm_fb[i])
            c = _rwkv(r, k, v, lw, kk, ka, g, bonus, od_rwkv_ln_g[i], od_rwkv_ln_b[i], B, L)
            dm = _mlstm(mq, mk, mv, mz, gates, od_mlstm_ln_g[i], od_mlstm_skip[i], B, L)
            mixed, glu, w_out = (c, dm), None, od_w_out[i]
        h = _post(mixed[0], mixed[1], h, L, w_out, ln1_g[layer], ln1_b[layer], ffn_w_up[layer], ffn_conv_w[layer],
                  ffn_conv_b[layer], ffn_w_down[layer], ln2_g[layer], ln2_b[layer], alpha, glu=glu)
    return h.reshape(B, L, D_MODEL).astype(x.dtype)
```
